```python
import math
import jax, jax.numpy as jnp
from jax import lax
import numpy as np

D_MODEL = 1024
BATCH = 8
SEQ = 2048
DEPTH = 2

CHUNK = 64
Q_BLOCK = 128
A_WIDTH = D_MODEL // 2
CONV_WIDTH = 31
B_WIDTH = D_MODEL // 2
LRU_BLOCKS = 8
LRU_BLOCK_DIM = B_WIDTH // LRU_BLOCKS
LRU_CONV_WIDTH = 4
LRU_C = 8.0
IN_WIDTH = 2 * A_WIDTH + 2 * B_WIDTH
DIFF_HEADS = D_MODEL // 128
DIFF_HEAD_DIM = 64
DIFF_V_DIM = 2 * DIFF_HEAD_DIM
QK_WIDTH = DIFF_HEADS * 2 * DIFF_HEAD_DIM
V_WIDTH = DIFF_HEADS * DIFF_V_DIM
ROPE_THETA = 10000.0
D_FF = ((8 * D_MODEL // 3 + 255) // 256) * 256
LN_EPS = 1e-5
DN_ALPHA = (2 * DEPTH) ** 0.25
DN_BETA = (8 * DEPTH) ** -0.25
N_EVEN = (DEPTH + 1) // 2
N_ODD = DEPTH // 2
NEG_INF = -1e30

kernel_name = 'hybrid_conv_lru_diffattn_streaming_encoder'


def layer_norm(x, g, b):
    xf = x.astype(jnp.float32)
    mu = jnp.mean(xf, axis=-1, keepdims=True)
    var = jnp.mean(jnp.square(xf - mu), axis=-1, keepdims=True)
    return ((xf - mu) * lax.rsqrt(var + LN_EPS)).astype(x.dtype) * g + b


def rms_norm(x, g):
    xf = x.astype(jnp.float32)
    return (xf * lax.rsqrt(jnp.mean(jnp.square(xf), axis=-1, keepdims=True) + LN_EPS)).astype(x.dtype) * g


def causal_depthwise_conv(x, w, b):
    k, c = w.shape
    y = lax.conv_general_dilated(x, w[:, None, :].astype(x.dtype), window_strides=(1,),
                                 padding=[(k - 1, 0)], dimension_numbers=('NWC', 'WIO', 'NWC'),
                                 feature_group_count=c)
    return y + b


def _linear_recurrence_combine(c1, c2):
    a1, u1 = c1
    a2, u2 = c2
    return a1 * a2, a2 * u1 + u2


def rg_lru(x, w_a, b_a, w_x, b_x, lam):
    bsz, seq, width = x.shape
    xb = x.reshape(bsz, seq, LRU_BLOCKS, LRU_BLOCK_DIM)
    gate_r = jax.nn.sigmoid((jnp.einsum('bsgi,gij->bsgj', xb, w_a).reshape(bsz, seq, width) + b_a).astype(jnp.float32))
    gate_i = jax.nn.sigmoid((jnp.einsum('bsgi,gij->bsgj', xb, w_x).reshape(bsz, seq, width) + b_x).astype(jnp.float32))
    log_a = -LRU_C * gate_r * jax.nn.softplus(-lam.astype(jnp.float32))
    a = jnp.exp(log_a)
    u = jnp.sqrt(-jnp.expm1(2.0 * log_a)) * (gate_i * x.astype(jnp.float32))
    _, h = lax.associative_scan(_linear_recurrence_combine, (a, u), axis=1)
    return h.astype(x.dtype)


def conv_lru_mixer(x, w_in, b_in, conv_w, conv_b, cnorm_g, cnorm_b, lru_conv_w, lru_conv_b,
                   w_a, b_a, w_x, b_x, lru_lambda, w_out):
    h = x @ w_in + b_in
    a_val = h[..., :A_WIDTH]
    a_gate = h[..., A_WIDTH:2 * A_WIDTH]
    b_gate = h[..., 2 * A_WIDTH:2 * A_WIDTH + B_WIDTH]
    b_rec = h[..., 2 * A_WIDTH + B_WIDTH:]
    ya = jax.nn.silu(layer_norm(causal_depthwise_conv(a_val * jax.nn.sigmoid(a_gate), conv_w, conv_b),
                                cnorm_g, cnorm_b))
    yb = rg_lru(causal_depthwise_conv(b_rec, lru_conv_w, lru_conv_b), w_a, b_a, w_x, b_x, lru_lambda) \
        * jax.nn.gelu(b_gate)
    return jnp.concatenate([ya, yb], axis=-1) @ w_out


def rotate_half(t):
    half = t.shape[-1] // 2
    return jnp.concatenate([-t[..., half:], t[..., :half]], axis=-1)


def diff_attention_mixer(x, w_qkv, lq1, lk1, lq2, lk2, subln_g, w_out, lambda_init):
    bsz, seq, _ = x.shape
    qkv = x @ w_qkv
    q = qkv[..., :QK_WIDTH].reshape(bsz, seq, DIFF_HEADS, 2, DIFF_HEAD_DIM)
    k = qkv[..., QK_WIDTH:2 * QK_WIDTH].reshape(bsz, seq, DIFF_HEADS, 2, DIFF_HEAD_DIM)
    v = qkv[..., 2 * QK_WIDTH:].reshape(bsz, seq, DIFF_HEADS, DIFF_V_DIM)
    pos = jnp.arange(seq, dtype=jnp.float32)
    inv_freq = ROPE_THETA ** (-jnp.arange(0, DIFF_HEAD_DIM, 2, dtype=jnp.float32) / DIFF_HEAD_DIM)
    ang = pos[:, None] * inv_freq[None, :]
    ang = jnp.concatenate([ang, ang], axis=-1)
    cos = jnp.cos(ang).astype(x.dtype)[:, None, None, :]
    sin = jnp.sin(ang).astype(x.dtype)[:, None, None, :]
    q = (q * cos + rotate_half(q) * sin) * (DIFF_HEAD_DIM ** -0.5)
    k = k * cos + rotate_half(k) * sin
    q = q.transpose(0, 2, 3, 1, 4)
    k = k.transpose(0, 2, 3, 1, 4)
    v = v.transpose(0, 2, 1, 3)
    lam = (jnp.exp(jnp.sum(lq1.astype(jnp.float32) * lk1.astype(jnp.float32)))
           - jnp.exp(jnp.sum(lq2.astype(jnp.float32) * lk2.astype(jnp.float32))) + lambda_init)
    chunk_id = jnp.arange(seq) // CHUNK
    outs = []
    for qb in range(seq // Q_BLOCK):
        s0, s1 = qb * Q_BLOCK, (qb + 1) * Q_BLOCK
        scores = jnp.einsum('bhmqd,bhmkd->bhmqk', q[:, :, :, s0:s1], k[:, :, :, :s1],
                            preferred_element_type=jnp.float32)
        mask = chunk_id[s0:s1, None] >= chunk_id[None, :s1]
        p = jax.nn.softmax(jnp.where(mask, scores, NEG_INF), axis=-1)
        attn = p[:, :, 0] - lam * p[:, :, 1]
        outs.append(jnp.einsum('bhqk,bhkd->bhqd', attn.astype(v.dtype), v[:, :, :s1]))
    o = jnp.concatenate(outs, axis=2)
    o = rms_norm(o, subln_g) * (1.0 - lambda_init)
    o = o.transpose(0, 2, 1, 3).reshape(bsz, seq, V_WIDTH)
    return o @ w_out


def swiglu(x, w_gate, w_up, w_down):
    return (jax.nn.silu(x @ w_gate) * (x @ w_up)) @ w_down


def setup_inputs(seed: int = 0) -> dict:
    key = jax.random.key(seed)
    ks = iter(jax.random.split(key, 40))
    f32 = jnp.float32

    def nrm(shape, scale):
        return scale * jax.random.normal(next(ks), shape, f32)

    def gain(shape):
        return 1.0 + nrm(shape, 0.02)

    a_pow = jax.random.uniform(next(ks), (N_EVEN, B_WIDTH), f32, minval=0.9, maxval=0.999)
    a0 = a_pow ** (1.0 / LRU_C)
    lru_lambda = jnp.log(a0) - jnp.log1p(-a0)
    return {
        'x': nrm((BATCH, SEQ, D_MODEL), 1.0),
        'even_w_in': nrm((N_EVEN, D_MODEL, IN_WIDTH), D_MODEL ** -0.5),
        'even_b_in': nrm((N_EVEN, IN_WIDTH), 0.01),
        'even_conv_w': nrm((N_EVEN, CONV_WIDTH, A_WIDTH), CONV_WIDTH ** -0.5),
        'even_conv_b': nrm((N_EVEN, A_WIDTH), 0.01),
        'even_cnorm_g': gain((N_EVEN, A_WIDTH)),
        'even_cnorm_b': nrm((N_EVEN, A_WIDTH), 0.01),
        'even_lru_conv_w': nrm((N_EVEN, LRU_CONV_WIDTH, B_WIDTH), LRU_CONV_WIDTH ** -0.5),
        'even_lru_conv_b': nrm((N_EVEN, B_WIDTH), 0.01),
        'even_w_a': nrm((N_EVEN, LRU_BLOCKS, LRU_BLOCK_DIM, LRU_BLOCK_DIM), LRU_BLOCK_DIM ** -0.5),
        'even_b_a': nrm((N_EVEN, B_WIDTH), 0.01),
        'even_w_x': nrm((N_EVEN, LRU_BLOCKS, LRU_BLOCK_DIM, LRU_BLOCK_DIM), LRU_BLOCK_DIM ** -0.5),
        'even_b_x': nrm((N_EVEN, B_WIDTH), 0.01),
        'even_lru_lambda': lru_lambda,
        'even_w_out': nrm((N_EVEN, A_WIDTH + B_WIDTH, D_MODEL), DN_BETA * (A_WIDTH + B_WIDTH) ** -0.5),
        'odd_w_qkv': nrm((N_ODD, D_MODEL, 2 * QK_WIDTH + V_WIDTH), D_MODEL ** -0.5),
        'odd_lambda_q1': nrm((N_ODD, DIFF_HEAD_DIM), 0.1),
        'odd_lambda_k1': nrm((N_ODD, DIFF_HEAD_DIM), 0.1),
        'odd_lambda_q2': nrm((N_ODD, DIFF_HEAD_DIM), 0.1),
        'odd_lambda_k2': nrm((N_ODD, DIFF_HEAD_DIM), 0.1),
        'odd_subln_g': gain((N_ODD, DIFF_V_DIM)),
        'odd_w_out': nrm((N_ODD, V_WIDTH, D_MODEL), DN_BETA * V_WIDTH ** -0.5),
        'mix_ln_g': gain((DEPTH, D_MODEL)),
        'mix_ln_b': nrm((DEPTH, D_MODEL), 0.01),
        'ffn_w_gate': nrm((DEPTH, D_MODEL, D_FF), D_MODEL ** -0.5),
        'ffn_w_up': nrm((DEPTH, D_MODEL, D_FF), D_MODEL ** -0.5),
        'ffn_w_down': nrm((DEPTH, D_FF, D_MODEL), DN_BETA * D_FF ** -0.5),
        'ffn_ln_g': gain((DEPTH, D_MODEL)),
        'ffn_ln_b': nrm((DEPTH, D_MODEL), 0.01),
    }


def reference(x, even_w_in, even_b_in, even_conv_w, even_conv_b, even_cnorm_g, even_cnorm_b,
              even_lru_conv_w, even_lru_conv_b, even_w_a, even_b_a, even_w_x, even_b_x,
              even_lru_lambda, even_w_out, odd_w_qkv, odd_lambda_q1, odd_lambda_k1,
              odd_lambda_q2, odd_lambda_k2, odd_subln_g, odd_w_out, mix_ln_g, mix_ln_b,
              ffn_w_gate, ffn_w_up, ffn_w_down, ffn_ln_g, ffn_ln_b):
    for layer in range(DEPTH):
        if layer % 2 == 0:
            e = layer // 2
            y = conv_lru_mixer(x, even_w_in[e], even_b_in[e], even_conv_w[e], even_conv_b[e],
                               even_cnorm_g[e], even_cnorm_b[e], even_lru_conv_w[e], even_lru_conv_b[e],
                               even_w_a[e], even_b_a[e], even_w_x[e], even_b_x[e],
                               even_lru_lambda[e], even_w_out[e])
        else:
            o = layer // 2
            lambda_init = 0.8 - 0.6 * math.exp(-0.3 * layer)
            y = diff_attention_mixer(x, odd_w_qkv[o], odd_lambda_q1[o], odd_lambda_k1[o],
                                     odd_lambda_q2[o], odd_lambda_k2[o], odd_subln_g[o],
                                     odd_w_out[o], lambda_init)
        x = layer_norm(DN_ALPHA * x + y, mix_ln_g[layer], mix_ln_b[layer])
        x = layer_norm(DN_ALPHA * x + swiglu(x, ffn_w_gate[layer], ffn_w_up[layer], ffn_w_down[layer]),
                       ffn_ln_g[layer], ffn_ln_b[layer])
    return x
```

```python
import functools
import math

import jax
import jax.numpy as jnp
from jax import lax
from jax.experimental import pallas as pl
from jax.experimental.pallas import tpu as pltpu

F32 = jnp.float32
BF16 = jnp.bfloat16

D_MODEL = 1024
BATCH = 8
SEQ = 2048
DEPTH = 2
CHUNK = 64
A_WIDTH = 512
B_WIDTH = 512
CONV_WIDTH = 31
LRU_BLOCKS = 8
LRU_BLOCK_DIM = 64
LRU_CONV_WIDTH = 4
LRU_C = 8.0
IN_WIDTH = 2 * A_WIDTH + 2 * B_WIDTH
DIFF_HEADS = 8
DIFF_HEAD_DIM = 64
DIFF_V_DIM = 128
QK_WIDTH = 1024
V_WIDTH = 1024
ROPE_THETA = 10000.0
D_FF = 2816
LN_EPS = 1e-5
DN_ALPHA = (2 * DEPTH) ** 0.25
NEG_INF = -1e30

LANES = 128
VMEM_LIMIT = 56 * 1024 * 1024

MIX_TS = 512
CONV_HALO = 32
CONV_ROWS = 32
LRU_HALO = 8
POST_TM = 512
FF_CHUNK = 256
QKV_TM = 512
ATT_TQ = 256
ATT_TK = 256


def _ln(x, g, b):
    mu = jnp.mean(x, axis=-1, keepdims=True)
    xc = x - mu
    var = jnp.mean(xc * xc, axis=-1, keepdims=True)
    return xc * lax.rsqrt(var + LN_EPS) * g + b


def _dot(a, b):
    return jnp.dot(a, b, preferred_element_type=F32)


def _mixer_kernel(x_ref, w_in_ref, b_in_ref, conv_w_ref, conv_b_ref, cn_g_ref, cn_b_ref,
                  lconv_w_ref, lconv_b_ref, w_gate_ref, b_gate_ref, lam_ref,
                  out_ref, a_ext, b_ext, h_carry, bgate_ref, gates_ref):
    ts = MIX_TS
    s = pl.program_id(1)

    @pl.when(s == 0)
    def _():
        a_ext[0:CONV_HALO, :] = jnp.zeros((CONV_HALO, A_WIDTH), F32)
        b_ext[0:LRU_HALO, :] = jnp.zeros((LRU_HALO, B_WIDTH), F32)
        h_carry[...] = jnp.zeros_like(h_carry)

    x = x_ref[0]
    h = _dot(x.astype(BF16), w_in_ref[...]) + b_in_ref[...]
    a_val = h[:, 0:A_WIDTH]
    a_gate = h[:, A_WIDTH:2 * A_WIDTH]
    bgate_ref[...] = h[:, 2 * A_WIDTH:2 * A_WIDTH + B_WIDTH]
    a_ext[CONV_HALO:CONV_HALO + ts, :] = a_val * jax.nn.sigmoid(a_gate)
    b_ext[LRU_HALO:LRU_HALO + ts, :] = h[:, 2 * A_WIDTH + B_WIDTH:]

    off = CONV_HALO - (CONV_WIDTH - 1)

    def conv_body(i, carry):
        base = pl.multiple_of(i * CONV_ROWS, CONV_ROWS)
        win = a_ext[pl.ds(base, CONV_ROWS + CONV_HALO), :]
        acc = jnp.broadcast_to(conv_b_ref[...], (CONV_ROWS, A_WIDTH))
        for r in range(8):
            part = None
            for j in range(CONV_WIDTH):
                if (off + j) % 8 == r:
                    term = conv_w_ref[j:j + 1, :] * win[off + j:off + j + CONV_ROWS, :]
                    part = term if part is None else part + term
            acc = acc + part
        ya = jax.nn.silu(_ln(acc, cn_g_ref[...], cn_b_ref[...]))
        out_ref[0, pl.ds(base, CONV_ROWS), 0:A_WIDTH] = ya.astype(BF16)
        return carry

    lax.fori_loop(0, ts // CONV_ROWS, conv_body, 0)
    a_ext[0:CONV_HALO, :] = a_ext[ts:ts + CONV_HALO, :]

    loff = LRU_HALO - (LRU_CONV_WIDTH - 1)
    xc = jnp.broadcast_to(lconv_b_ref[...], (ts, B_WIDTH))
    for j in range(LRU_CONV_WIDTH):
        xc = xc + lconv_w_ref[j:j + 1, :] * b_ext[loff + j:loff + j + ts, :]
    b_ext[0:LRU_HALO, :] = b_ext[ts:ts + LRU_HALO, :]
    gates_ref[...] = _dot(xc.astype(BF16), w_gate_ref[...]) + b_gate_ref[...]
    b_ext[LRU_HALO:LRU_HALO + ts, :] = xc

    t_idx = lax.broadcasted_iota(jnp.int32, (ts, LANES), 0)
    for g in range(B_WIDTH // LANES):
        sl = slice(g * LANES, (g + 1) * LANES)
        xg = b_ext[LRU_HALO:LRU_HALO + ts, sl]
        gate_r = jax.nn.sigmoid(gates_ref[:, sl])
        gate_i = jax.nn.sigmoid(gates_ref[:, B_WIDTH + g * LANES:B_WIDTH + (g + 1) * LANES])
        lam = lam_ref[:, sl]
        neg = -lam
        softplus = jnp.maximum(neg, 0.0) + jnp.log1p(jnp.exp(-jnp.abs(neg)))
        log_a = (-LRU_C * gate_r) * softplus
        a_cum = jnp.exp(log_a)
        th = jnp.tanh(log_a)
        u_cum = jnp.sqrt(-2.0 * th / (1.0 - th)) * (gate_i * xg)
        step = 1
        while step < ts:
            if step < 8:
                keep = t_idx >= step
                a_sh = jnp.where(keep, pltpu.roll(a_cum, step, axis=0), 1.0)
                u_sh = jnp.where(keep, pltpu.roll(u_cum, step, axis=0), 0.0)
            else:
                a_sh = jnp.concatenate([jnp.ones((step, LANES), F32), a_cum[:ts - step]], axis=0)
                u_sh = jnp.concatenate([jnp.zeros((step, LANES), F32), u_cum[:ts - step]], axis=0)
            u_cum = a_cum * u_sh + u_cum
            a_cum = a_cum * a_sh
            step *= 2
        hg = a_cum * h_carry[0:1, sl] + u_cum
        h_carry[:, sl] = jnp.broadcast_to(hg[ts - 1:ts, :], (8, LANES))
        yb = hg * jax.nn.gelu(bgate_ref[:, sl])
        out_ref[0, :, A_WIDTH + g * LANES:A_WIDTH + (g + 1) * LANES] = yb.astype(BF16)


def _mixer_call(x, w_in, b_in, conv_w, conv_b, cn_g, cn_b, lconv_w, lconv_b, w_gate, b_gate, lam):
    ts = MIX_TS
    const = lambda b, s: (0, 0)
    full = lambda a: pl.BlockSpec(a.shape, const)
    return pl.pallas_call(
        _mixer_kernel,
        grid=(BATCH, SEQ // ts),
        in_specs=[pl.BlockSpec((1, ts, D_MODEL), lambda b, s: (b, s, 0)),
                  full(w_in), full(b_in), full(conv_w), full(conv_b), full(cn_g), full(cn_b),
                  full(lconv_w), full(lconv_b), full(w_gate), full(b_gate), full(lam)],
        out_specs=pl.BlockSpec((1, ts, A_WIDTH + B_WIDTH), lambda b, s: (b, s, 0)),
        out_shape=jax.ShapeDtypeStruct((BATCH, SEQ, A_WIDTH + B_WIDTH), BF16),
        scratch_shapes=[pltpu.VMEM((CONV_HALO + ts, A_WIDTH), F32),
                        pltpu.VMEM((LRU_HALO + ts, B_WIDTH), F32),
                        pltpu.VMEM((8, B_WIDTH), F32),
                        pltpu.VMEM((ts, B_WIDTH), F32),
                        pltpu.VMEM((ts, 2 * B_WIDTH), F32)],
        compiler_params=pltpu.CompilerParams(
            dimension_semantics=("arbitrary", "arbitrary"), vmem_limit_bytes=VMEM_LIMIT),
        name="mixer0",
    )(x, w_in, b_in, conv_w, conv_b, cn_g, cn_b, lconv_w, lconv_b, w_gate, b_gate, lam)


def _post_kernel(m_ref, x_ref, w_out_ref, g1_ref, b1_ref, wg_ref, wu_ref, wd_ref, g2_ref, b2_ref,
                 out_ref, x1_ref, xb_ref, acc_ref):
    y = _dot(m_ref[...], w_out_ref[...])
    x1 = _ln(DN_ALPHA * x_ref[...] + y, g1_ref[...], b1_ref[...])
    x1_ref[...] = x1
    xb_ref[...] = x1.astype(BF16)
    for c in range(D_FF // FF_CHUNK):
        cs = slice(c * FF_CHUNK, (c + 1) * FF_CHUNK)
        gate = _dot(xb_ref[...], wg_ref[:, cs])
        up = _dot(xb_ref[...], wu_ref[:, cs])
        act = (jax.nn.silu(gate) * up).astype(BF16)
        contrib = _dot(act, wd_ref[cs, :])
        if c == 0:
            acc_ref[...] = contrib
        else:
            acc_ref[...] += contrib
    out_ref[...] = _ln(DN_ALPHA * x1_ref[...] + acc_ref[...], g2_ref[...], b2_ref[...])


def _post_call(m, x, w_out, g1, b1, wg, wu, wd, g2, b2):
    tm = POST_TM
    rows = m.shape[0]
    const = lambda i: (0, 0)
    full = lambda a: pl.BlockSpec(a.shape, const, pipeline_mode=pl.Buffered(1))
    return pl.pallas_call(
        _post_kernel,
        grid=(rows // tm,),
        in_specs=[pl.BlockSpec((tm, D_MODEL), lambda i: (i, 0)),
                  pl.BlockSpec((tm, D_MODEL), lambda i: (i, 0)),
                  full(w_out), full(g1), full(b1), full(wg), full(wu), full(wd), full(g2), full(b2)],
        out_specs=pl.BlockSpec((tm, D_MODEL), lambda i: (i, 0)),
        out_shape=jax.ShapeDtypeStruct((rows, D_MODEL), F32),
        scratch_shapes=[pltpu.VMEM((tm, D_MODEL), F32),
                        pltpu.VMEM((tm, D_MODEL), BF16),
                        pltpu.VMEM((tm, D_MODEL), F32)],
        compiler_params=pltpu.CompilerParams(
            dimension_semantics=("arbitrary",), vmem_limit_bytes=VMEM_LIMIT),
        name="post",
    )(m, x, w_out, g1, b1, wg, wu, wd, g2, b2)


def _qkv_kernel(x_ref, w_ref, cos_ref, sin_ref, q_ref, k_ref, v_ref):
    xb = x_ref[...].astype(BF16)
    cos = cos_ref[...]
    sin = sin_ref[...]
    for g in range(2 * QK_WIDTH // LANES):
        t = _dot(xb, w_ref[:, g * LANES:(g + 1) * LANES])
        r = t * cos + pltpu.roll(t, LANES // 2, axis=1) * sin
        if g < QK_WIDTH // LANES:
            q_ref[:, g * LANES:(g + 1) * LANES] = (r * (DIFF_HEAD_DIM ** -0.5)).astype(BF16)
        else:
            gk = g - QK_WIDTH // LANES
            k_ref[:, gk * LANES:(gk + 1) * LANES] = r.astype(BF16)
    v_ref[...] = _dot(xb, w_ref[:, 2 * QK_WIDTH:]).astype(BF16)


def _qkv_call(x, w, cos, sin):
    tm = QKV_TM
    rows = x.shape[0]
    pos_blocks = SEQ // tm
    row_spec = pl.BlockSpec((tm, D_MODEL), lambda i: (i, 0))
    tab_spec = pl.BlockSpec((tm, LANES), lambda i: (i % pos_blocks, 0))
    out = jax.ShapeDtypeStruct((rows, QK_WIDTH), BF16)
    return pl.pallas_call(
        _qkv_kernel,
        grid=(rows // tm,),
        in_specs=[row_spec,
                  pl.BlockSpec(w.shape, lambda i: (0, 0), pipeline_mode=pl.Buffered(1)),
                  tab_spec, tab_spec],
        out_specs=[row_spec, row_spec, row_spec],
        out_shape=[out, out, out],
        compiler_params=pltpu.CompilerParams(
            dimension_semantics=("arbitrary",), vmem_limit_bytes=VMEM_LIMIT),
        name="qkv",
    )(x, w, cos, sin)


def _attn_kernel(lq1_ref, lk1_ref, lq2_ref, lk2_ref, g_ref, q_ref, k_ref, v_ref, o_ref,
                 s_ref, *, lambda_init):
    tq, tk = ATT_TQ, ATT_TK
    lane = lax.broadcasted_iota(jnp.int32, (1, LANES), 1)
    is_map1 = ((lane // 32) % 2) == 0
    lam = (jnp.exp(jnp.sum(lq1_ref[...] * lk1_ref[...], axis=-1, keepdims=True))
           - jnp.exp(jnp.sum(lq2_ref[...] * lk2_ref[...], axis=-1, keepdims=True)) + lambda_init)
    row_chunk = lax.broadcasted_iota(jnp.int32, (tq, tk), 0) // CHUNK
    col_chunk = lax.broadcasted_iota(jnp.int32, (tq, tk), 1) // CHUNK
    diag_visible = row_chunk >= col_chunk
    zero = jnp.zeros((), BF16)

    def scores(qm, j):
        kj = k_ref[0, pl.ds(pl.multiple_of(j * tk, tk), tk), :]
        return lax.dot_general(qm, kj, (((1,), (1,)), ((), ())), preferred_element_type=F32)

    def q_block(i, carry):
        q = q_ref[0, pl.ds(pl.multiple_of(i * tq, tq), tq), :]
        q1 = jnp.where(is_map1, q, zero)
        q2 = jnp.where(is_map1, zero, q)

        def sweep_max(j, m):
            m1, m2 = m
            s1 = scores(q1, j)
            s2 = scores(q2, j)
            s_ref[0, j] = s1
            s_ref[1, j] = s2
            return (jnp.maximum(m1, jnp.max(s1, axis=-1, keepdims=True)),
                    jnp.maximum(m2, jnp.max(s2, axis=-1, keepdims=True)))

        m0 = jnp.full((tq, 1), NEG_INF, F32)
        m1, m2 = lax.fori_loop(0, i, sweep_max, (m0, m0))
        s1 = jnp.where(diag_visible, scores(q1, i), NEG_INF)
        s2 = jnp.where(diag_visible, scores(q2, i), NEG_INF)
        s_ref[0, i] = s1
        s_ref[1, i] = s2
        m1 = jnp.maximum(m1, jnp.max(s1, axis=-1, keepdims=True))
        m2 = jnp.maximum(m2, jnp.max(s2, axis=-1, keepdims=True))

        def sweep_pv(j, c):
            l1, l2, acc1, acc2 = c
            vj = v_ref[0, pl.ds(pl.multiple_of(j * tk, tk), tk), :]
            e1 = jnp.exp(s_ref[0, j] - m1)
            e2 = jnp.exp(s_ref[1, j] - m2)
            l1 = l1 + jnp.sum(e1, axis=-1, keepdims=True)
            l2 = l2 + jnp.sum(e2, axis=-1, keepdims=True)
            acc1 = acc1 + _dot(e1.astype(BF16), vj)
            acc2 = acc2 + _dot(e2.astype(BF16), vj)
            return l1, l2, acc1, acc2

        z1 = jnp.zeros((tq, 1), F32)
        za = jnp.zeros((tq, DIFF_V_DIM), F32)
        l1, l2, acc1, acc2 = lax.fori_loop(0, i + 1, sweep_pv, (z1, z1, za, za))
        o = acc1 * (1.0 / l1) - acc2 * (lam / l2)
        o = o * lax.rsqrt(jnp.mean(o * o, axis=-1, keepdims=True) + LN_EPS) * g_ref[...]
        o = o * (1.0 - lambda_init)
        o_ref[0, pl.ds(pl.multiple_of(i * tq, tq), tq), :] = o.astype(BF16)
        return carry

    lax.fori_loop(0, SEQ // tq, q_block, 0)


def _attn_call(lq1, lk1, lq2, lk2, g, q, k, v, lambda_init):
    const = lambda b, h: (0, 0)
    small = lambda a: pl.BlockSpec(a.shape, const)
    head_spec = pl.BlockSpec((1, SEQ, LANES), lambda b, h: (b, 0, h))
    return pl.pallas_call(
        functools.partial(_attn_kernel, lambda_init=lambda_init),
        grid=(BATCH, DIFF_HEADS),
        in_specs=[small(lq1), small(lk1), small(lq2), small(lk2), small(g),
                  head_spec, head_spec, head_spec],
        out_specs=head_spec,
        out_shape=jax.ShapeDtypeStruct((BATCH, SEQ, V_WIDTH), BF16),
        scratch_shapes=[pltpu.VMEM((2, SEQ // ATT_TK, ATT_TQ, ATT_TK), F32)],
        compiler_params=pltpu.CompilerParams(
            dimension_semantics=("arbitrary", "arbitrary"), vmem_limit_bytes=VMEM_LIMIT),
        name="diff_attn",
    )(lq1, lk1, lq2, lk2, g, q, k, v)


def _block_diag(w):
    eye = jnp.eye(LRU_BLOCKS, dtype=w.dtype)
    return jnp.einsum('gij,gh->gihj', w, eye).reshape(B_WIDTH, B_WIDTH)


def _head_perm():
    l = jnp.arange(LANES)
    within = ((l // 32) % 2) * DIFF_HEAD_DIM + (l // 64) * 32 + (l % 32)
    return (jnp.arange(DIFF_HEADS)[:, None] * LANES + within[None, :]).reshape(-1)


def _rope_tables():
    pos = jnp.arange(SEQ, dtype=F32)
    inv_freq = ROPE_THETA ** (-jnp.arange(0, DIFF_HEAD_DIM, 2, dtype=F32) / DIFF_HEAD_DIM)
    ang = pos[:, None] * inv_freq[None, :]
    ang = jnp.concatenate([ang] * (LANES // 32), axis=-1)
    sign = jnp.where(jnp.arange(LANES) < LANES // 2, -1.0, 1.0).astype(F32)
    return jnp.cos(ang), jnp.sin(ang) * sign[None, :]


def kernel(x, even_w_in, even_b_in, even_conv_w, even_conv_b, even_cnorm_g, even_cnorm_b,
           even_lru_conv_w, even_lru_conv_b, even_w_a, even_b_a, even_w_x, even_b_x,
           even_lru_lambda, even_w_out, odd_w_qkv, odd_lambda_q1, odd_lambda_k1,
           odd_lambda_q2, odd_lambda_k2, odd_subln_g, odd_w_out, mix_ln_g, mix_ln_b,
           ffn_w_gate, ffn_w_up, ffn_w_down, ffn_ln_g, ffn_ln_b):
    row = lambda a: a.reshape(1, -1)
    rows = BATCH * SEQ

    def post(m, xres, w_out, layer):
        return _post_call(m.reshape(rows, -1), xres.reshape(rows, D_MODEL), w_out.astype(BF16),
                          row(mix_ln_g[layer]), row(mix_ln_b[layer]),
                          ffn_w_gate[layer].astype(BF16), ffn_w_up[layer].astype(BF16),
                          ffn_w_down[layer].astype(BF16),
                          row(ffn_ln_g[layer]), row(ffn_ln_b[layer]))

    w_gate = jnp.concatenate([_block_diag(even_w_a[0]), _block_diag(even_w_x[0])], axis=1)
    b_gate = jnp.concatenate([even_b_a[0], even_b_x[0]]).reshape(1, -1)
    m0 = _mixer_call(x, even_w_in[0].astype(BF16), row(even_b_in[0]), even_conv_w[0],
                     row(even_conv_b[0]), row(even_cnorm_g[0]), row(even_cnorm_b[0]),
                     even_lru_conv_w[0], row(even_lru_conv_b[0]), w_gate.astype(BF16), b_gate,
                     row(even_lru_lambda[0]))
    x1 = post(m0, x, even_w_out[0], 0)

    lambda_init = 0.8 - 0.6 * math.exp(-0.3 * 1)
    perm = _head_perm()
    w_qkv = odd_w_qkv[0]
    w_qkv = jnp.concatenate([w_qkv[:, :QK_WIDTH][:, perm], w_qkv[:, QK_WIDTH:2 * QK_WIDTH][:, perm],
                             w_qkv[:, 2 * QK_WIDTH:]], axis=1).astype(BF16)
    cos, sin = _rope_tables()
    q, k, v = _qkv_call(x1, w_qkv, cos, sin)
    shape3 = (BATCH, SEQ, QK_WIDTH)
    o = _attn_call(row(odd_lambda_q1[0]), row(odd_lambda_k1[0]), row(odd_lambda_q2[0]),
                   row(odd_lambda_k2[0]), row(odd_subln_g[0]),
                   q.reshape(shape3), k.reshape(shape3), v.reshape(shape3), lambda_init)
    out = post(o, x1, odd_w_out[0], 1)
    return out.reshape(BATCH, SEQ, D_MODEL)
```

```python
import functools
import math

import jax
import jax.numpy as jnp
from jax import lax
from jax.experimental import pallas as pl
from jax.experimental.pallas import tpu as pltpu

F32 = jnp.float32
BF16 = jnp.bfloat16

D_MODEL = 1024
BATCH = 8
SEQ = 2048
DEPTH = 2
CHUNK = 64
A_WIDTH = 512
B_WIDTH = 512
CONV_WIDTH = 31
LRU_BLOCKS = 8
LRU_BLOCK_DIM = 64
LRU_CONV_WIDTH = 4
LRU_C = 8.0
IN_WIDTH = 2 * A_WIDTH + 2 * B_WIDTH
DIFF_HEADS = 8
DIFF_HEAD_DIM = 64
DIFF_V_DIM = 128
QK_WIDTH = 1024
V_WIDTH = 1024
ROPE_THETA = 10000.0
D_FF = 2816
LN_EPS = 1e-5
DN_ALPHA = (2 * DEPTH) ** 0.25
NEG_INF = -1e30
Q_SCALE = DIFF_HEAD_DIM ** -0.5 * math.log2(math.e)

LANES = 128
VMEM_LIMIT = 56 * 1024 * 1024

MIX_TS = 512
CONV_HALO = 32
CONV_ROWS = 32
LRU_HALO = 8
POST_TM = 512
FF_CHUNK = 256
QKV_TM = 512
ATT_HP = 4
ATT_TQ = 256
ATT_TK = 256


def _ln(x, g, b):
    mu = jnp.mean(x, axis=-1, keepdims=True)
    xc = x - mu
    var = jnp.mean(xc * xc, axis=-1, keepdims=True)
    return xc * lax.rsqrt(var + LN_EPS) * g + b


def _dot(a, b):
    return jnp.dot(a, b, preferred_element_type=F32)


def _mixer_kernel(x_ref, w_in_ref, b_in_ref, conv_w_ref, conv_b_ref, cn_g_ref, cn_b_ref,
                  lconv_w_ref, lconv_b_ref, w_gate_ref, b_gate_ref, lam_ref,
                  out_ref, a_ext, b_ext, h_carry, bgate_ref, gates_ref):
    ts = MIX_TS
    s = pl.program_id(1)

    @pl.when(s == 0)
    def _():
        a_ext[0:CONV_HALO, :] = jnp.zeros((CONV_HALO, A_WIDTH), F32)
        b_ext[0:LRU_HALO, :] = jnp.zeros((LRU_HALO, B_WIDTH), F32)
        h_carry[...] = jnp.zeros_like(h_carry)

    x = x_ref[0]
    h = _dot(x.astype(BF16), w_in_ref[...]) + b_in_ref[...]
    a_val = h[:, 0:A_WIDTH]
    a_gate = h[:, A_WIDTH:2 * A_WIDTH]
    bgate_ref[...] = h[:, 2 * A_WIDTH:2 * A_WIDTH + B_WIDTH]
    a_ext[CONV_HALO:CONV_HALO + ts, :] = a_val * jax.nn.sigmoid(a_gate)
    b_ext[LRU_HALO:LRU_HALO + ts, :] = h[:, 2 * A_WIDTH + B_WIDTH:]

    off = CONV_HALO - (CONV_WIDTH - 1)

    def conv_body(i, carry):
        base = pl.multiple_of(i * CONV_ROWS, CONV_ROWS)
        win = a_ext[pl.ds(base, CONV_ROWS + CONV_HALO), :]
        acc = jnp.broadcast_to(conv_b_ref[...], (CONV_ROWS, A_WIDTH))
        for r in range(8):
            part = None
            for j in range(CONV_WIDTH):
                if (off + j) % 8 == r:
                    term = conv_w_ref[j:j + 1, :] * win[off + j:off + j + CONV_ROWS, :]
                    part = term if part is None else part + term
            acc = acc + part
        ya = jax.nn.silu(_ln(acc, cn_g_ref[...], cn_b_ref[...]))
        out_ref[0, pl.ds(base, CONV_ROWS), 0:A_WIDTH] = ya.astype(BF16)
        return carry

    lax.fori_loop(0, ts // CONV_ROWS, conv_body, 0)
    a_ext[0:CONV_HALO, :] = a_ext[ts:ts + CONV_HALO, :]

    loff = LRU_HALO - (LRU_CONV_WIDTH - 1)
    xc = jnp.broadcast_to(lconv_b_ref[...], (ts, B_WIDTH))
    for j in range(LRU_CONV_WIDTH):
        xc = xc + lconv_w_ref[j:j + 1, :] * b_ext[loff + j:loff + j + ts, :]
    b_ext[0:LRU_HALO, :] = b_ext[ts:ts + LRU_HALO, :]
    gates_ref[...] = _dot(xc.astype(BF16), w_gate_ref[...]) + b_gate_ref[...]
    b_ext[LRU_HALO:LRU_HALO + ts, :] = xc

    t_idx = lax.broadcasted_iota(jnp.int32, (ts, LANES), 0)
    for g in range(B_WIDTH // LANES):
        sl = slice(g * LANES, (g + 1) * LANES)
        xg = b_ext[LRU_HALO:LRU_HALO + ts, sl]
        gate_r = jax.nn.sigmoid(gates_ref[:, sl])
        gate_i = jax.nn.sigmoid(gates_ref[:, B_WIDTH + g * LANES:B_WIDTH + (g + 1) * LANES])
        lam = lam_ref[:, sl]
        neg = -lam
        softplus = jnp.maximum(neg, 0.0) + jnp.log1p(jnp.exp(-jnp.abs(neg)))
        log_a = (-LRU_C * gate_r) * softplus
        a_cum = jnp.exp(log_a)
        th = jnp.tanh(log_a)
        u_cum = jnp.sqrt(-2.0 * th / (1.0 - th)) * (gate_i * xg)
        step = 1
        while step < ts:
            if step < 8:
                keep = t_idx >= step
                a_sh = jnp.where(keep, pltpu.roll(a_cum, step, axis=0), 1.0)
                u_sh = jnp.where(keep, pltpu.roll(u_cum, step, axis=0), 0.0)
            else:
                a_sh = jnp.concatenate([jnp.ones((step, LANES), F32), a_cum[:ts - step]], axis=0)
                u_sh = jnp.concatenate([jnp.zeros((step, LANES), F32), u_cum[:ts - step]], axis=0)
            u_cum = a_cum * u_sh + u_cum
            a_cum = a_cum * a_sh
            step *= 2
        hg = a_cum * h_carry[0:1, sl] + u_cum
        h_carry[:, sl] = jnp.broadcast_to(hg[ts - 1:ts, :], (8, LANES))
        yb = hg * jax.nn.gelu(bgate_ref[:, sl])
        out_ref[0, :, A_WIDTH + g * LANES:A_WIDTH + (g + 1) * LANES] = yb.astype(BF16)


def _mixer_call(x, w_in, b_in, conv_w, conv_b, cn_g, cn_b, lconv_w, lconv_b, w_gate, b_gate, lam):
    ts = MIX_TS
    const = lambda b, s: (0, 0)
    full = lambda a: pl.BlockSpec(a.shape, const)
    return pl.pallas_call(
        _mixer_kernel,
        grid=(BATCH, SEQ // ts),
        in_specs=[pl.BlockSpec((1, ts, D_MODEL), lambda b, s: (b, s, 0)),
                  full(w_in), full(b_in), full(conv_w), full(conv_b), full(cn_g), full(cn_b),
                  full(lconv_w), full(lconv_b), full(w_gate), full(b_gate), full(lam)],
        out_specs=pl.BlockSpec((1, ts, A_WIDTH + B_WIDTH), lambda b, s: (b, s, 0)),
        out_shape=jax.ShapeDtypeStruct((BATCH, SEQ, A_WIDTH + B_WIDTH), BF16),
        scratch_shapes=[pltpu.VMEM((CONV_HALO + ts, A_WIDTH), F32),
                        pltpu.VMEM((LRU_HALO + ts, B_WIDTH), F32),
                        pltpu.VMEM((8, B_WIDTH), F32),
                        pltpu.VMEM((ts, B_WIDTH), F32),
                        pltpu.VMEM((ts, 2 * B_WIDTH), F32)],
        compiler_params=pltpu.CompilerParams(
            dimension_semantics=("arbitrary", "arbitrary"), vmem_limit_bytes=VMEM_LIMIT),
        name="mixer0",
    )(x, w_in, b_in, conv_w, conv_b, cn_g, cn_b, lconv_w, lconv_b, w_gate, b_gate, lam)


def _post_kernel(m_ref, x_ref, w_out_ref, g1_ref, b1_ref, wg_ref, wu_ref, wd_ref, g2_ref, b2_ref,
                 out_ref, x1_ref, xb_ref, acc_ref):
    y = _dot(m_ref[...], w_out_ref[...])
    x1 = _ln(DN_ALPHA * x_ref[...] + y, g1_ref[...], b1_ref[...])
    x1_ref[...] = x1
    xb_ref[...] = x1.astype(BF16)
    for c in range(D_FF // FF_CHUNK):
        cs = slice(c * FF_CHUNK, (c + 1) * FF_CHUNK)
        gate = _dot(xb_ref[...], wg_ref[:, cs])
        up = _dot(xb_ref[...], wu_ref[:, cs])
        act = (jax.nn.silu(gate) * up).astype(BF16)
        contrib = _dot(act, wd_ref[cs, :])
        if c == 0:
            acc_ref[...] = contrib
        else:
            acc_ref[...] += contrib
    out_ref[...] = _ln(DN_ALPHA * x1_ref[...] + acc_ref[...], g2_ref[...], b2_ref[...])


def _post_call(m, x, w_out, g1, b1, wg, wu, wd, g2, b2):
    tm = POST_TM
    rows = m.shape[0]
    const = lambda i: (0, 0)
    full = lambda a: pl.BlockSpec(a.shape, const, pipeline_mode=pl.Buffered(1))
    return pl.pallas_call(
        _post_kernel,
        grid=(rows // tm,),
        in_specs=[pl.BlockSpec((tm, D_MODEL), lambda i: (i, 0)),
                  pl.BlockSpec((tm, D_MODEL), lambda i: (i, 0)),
                  full(w_out), full(g1), full(b1), full(wg), full(wu), full(wd), full(g2), full(b2)],
        out_specs=pl.BlockSpec((tm, D_MODEL), lambda i: (i, 0)),
        out_shape=jax.ShapeDtypeStruct((rows, D_MODEL), F32),
        scratch_shapes=[pltpu.VMEM((tm, D_MODEL), F32),
                        pltpu.VMEM((tm, D_MODEL), BF16),
                        pltpu.VMEM((tm, D_MODEL), F32)],
        compiler_params=pltpu.CompilerParams(
            dimension_semantics=("arbitrary",), vmem_limit_bytes=VMEM_LIMIT),
        name="post",
    )(m, x, w_out, g1, b1, wg, wu, wd, g2, b2)


def _qkv_kernel(x_ref, w_ref, cos_ref, sin_ref, q_ref, k_ref, v_ref):
    xb = x_ref[...].astype(BF16)
    cos = cos_ref[...]
    sin = sin_ref[...]
    for g in range(2 * QK_WIDTH // LANES):
        t = _dot(xb, w_ref[:, g * LANES:(g + 1) * LANES])
        r = t * cos + pltpu.roll(t, LANES // 2, axis=1) * sin
        if g < QK_WIDTH // LANES:
            q_ref[:, g * LANES:(g + 1) * LANES] = (r * Q_SCALE).astype(BF16)
        else:
            gk = g - QK_WIDTH // LANES
            k_ref[:, gk * LANES:(gk + 1) * LANES] = r.astype(BF16)
    v_ref[...] = _dot(xb, w_ref[:, 2 * QK_WIDTH:]).astype(BF16)


def _qkv_call(x, w, cos, sin):
    tm = QKV_TM
    rows = x.shape[0]
    pos_blocks = SEQ // tm
    row_spec = pl.BlockSpec((tm, D_MODEL), lambda i: (i, 0))
    tab_spec = pl.BlockSpec((tm, LANES), lambda i: (i % pos_blocks, 0))
    out = jax.ShapeDtypeStruct((rows, QK_WIDTH), BF16)
    return pl.pallas_call(
        _qkv_kernel,
        grid=(rows // tm,),
        in_specs=[row_spec,
                  pl.BlockSpec(w.shape, lambda i: (0, 0), pipeline_mode=pl.Buffered(1)),
                  tab_spec, tab_spec],
        out_specs=[row_spec, row_spec, row_spec],
        out_shape=[out, out, out],
        compiler_params=pltpu.CompilerParams(
            dimension_semantics=("arbitrary",), vmem_limit_bytes=VMEM_LIMIT),
        name="qkv",
    )(x, w, cos, sin)


def _attn_kernel(lq1_ref, lk1_ref, lq2_ref, lk2_ref, g_ref, q_ref, k_ref, v_ref, o_ref,
                 qs_ref, m_ref, acc_ref, *, lambda_init):
    tq, tk = ATT_TQ, ATT_TK
    lane = lax.broadcasted_iota(jnp.int32, (1, LANES), 1)
    is_map1 = ((lane // 32) % 2) == 0
    lam = (jnp.exp(jnp.sum(lq1_ref[...] * lk1_ref[...], axis=-1, keepdims=True))
           - jnp.exp(jnp.sum(lq2_ref[...] * lk2_ref[...], axis=-1, keepdims=True)) + lambda_init)
    row = lax.broadcasted_iota(jnp.int32, (2 * tq, tk), 0) % tq
    visible = row // CHUNK >= lax.broadcasted_iota(jnp.int32, (2 * tq, tk), 1) // CHUNK
    zero = jnp.zeros((), BF16)
    ones = jnp.ones((tk, LANES), BF16)

    def key_block(j, masked):
        ks = pl.ds(pl.multiple_of(j * tk, tk), tk)
        for h in range(ATT_HP):
            hs = slice(h * LANES, (h + 1) * LANES)
            s = lax.dot_general(qs_ref[h], k_ref[0, ks, hs], (((1,), (1,)), ((), ())),
                                preferred_element_type=F32)
            if masked:
                s = jnp.where(visible, s, NEG_INF)
            m_old = m_ref[h]
            m_cur = jnp.max(s, axis=-1, keepdims=True)
            m_new = jnp.maximum(m_old, jnp.broadcast_to(m_cur, (2 * tq, LANES)))
            alpha = jnp.exp2(m_old - m_new)
            e = jnp.exp2(s - jnp.concatenate([m_new] * (tk // LANES), axis=1)).astype(BF16)
            v_ext = jnp.concatenate([v_ref[0, ks, hs], ones], axis=1)
            acc_ref[h] = jnp.concatenate([alpha, alpha], axis=1) * acc_ref[h] + _dot(e, v_ext)
            m_ref[h] = m_new

    def q_block(i, carry):
        qrows = pl.ds(pl.multiple_of(i * tq, tq), tq)
        for h in range(ATT_HP):
            q = q_ref[0, qrows, h * LANES:(h + 1) * LANES]
            qs_ref[h, 0:tq, :] = jnp.where(is_map1, q, zero)
            qs_ref[h, tq:2 * tq, :] = jnp.where(is_map1, zero, q)
        m_ref[...] = jnp.full(m_ref.shape, NEG_INF, F32)
        acc_ref[...] = jnp.zeros(acc_ref.shape, F32)

        def off_diag(j, c):
            key_block(j, False)
            return c

        lax.fori_loop(0, i, off_diag, 0)
        key_block(i, True)

        for h in range(ATT_HP):
            a1 = acc_ref[h, 0:tq, :]
            a2 = acc_ref[h, tq:2 * tq, :]
            o = (a1[:, :LANES] * (1.0 / a1[:, LANES:])
                 - a2[:, :LANES] * (lam / a2[:, LANES:]))
            o = o * lax.rsqrt(jnp.mean(o * o, axis=-1, keepdims=True) + LN_EPS) * g_ref[...]
            o = o * (1.0 - lambda_init)
            o_ref[0, qrows, h * LANES:(h + 1) * LANES] = o.astype(BF16)
        return carry

    lax.fori_loop(0, SEQ // tq, q_block, 0)


def _attn_call(lq1, lk1, lq2, lk2, g, q, k, v, lambda_init):
    const = lambda b, h: (0, 0)
    small = lambda a: pl.BlockSpec(a.shape, const)
    head_spec = pl.BlockSpec((1, SEQ, ATT_HP * LANES), lambda b, h: (b, 0, h))
    return pl.pallas_call(
        functools.partial(_attn_kernel, lambda_init=lambda_init),
        grid=(BATCH, DIFF_HEADS // ATT_HP),
        in_specs=[small(lq1), small(lk1), small(lq2), small(lk2), small(g),
                  head_spec, head_spec, head_spec],
        out_specs=head_spec,
        out_shape=jax.ShapeDtypeStruct((BATCH, SEQ, V_WIDTH), BF16),
        scratch_shapes=[pltpu.VMEM((ATT_HP, 2 * ATT_TQ, LANES), BF16),
                        pltpu.VMEM((ATT_HP, 2 * ATT_TQ, LANES), F32),
                        pltpu.VMEM((ATT_HP, 2 * ATT_TQ, 2 * LANES), F32)],
        compiler_params=pltpu.CompilerParams(
            dimension_semantics=("arbitrary", "arbitrary"), vmem_limit_bytes=VMEM_LIMIT),
        name="diff_attn",
    )(lq1, lk1, lq2, lk2, g, q, k, v)


def _block_diag(w):
    eye = jnp.eye(LRU_BLOCKS, dtype=w.dtype)
    return jnp.einsum('gij,gh->gihj', w, eye).reshape(B_WIDTH, B_WIDTH)


def _head_perm():
    l = jnp.arange(LANES)
    within = ((l // 32) % 2) * DIFF_HEAD_DIM + (l // 64) * 32 + (l % 32)
    return (jnp.arange(DIFF_HEADS)[:, None] * LANES + within[None, :]).reshape(-1)


def _rope_tables():
    pos = jnp.arange(SEQ, dtype=F32)
    inv_freq = ROPE_THETA ** (-jnp.arange(0, DIFF_HEAD_DIM, 2, dtype=F32) / DIFF_HEAD_DIM)
    ang = pos[:, None] * inv_freq[None, :]
    ang = jnp.concatenate([ang] * (LANES // 32), axis=-1)
    sign = jnp.where(jnp.arange(LANES) < LANES // 2, -1.0, 1.0).astype(F32)
    return jnp.cos(ang), jnp.sin(ang) * sign[None, :]


def kernel(x, even_w_in, even_b_in, even_conv_w, even_conv_b, even_cnorm_g, even_cnorm_b,
           even_lru_conv_w, even_lru_conv_b, even_w_a, even_b_a, even_w_x, even_b_x,
           even_lru_lambda, even_w_out, odd_w_qkv, odd_lambda_q1, odd_lambda_k1,
           odd_lambda_q2, odd_lambda_k2, odd_subln_g, odd_w_out, mix_ln_g, mix_ln_b,
           ffn_w_gate, ffn_w_up, ffn_w_down, ffn_ln_g, ffn_ln_b):
    row = lambda a: a.reshape(1, -1)
    rows = BATCH * SEQ

    def post(m, xres, w_out, layer):
        return _post_call(m.reshape(rows, -1), xres.reshape(rows, D_MODEL), w_out.astype(BF16),
                          row(mix_ln_g[layer]), row(mix_ln_b[layer]),
                          ffn_w_gate[layer].astype(BF16), ffn_w_up[layer].astype(BF16),
                          ffn_w_down[layer].astype(BF16),
                          row(ffn_ln_g[layer]), row(ffn_ln_b[layer]))

    w_gate = jnp.concatenate([_block_diag(even_w_a[0]), _block_diag(even_w_x[0])], axis=1)
    b_gate = jnp.concatenate([even_b_a[0], even_b_x[0]]).reshape(1, -1)
    m0 = _mixer_call(x, even_w_in[0].astype(BF16), row(even_b_in[0]), even_conv_w[0],
                     row(even_conv_b[0]), row(even_cnorm_g[0]), row(even_cnorm_b[0]),
                     even_lru_conv_w[0], row(even_lru_conv_b[0]), w_gate.astype(BF16), b_gate,
                     row(even_lru_lambda[0]))
    x1 = post(m0, x, even_w_out[0], 0)

    lambda_init = 0.8 - 0.6 * math.exp(-0.3 * 1)
    perm = _head_perm()
    w_qkv = odd_w_qkv[0]
    w_qkv = jnp.concatenate([w_qkv[:, :QK_WIDTH][:, perm], w_qkv[:, QK_WIDTH:2 * QK_WIDTH][:, perm],
                             w_qkv[:, 2 * QK_WIDTH:]], axis=1).astype(BF16)
    cos, sin = _rope_tables()
    q, k, v = _qkv_call(x1, w_qkv, cos, sin)
    shape3 = (BATCH, SEQ, QK_WIDTH)
    o = _attn_call(row(odd_lambda_q1[0]), row(odd_lambda_k1[0]), row(odd_lambda_q2[0]),
                   row(odd_lambda_k2[0]), row(odd_subln_g[0]),
                   q.reshape(shape3), k.reshape(shape3), v.reshape(shape3), lambda_init)
    out = post(o, x1, odd_w_out[0], 1)
    return out.reshape(BATCH, SEQ, D_MODEL)
```

```python
import functools
import math

import jax
import jax.numpy as jnp
from jax import lax
from jax.experimental import pallas as pl
from jax.experimental.pallas import tpu as pltpu

F32 = jnp.float32
BF16 = jnp.bfloat16

D_MODEL = 1024
BATCH = 8
SEQ = 2048
DEPTH = 2
CHUNK = 64
A_WIDTH = 512
B_WIDTH = 512
CONV_WIDTH = 31
LRU_BLOCKS = 8
LRU_BLOCK_DIM = 64
LRU_CONV_WIDTH = 4
LRU_C = 8.0
IN_WIDTH = 2 * A_WIDTH + 2 * B_WIDTH
DIFF_HEADS = 8
DIFF_HEAD_DIM = 64
DIFF_V_DIM = 128
QK_WIDTH = 1024
V_WIDTH = 1024
ROPE_THETA = 10000.0
D_FF = 2816
LN_EPS = 1e-5
DN_ALPHA = (2 * DEPTH) ** 0.25
NEG_INF = -1e30
Q_SCALE = DIFF_HEAD_DIM ** -0.5 * math.log2(math.e)

LANES = 128
SUBLANES = 8
MXU_COLS = 256
VMEM_LIMIT = 56 * 1024 * 1024

MIX_TS = 512
CONV_HALO = 32
CONV_ROWS = 32
LRU_HALO = 8
POST_TM = 512
FF_CHUNK = 256
QKV_TM = 512
ATT_HP = 4
ATT_TQ = 256
ATT_TK = 256


def _ln(x, g, b):
    mu = jnp.mean(x, axis=-1, keepdims=True)
    xc = x - mu
    var = jnp.mean(xc * xc, axis=-1, keepdims=True)
    return xc * lax.rsqrt(var + LN_EPS) * g + b


def _dot(a, b):
    return jnp.dot(a, b, preferred_element_type=F32)


def _mixer_kernel(x_ref, w_in_ref, b_in_ref, conv_w_ref, conv_b_ref, cn_g_ref, cn_b_ref,
                  lconv_w_ref, lconv_b_ref, w_gate_ref, b_gate_ref, lam_ref,
                  out_ref, a_ext, a_sh, b_ext, h_carry, bgate_ref, gates_ref, au_ref, xb_ref):
    ts = MIX_TS
    s = pl.program_id(1)

    @pl.when(s == 0)
    def _():
        a_ext[0:CONV_HALO, :] = jnp.zeros((CONV_HALO, A_WIDTH), F32)
        a_ext[CONV_HALO + ts:, :] = jnp.zeros((SUBLANES, A_WIDTH), F32)
        b_ext[0:LRU_HALO, :] = jnp.zeros((LRU_HALO, B_WIDTH), F32)
        h_carry[...] = jnp.zeros_like(h_carry)

    xb_ref[...] = x_ref[0].astype(BF16)
    ha = _dot(xb_ref[...], w_in_ref[:, 0:2 * A_WIDTH]) + b_in_ref[:, 0:2 * A_WIDTH]
    a_ext[CONV_HALO:CONV_HALO + ts, :] = ha[:, 0:A_WIDTH] * jax.nn.sigmoid(ha[:, A_WIDTH:])

    off = CONV_HALO - (CONV_WIDTH - 1)

    def shift_body(i, carry):
        base = pl.multiple_of(i * CONV_ROWS, CONV_ROWS)
        win = a_ext[pl.ds(base, CONV_ROWS + SUBLANES), :]
        for r in range(1, SUBLANES):
            rolled = pltpu.roll(win, CONV_ROWS + SUBLANES - r, axis=0)
            a_sh[r - 1, pl.ds(base, CONV_ROWS), :] = rolled[0:CONV_ROWS, :]
        return carry

    lax.fori_loop(0, (ts + CONV_HALO) // CONV_ROWS, shift_body, 0)

    def conv_body(i, carry):
        base = pl.multiple_of(i * CONV_ROWS, CONV_ROWS)
        acc = jnp.broadcast_to(conv_b_ref[...], (CONV_ROWS, A_WIDTH))
        for j in range(CONV_WIDTH):
            q, r = divmod(off + j, SUBLANES)
            rows = pl.ds(base + q * SUBLANES, CONV_ROWS)
            win = a_ext[rows, :] if r == 0 else a_sh[r - 1, rows, :]
            acc = acc + (win.reshape(-1, SUBLANES, A_WIDTH) * conv_w_ref[j][None]).reshape(CONV_ROWS, A_WIDTH)
        ya = jax.nn.silu(_ln(acc, cn_g_ref[...], cn_b_ref[...]))
        out_ref[0, pl.ds(base, CONV_ROWS), 0:A_WIDTH] = ya.astype(BF16)
        return carry

    lax.fori_loop(0, ts // CONV_ROWS, conv_body, 0, unroll=2)
    a_ext[0:CONV_HALO, :] = a_ext[ts:ts + CONV_HALO, :]

    hb = _dot(xb_ref[...], w_in_ref[:, 2 * A_WIDTH:]) + b_in_ref[:, 2 * A_WIDTH:]
    bgate_ref[...] = hb[:, 0:B_WIDTH]
    b_ext[LRU_HALO:LRU_HALO + ts, :] = hb[:, B_WIDTH:]
    loff = LRU_HALO - (LRU_CONV_WIDTH - 1)
    xc = jnp.broadcast_to(lconv_b_ref[...], (ts, B_WIDTH))
    for j in range(LRU_CONV_WIDTH):
        xc = xc + lconv_w_ref[j:j + 1, :] * b_ext[loff + j:loff + j + ts, :]
    b_ext[0:LRU_HALO, :] = b_ext[ts:ts + LRU_HALO, :]
    gates_ref[...] = _dot(xc.astype(BF16), w_gate_ref[...]) + b_gate_ref[...]
    b_ext[LRU_HALO:LRU_HALO + ts, :] = xc

    def scan_steps(a_cum, u_cum, axis, steps):
        idx = lax.broadcasted_iota(jnp.int32, a_cum.shape, axis)
        for step in steps:
            keep = idx >= step
            a_prev = jnp.where(keep, pltpu.roll(a_cum, step, axis=axis), 1.0)
            u_prev = jnp.where(keep, pltpu.roll(u_cum, step, axis=axis), 0.0)
            u_cum = a_cum * u_prev + u_cum
            a_cum = a_cum * a_prev
        return a_cum, u_cum

    nblk = ts // SUBLANES
    for g in range(B_WIDTH // LANES):
        sl = slice(g * LANES, (g + 1) * LANES)
        xg = b_ext[LRU_HALO:LRU_HALO + ts, sl]
        gate_r = jax.nn.sigmoid(gates_ref[:, sl])
        gate_i = jax.nn.sigmoid(gates_ref[:, B_WIDTH + g * LANES:B_WIDTH + (g + 1) * LANES])
        lam = lam_ref[:, sl]
        neg = -lam
        softplus = jnp.maximum(neg, 0.0) + jnp.log1p(jnp.exp(-jnp.abs(neg)))
        log_a = (-LRU_C * gate_r) * softplus
        th = jnp.tanh(log_a)
        y = -2.0 * th
        scale = jnp.where(y > 0.0, y * lax.rsqrt(y), 0.0) * lax.rsqrt(1.0 - th)
        a_blk, u_blk = scan_steps(jnp.exp(log_a).reshape(nblk, SUBLANES, LANES),
                                  (scale * (gate_i * xg)).reshape(nblk, SUBLANES, LANES),
                                  1, (1, 2, 4))
        au_ref[0] = a_blk.reshape(ts, LANES)
        au_ref[1] = u_blk.reshape(ts, LANES)
        last = pl.ds(SUBLANES - 1, nblk, stride=SUBLANES)
        a_end, u_end = scan_steps(au_ref[0, last, :], au_ref[1, last, :], 0,
                                  [1 << k for k in range(nblk.bit_length() - 1)])
        h_prev = h_carry[0:1, sl]
        h_end = a_end * h_prev + u_end
        h_carry[:, sl] = jnp.broadcast_to(h_end[nblk - 1:nblk, :], (SUBLANES, LANES))
        row0 = lax.broadcasted_iota(jnp.int32, (nblk, LANES), 0) == 0
        h_in = jnp.where(row0, h_prev, pltpu.roll(h_end, 1, axis=0))
        hg = (a_blk * h_in[:, None, :] + u_blk).reshape(ts, LANES)
        yb = hg * jax.nn.gelu(bgate_ref[:, sl])
        out_ref[0, :, A_WIDTH + g * LANES:A_WIDTH + (g + 1) * LANES] = yb.astype(BF16)


def _mixer_call(x, w_in, b_in, conv_w, conv_b, cn_g, cn_b, lconv_w, lconv_b, w_gate, b_gate, lam):
    ts = MIX_TS
    full = lambda a: pl.BlockSpec(a.shape, lambda b, s, nd=a.ndim: (0,) * nd,
                                  pipeline_mode=pl.Buffered(1))
    return pl.pallas_call(
        _mixer_kernel,
        grid=(BATCH, SEQ // ts),
        in_specs=[pl.BlockSpec((1, ts, D_MODEL), lambda b, s: (b, s, 0)),
                  full(w_in), full(b_in), full(conv_w), full(conv_b), full(cn_g), full(cn_b),
                  full(lconv_w), full(lconv_b), full(w_gate), full(b_gate), full(lam)],
        out_specs=pl.BlockSpec((1, ts, A_WIDTH + B_WIDTH), lambda b, s: (b, s, 0)),
        out_shape=jax.ShapeDtypeStruct((BATCH, SEQ, A_WIDTH + B_WIDTH), BF16),
        scratch_shapes=[pltpu.VMEM((CONV_HALO + ts + SUBLANES, A_WIDTH), F32),
                        pltpu.VMEM((SUBLANES - 1, CONV_HALO + ts, A_WIDTH), F32),
                        pltpu.VMEM((LRU_HALO + ts, B_WIDTH), F32),
                        pltpu.VMEM((SUBLANES, B_WIDTH), F32),
                        pltpu.VMEM((ts, B_WIDTH), F32),
                        pltpu.VMEM((ts, 2 * B_WIDTH), F32),
                        pltpu.VMEM((2, ts, LANES), F32),
                        pltpu.VMEM((ts, D_MODEL), BF16)],
        compiler_params=pltpu.CompilerParams(
            dimension_semantics=("arbitrary", "arbitrary"), vmem_limit_bytes=VMEM_LIMIT),
        name="mixer0",
    )(x, w_in, b_in, conv_w, conv_b, cn_g, cn_b, lconv_w, lconv_b, w_gate, b_gate, lam)


def _post_kernel(m_ref, x_ref, w_out_ref, g1_ref, b1_ref, wg_ref, wu_ref, wd_ref, g2_ref, b2_ref,
                 out_ref, x1_ref, xb_ref, acc_ref):
    y = _dot(m_ref[...], w_out_ref[...])
    x1 = _ln(DN_ALPHA * x_ref[...] + y, g1_ref[...], b1_ref[...])
    x1_ref[...] = x1
    xb_ref[...] = x1.astype(BF16)
    for c in range(D_FF // FF_CHUNK):
        cs = slice(c * FF_CHUNK, (c + 1) * FF_CHUNK)
        gate = _dot(xb_ref[...], wg_ref[:, cs])
        up = _dot(xb_ref[...], wu_ref[:, cs])
        act = (jax.nn.silu(gate) * up).astype(BF16)
        contrib = _dot(act, wd_ref[cs, :])
        if c == 0:
            acc_ref[...] = contrib
        else:
            acc_ref[...] += contrib
    out_ref[...] = _ln(DN_ALPHA * x1_ref[...] + acc_ref[...], g2_ref[...], b2_ref[...])


def _post_call(m, x, w_out, g1, b1, wg, wu, wd, g2, b2):
    tm = POST_TM
    rows = m.shape[0]
    const = lambda i: (0, 0)
    full = lambda a: pl.BlockSpec(a.shape, const, pipeline_mode=pl.Buffered(1))
    return pl.pallas_call(
        _post_kernel,
        grid=(rows // tm,),
        in_specs=[pl.BlockSpec((tm, D_MODEL), lambda i: (i, 0)),
                  pl.BlockSpec((tm, D_MODEL), lambda i: (i, 0)),
                  full(w_out), full(g1), full(b1), full(wg), full(wu), full(wd), full(g2), full(b2)],
        out_specs=pl.BlockSpec((tm, D_MODEL), lambda i: (i, 0)),
        out_shape=jax.ShapeDtypeStruct((rows, D_MODEL), F32),
        scratch_shapes=[pltpu.VMEM((tm, D_MODEL), F32),
                        pltpu.VMEM((tm, D_MODEL), BF16),
                        pltpu.VMEM((tm, D_MODEL), F32)],
        compiler_params=pltpu.CompilerParams(
            dimension_semantics=("arbitrary",), vmem_limit_bytes=VMEM_LIMIT),
        name="post",
    )(m, x, w_out, g1, b1, wg, wu, wd, g2, b2)


def _qkv_kernel(x_ref, w_ref, cos_ref, sin_ref, q_ref, k_ref, v_ref):
    xb = x_ref[...].astype(BF16)
    cos = cos_ref[...]
    sin = sin_ref[...]
    for g in range(2 * QK_WIDTH // MXU_COLS):
        t2 = _dot(xb, w_ref[:, g * MXU_COLS:(g + 1) * MXU_COLS])
        for half in range(MXU_COLS // LANES):
            t = t2[:, half * LANES:(half + 1) * LANES]
            r = t * cos + pltpu.roll(t, LANES // 2, axis=1) * sin
            col = g * MXU_COLS + half * LANES
            if col < QK_WIDTH:
                q_ref[:, col:col + LANES] = (r * Q_SCALE).astype(BF16)
            else:
                k_ref[:, col - QK_WIDTH:col - QK_WIDTH + LANES] = r.astype(BF16)
    v_ref[...] = _dot(xb, w_ref[:, 2 * QK_WIDTH:]).astype(BF16)


def _qkv_call(x, w, cos, sin):
    tm = QKV_TM
    rows = x.shape[0]
    pos_blocks = SEQ // tm
    row_spec = pl.BlockSpec((tm, D_MODEL), lambda i: (i, 0))
    tab_spec = pl.BlockSpec((tm, LANES), lambda i: (i % pos_blocks, 0))
    out = jax.ShapeDtypeStruct((rows, QK_WIDTH), BF16)
    return pl.pallas_call(
        _qkv_kernel,
        grid=(rows // tm,),
        in_specs=[row_spec,
                  pl.BlockSpec(w.shape, lambda i: (0, 0), pipeline_mode=pl.Buffered(1)),
                  tab_spec, tab_spec],
        out_specs=[row_spec, row_spec, row_spec],
        out_shape=[out, out, out],
        compiler_params=pltpu.CompilerParams(
            dimension_semantics=("arbitrary",), vmem_limit_bytes=VMEM_LIMIT),
        name="qkv",
    )(x, w, cos, sin)


def _attn_kernel(lq1_ref, lk1_ref, lq2_ref, lk2_ref, g_ref, q_ref, k_ref, v_ref, o_ref,
                 qs_ref, m_ref, acc_ref, *, lambda_init):
    tq, tk = ATT_TQ, ATT_TK
    lane = lax.broadcasted_iota(jnp.int32, (1, LANES), 1)
    is_map1 = ((lane // 32) % 2) == 0
    lam = (jnp.exp(jnp.sum(lq1_ref[...] * lk1_ref[...], axis=-1, keepdims=True))
           - jnp.exp(jnp.sum(lq2_ref[...] * lk2_ref[...], axis=-1, keepdims=True)) + lambda_init)
    row = lax.broadcasted_iota(jnp.int32, (2 * tq, tk), 0) % tq
    visible = row // CHUNK >= lax.broadcasted_iota(jnp.int32, (2 * tq, tk), 1) // CHUNK
    zero = jnp.zeros((), BF16)
    ones = jnp.ones((tk, LANES), BF16)

    def key_block(j, masked):
        ks = pl.ds(pl.multiple_of(j * tk, tk), tk)
        for h in range(ATT_HP):
            hs = slice(h * LANES, (h + 1) * LANES)
            s = lax.dot_general(qs_ref[h], k_ref[0, ks, hs], (((1,), (1,)), ((), ())),
                                preferred_element_type=F32)
            if masked:
                s = jnp.where(visible, s, NEG_INF)
            m_old = m_ref[h]
            m_cur = jnp.max(s, axis=-1, keepdims=True)
            m_new = jnp.maximum(m_old, jnp.broadcast_to(m_cur, (2 * tq, LANES)))
            alpha = jnp.exp2(m_old - m_new)
            e = jnp.exp2(s - jnp.concatenate([m_new] * (tk // LANES), axis=1)).astype(BF16)
            v_ext = jnp.concatenate([v_ref[0, ks, hs], ones], axis=1)
            acc_ref[h] = jnp.concatenate([alpha, alpha], axis=1) * acc_ref[h] + _dot(e, v_ext)
            m_ref[h] = m_new

    def q_block(i, carry):
        qrows = pl.ds(pl.multiple_of(i * tq, tq), tq)
        for h in range(ATT_HP):
            q = q_ref[0, qrows, h * LANES:(h + 1) * LANES]
            qs_ref[h, 0:tq, :] = jnp.where(is_map1, q, zero)
            qs_ref[h, tq:2 * tq, :] = jnp.where(is_map1, zero, q)
        m_ref[...] = jnp.full(m_ref.shape, NEG_INF, F32)
        acc_ref[...] = jnp.zeros(acc_ref.shape, F32)

        def off_diag(j, c):
            key_block(j, False)
            return c

        lax.fori_loop(0, i, off_diag, 0)
        key_block(i, True)

        for h in range(ATT_HP):
            a1 = acc_ref[h, 0:tq, :]
            a2 = acc_ref[h, tq:2 * tq, :]
            o = (a1[:, :LANES] * (1.0 / a1[:, LANES:])
                 - a2[:, :LANES] * (lam / a2[:, LANES:]))
            o = o * lax.rsqrt(jnp.mean(o * o, axis=-1, keepdims=True) + LN_EPS) * g_ref[...]
            o = o * (1.0 - lambda_init)
            o_ref[0, qrows, h * LANES:(h + 1) * LANES] = o.astype(BF16)
        return carry

    lax.fori_loop(0, SEQ // tq, q_block, 0)


def _attn_call(lq1, lk1, lq2, lk2, g, q, k, v, lambda_init):
    const = lambda b, h: (0, 0)
    small = lambda a: pl.BlockSpec(a.shape, const)
    head_spec = pl.BlockSpec((1, SEQ, ATT_HP * LANES), lambda b, h: (b, 0, h))
    return pl.pallas_call(
        functools.partial(_attn_kernel, lambda_init=lambda_init),
        grid=(BATCH, DIFF_HEADS // ATT_HP),
        in_specs=[small(lq1), small(lk1), small(lq2), small(lk2), small(g),
                  head_spec, head_spec, head_spec],
        out_specs=head_spec,
        out_shape=jax.ShapeDtypeStruct((BATCH, SEQ, V_WIDTH), BF16),
        scratch_shapes=[pltpu.VMEM((ATT_HP, 2 * ATT_TQ, LANES), BF16),
                        pltpu.VMEM((ATT_HP, 2 * ATT_TQ, LANES), F32),
                        pltpu.VMEM((ATT_HP, 2 * ATT_TQ, 2 * LANES), F32)],
        compiler_params=pltpu.CompilerParams(
            dimension_semantics=("arbitrary", "arbitrary"), vmem_limit_bytes=VMEM_LIMIT),
        name="diff_attn",
    )(lq1, lk1, lq2, lk2, g, q, k, v)


def _block_diag(w):
    eye = jnp.eye(LRU_BLOCKS, dtype=w.dtype)
    return jnp.einsum('gij,gh->gihj', w, eye).reshape(B_WIDTH, B_WIDTH)


def _head_perm():
    l = jnp.arange(LANES)
    within = ((l // 32) % 2) * DIFF_HEAD_DIM + (l // 64) * 32 + (l % 32)
    return (jnp.arange(DIFF_HEADS)[:, None] * LANES + within[None, :]).reshape(-1)


def _rope_tables():
    pos = jnp.arange(SEQ, dtype=F32)
    inv_freq = ROPE_THETA ** (-jnp.arange(0, DIFF_HEAD_DIM, 2, dtype=F32) / DIFF_HEAD_DIM)
    ang = pos[:, None] * inv_freq[None, :]
    ang = jnp.concatenate([ang] * (LANES // 32), axis=-1)
    sign = jnp.where(jnp.arange(LANES) < LANES // 2, -1.0, 1.0).astype(F32)
    return jnp.cos(ang), jnp.sin(ang) * sign[None, :]


def kernel(x, even_w_in, even_b_in, even_conv_w, even_conv_b, even_cnorm_g, even_cnorm_b,
           even_lru_conv_w, even_lru_conv_b, even_w_a, even_b_a, even_w_x, even_b_x,
           even_lru_lambda, even_w_out, odd_w_qkv, odd_lambda_q1, odd_lambda_k1,
           odd_lambda_q2, odd_lambda_k2, odd_subln_g, odd_w_out, mix_ln_g, mix_ln_b,
           ffn_w_gate, ffn_w_up, ffn_w_down, ffn_ln_g, ffn_ln_b):
    row = lambda a: a.reshape(1, -1)
    rows = BATCH * SEQ

    def post(m, xres, w_out, layer):
        return _post_call(m.reshape(rows, -1), xres.reshape(rows, D_MODEL), w_out.astype(BF16),
                          row(mix_ln_g[layer]), row(mix_ln_b[layer]),
                          ffn_w_gate[layer].astype(BF16), ffn_w_up[layer].astype(BF16),
                          ffn_w_down[layer].astype(BF16),
                          row(ffn_ln_g[layer]), row(ffn_ln_b[layer]))

    w_gate = jnp.concatenate([_block_diag(even_w_a[0]), _block_diag(even_w_x[0])], axis=1)
    b_gate = jnp.concatenate([even_b_a[0], even_b_x[0]]).reshape(1, -1)
    conv_w = jnp.broadcast_to(even_conv_w[0][:, None, :], (CONV_WIDTH, SUBLANES, A_WIDTH))
    m0 = _mixer_call(x, even_w_in[0].astype(BF16), row(even_b_in[0]), conv_w,
                     row(even_conv_b[0]), row(even_cnorm_g[0]), row(even_cnorm_b[0]),
                     even_lru_conv_w[0], row(even_lru_conv_b[0]), w_gate.astype(BF16), b_gate,
                     row(even_lru_lambda[0]))
    x1 = post(m0, x, even_w_out[0], 0)

    lambda_init = 0.8 - 0.6 * math.exp(-0.3 * 1)
    perm = _head_perm()
    w_qkv = odd_w_qkv[0]
    w_qkv = jnp.concatenate([w_qkv[:, :QK_WIDTH][:, perm], w_qkv[:, QK_WIDTH:2 * QK_WIDTH][:, perm],
                             w_qkv[:, 2 * QK_WIDTH:]], axis=1).astype(BF16)
    cos, sin = _rope_tables()
    q, k, v = _qkv_call(x1, w_qkv, cos, sin)
    shape3 = (BATCH, SEQ, QK_WIDTH)
    o = _attn_call(row(odd_lambda_q1[0]), row(odd_lambda_k1[0]), row(odd_lambda_q2[0]),
                   row(odd_lambda_k2[0]), row(odd_subln_g[0]),
                   q.reshape(shape3), k.reshape(shape3), v.reshape(shape3), lambda_init)
    out = post(o, x1, odd_w_out[0], 1)
    return out.reshape(BATCH, SEQ, D_MODEL)
```

```python
import functools
import math

import jax
import jax.numpy as jnp
import numpy as np
from jax import lax
from jax.experimental import pallas as pl
from jax.experimental.pallas import tpu as pltpu

F32 = jnp.float32
BF16 = jnp.bfloat16

D_MODEL = 1024
BATCH = 8
SEQ = 2048
DEPTH = 2
CHUNK = 64
A_WIDTH = 512
B_WIDTH = 512
CONV_WIDTH = 31
LRU_BLOCKS = 8
LRU_BLOCK_DIM = 64
LRU_CONV_WIDTH = 4
LRU_C = 8.0
IN_WIDTH = 2 * A_WIDTH + 2 * B_WIDTH
DIFF_HEADS = 8
DIFF_HEAD_DIM = 64
DIFF_V_DIM = 128
QK_WIDTH = 1024
V_WIDTH = 1024
ROPE_THETA = 10000.0
D_FF = 2816
LN_EPS = 1e-5
DN_ALPHA = (2 * DEPTH) ** 0.25
NEG_INF = -1e30
Q_SCALE = DIFF_HEAD_DIM ** -0.5 * math.log2(math.e)

LANES = 128
SUBLANES = 8
MXU_COLS = 256
VMEM_LIMIT = 56 * 1024 * 1024

MIX_TS = 512
CONV_HALO = 32
CONV_ROWS = 32
LRU_HALO = 8
POST_TM = 512
FF_CHUNK = 256
QKV_TM = 512
ATT_HP = 4
ATT_TH = 256


def _ln(x, g, b):
    mu = jnp.mean(x, axis=-1, keepdims=True)
    xc = x - mu
    var = jnp.mean(xc * xc, axis=-1, keepdims=True)
    return xc * lax.rsqrt(var + LN_EPS) * g + b


def _dot(a, b):
    return jnp.dot(a, b, preferred_element_type=F32)


def _mixer_kernel(x_ref, w_in_ref, b_in_ref, conv_w_ref, conv_b_ref, cn_g_ref, cn_b_ref,
                  lconv_w_ref, lconv_b_ref, w_gate_ref, b_gate_ref, lam_ref,
                  out_ref, a_ext, a_sh, b_ext, h_carry, bgate_ref, gates_ref, au_ref, xb_ref):
    ts = MIX_TS
    s = pl.program_id(1)

    @pl.when(s == 0)
    def _():
        a_ext[0:CONV_HALO, :] = jnp.zeros((CONV_HALO, A_WIDTH), F32)
        a_ext[CONV_HALO + ts:, :] = jnp.zeros((SUBLANES, A_WIDTH), F32)
        b_ext[0:LRU_HALO, :] = jnp.zeros((LRU_HALO, B_WIDTH), F32)
        h_carry[...] = jnp.zeros_like(h_carry)

    xb_ref[...] = x_ref[0].astype(BF16)
    ha = _dot(xb_ref[...], w_in_ref[:, 0:2 * A_WIDTH]) + b_in_ref[:, 0:2 * A_WIDTH]
    a_ext[CONV_HALO:CONV_HALO + ts, :] = ha[:, 0:A_WIDTH] * jax.nn.sigmoid(ha[:, A_WIDTH:])

    off = CONV_HALO - (CONV_WIDTH - 1)

    def shift_body(i, carry):
        base = pl.multiple_of(i * CONV_ROWS, CONV_ROWS)
        win = a_ext[pl.ds(base, CONV_ROWS + SUBLANES), :]
        for r in range(1, SUBLANES):
            rolled = pltpu.roll(win, CONV_ROWS + SUBLANES - r, axis=0)
            a_sh[r - 1, pl.ds(base, CONV_ROWS), :] = rolled[0:CONV_ROWS, :]
        return carry

    lax.fori_loop(0, (ts + CONV_HALO) // CONV_ROWS, shift_body, 0)

    def conv_body(i, carry):
        base = pl.multiple_of(i * CONV_ROWS, CONV_ROWS)
        acc = jnp.broadcast_to(conv_b_ref[...], (CONV_ROWS, A_WIDTH))
        for j in range(CONV_WIDTH):
            q, r = divmod(off + j, SUBLANES)
            rows = pl.ds(base + q * SUBLANES, CONV_ROWS)
            win = a_ext[rows, :] if r == 0 else a_sh[r - 1, rows, :]
            acc = acc + (win.reshape(-1, SUBLANES, A_WIDTH) * conv_w_ref[j][None]).reshape(CONV_ROWS, A_WIDTH)
        ya = jax.nn.silu(_ln(acc, cn_g_ref[...], cn_b_ref[...]))
        out_ref[0, pl.ds(base, CONV_ROWS), 0:A_WIDTH] = ya.astype(BF16)
        return carry

    lax.fori_loop(0, ts // CONV_ROWS, conv_body, 0, unroll=2)
    a_ext[0:CONV_HALO, :] = a_ext[ts:ts + CONV_HALO, :]

    hb = _dot(xb_ref[...], w_in_ref[:, 2 * A_WIDTH:]) + b_in_ref[:, 2 * A_WIDTH:]
    bgate_ref[...] = hb[:, 0:B_WIDTH]
    b_ext[LRU_HALO:LRU_HALO + ts, :] = hb[:, B_WIDTH:]
    loff = LRU_HALO - (LRU_CONV_WIDTH - 1)
    xc = jnp.broadcast_to(lconv_b_ref[...], (ts, B_WIDTH))
    for j in range(LRU_CONV_WIDTH):
        xc = xc + lconv_w_ref[j:j + 1, :] * b_ext[loff + j:loff + j + ts, :]
    b_ext[0:LRU_HALO, :] = b_ext[ts:ts + LRU_HALO, :]
    gates_ref[...] = _dot(xc.astype(BF16), w_gate_ref[...]) + b_gate_ref[...]
    b_ext[LRU_HALO:LRU_HALO + ts, :] = xc

    def scan_steps(a_cum, u_cum, axis, steps):
        idx = lax.broadcasted_iota(jnp.int32, a_cum.shape, axis)
        for step in steps:
            keep = idx >= step
            a_prev = jnp.where(keep, pltpu.roll(a_cum, step, axis=axis), 1.0)
            u_prev = jnp.where(keep, pltpu.roll(u_cum, step, axis=axis), 0.0)
            u_cum = a_cum * u_prev + u_cum
            a_cum = a_cum * a_prev
        return a_cum, u_cum

    nblk = ts // SUBLANES
    for g in range(B_WIDTH // LANES):
        sl = slice(g * LANES, (g + 1) * LANES)
        xg = b_ext[LRU_HALO:LRU_HALO + ts, sl]
        gate_r = jax.nn.sigmoid(gates_ref[:, sl])
        gate_i = jax.nn.sigmoid(gates_ref[:, B_WIDTH + g * LANES:B_WIDTH + (g + 1) * LANES])
        lam = lam_ref[:, sl]
        neg = -lam
        softplus = jnp.maximum(neg, 0.0) + jnp.log1p(jnp.exp(-jnp.abs(neg)))
        log_a = (-LRU_C * gate_r) * softplus
        th = jnp.tanh(log_a)
        y = -2.0 * th
        scale = jnp.where(y > 0.0, y * lax.rsqrt(y), 0.0) * lax.rsqrt(1.0 - th)
        a_blk, u_blk = scan_steps(jnp.exp(log_a).reshape(nblk, SUBLANES, LANES),
                                  (scale * (gate_i * xg)).reshape(nblk, SUBLANES, LANES),
                                  1, (1, 2, 4))
        au_ref[0] = a_blk.reshape(ts, LANES)
        au_ref[1] = u_blk.reshape(ts, LANES)
        last = pl.ds(SUBLANES - 1, nblk, stride=SUBLANES)
        a_end, u_end = scan_steps(au_ref[0, last, :], au_ref[1, last, :], 0,
                                  [1 << k for k in range(nblk.bit_length() - 1)])
        h_prev = h_carry[0:1, sl]
        h_end = a_end * h_prev + u_end
        h_carry[:, sl] = jnp.broadcast_to(h_end[nblk - 1:nblk, :], (SUBLANES, LANES))
        row0 = lax.broadcasted_iota(jnp.int32, (nblk, LANES), 0) == 0
        h_in = jnp.where(row0, h_prev, pltpu.roll(h_end, 1, axis=0))
        hg = (a_blk * h_in[:, None, :] + u_blk).reshape(ts, LANES)
        yb = hg * jax.nn.gelu(bgate_ref[:, sl])
        out_ref[0, :, A_WIDTH + g * LANES:A_WIDTH + (g + 1) * LANES] = yb.astype(BF16)


def _mixer_call(x, w_in, b_in, conv_w, conv_b, cn_g, cn_b, lconv_w, lconv_b, w_gate, b_gate, lam):
    ts = MIX_TS
    full = lambda a: pl.BlockSpec(a.shape, lambda b, s, nd=a.ndim: (0,) * nd,
                                  pipeline_mode=pl.Buffered(1))
    return pl.pallas_call(
        _mixer_kernel,
        grid=(BATCH, SEQ // ts),
        in_specs=[pl.BlockSpec((1, ts, D_MODEL), lambda b, s: (b, s, 0)),
                  full(w_in), full(b_in), full(conv_w), full(conv_b), full(cn_g), full(cn_b),
                  full(lconv_w), full(lconv_b), full(w_gate), full(b_gate), full(lam)],
        out_specs=pl.BlockSpec((1, ts, A_WIDTH + B_WIDTH), lambda b, s: (b, s, 0)),
        out_shape=jax.ShapeDtypeStruct((BATCH, SEQ, A_WIDTH + B_WIDTH), BF16),
        scratch_shapes=[pltpu.VMEM((CONV_HALO + ts + SUBLANES, A_WIDTH), F32),
                        pltpu.VMEM((SUBLANES - 1, CONV_HALO + ts, A_WIDTH), F32),
                        pltpu.VMEM((LRU_HALO + ts, B_WIDTH), F32),
                        pltpu.VMEM((SUBLANES, B_WIDTH), F32),
                        pltpu.VMEM((ts, B_WIDTH), F32),
                        pltpu.VMEM((ts, 2 * B_WIDTH), F32),
                        pltpu.VMEM((2, ts, LANES), F32),
                        pltpu.VMEM((ts, D_MODEL), BF16)],
        compiler_params=pltpu.CompilerParams(
            dimension_semantics=("arbitrary", "arbitrary"), vmem_limit_bytes=VMEM_LIMIT),
        name="mixer0",
    )(x, w_in, b_in, conv_w, conv_b, cn_g, cn_b, lconv_w, lconv_b, w_gate, b_gate, lam)


def _post_kernel(m_ref, x_ref, w_out_ref, g1_ref, b1_ref, wg_ref, wu_ref, wd_ref, g2_ref, b2_ref,
                 out_ref, x1_ref, xb_ref, acc_ref):
    y = _dot(m_ref[...], w_out_ref[...])
    x1 = _ln(DN_ALPHA * x_ref[...] + y, g1_ref[...], b1_ref[...])
    x1_ref[...] = x1
    xb_ref[...] = x1.astype(BF16)
    for c in range(D_FF // FF_CHUNK):
        cs = slice(c * FF_CHUNK, (c + 1) * FF_CHUNK)
        gate = _dot(xb_ref[...], wg_ref[:, cs])
        up = _dot(xb_ref[...], wu_ref[:, cs])
        act = (jax.nn.silu(gate) * up).astype(BF16)
        contrib = _dot(act, wd_ref[cs, :])
        if c == 0:
            acc_ref[...] = contrib
        else:
            acc_ref[...] += contrib
    out_ref[...] = _ln(DN_ALPHA * x1_ref[...] + acc_ref[...], g2_ref[...], b2_ref[...])


def _post_call(m, x, w_out, g1, b1, wg, wu, wd, g2, b2):
    tm = POST_TM
    rows = m.shape[0]
    const = lambda i: (0, 0)
    full = lambda a: pl.BlockSpec(a.shape, const, pipeline_mode=pl.Buffered(1))
    return pl.pallas_call(
        _post_kernel,
        grid=(rows // tm,),
        in_specs=[pl.BlockSpec((tm, D_MODEL), lambda i: (i, 0)),
                  pl.BlockSpec((tm, D_MODEL), lambda i: (i, 0)),
                  full(w_out), full(g1), full(b1), full(wg), full(wu), full(wd), full(g2), full(b2)],
        out_specs=pl.BlockSpec((tm, D_MODEL), lambda i: (i, 0)),
        out_shape=jax.ShapeDtypeStruct((rows, D_MODEL), F32),
        scratch_shapes=[pltpu.VMEM((tm, D_MODEL), F32),
                        pltpu.VMEM((tm, D_MODEL), BF16),
                        pltpu.VMEM((tm, D_MODEL), F32)],
        compiler_params=pltpu.CompilerParams(
            dimension_semantics=("arbitrary",), vmem_limit_bytes=VMEM_LIMIT),
        name="post",
    )(m, x, w_out, g1, b1, wg, wu, wd, g2, b2)


def _qkv_kernel(x_ref, w_ref, cos_ref, sin_ref, q_ref, k_ref, v_ref):
    xb = x_ref[...].astype(BF16)
    cos = cos_ref[...]
    sin = sin_ref[...]
    half_dim = DIFF_HEAD_DIM // 2
    lane = lax.broadcasted_iota(jnp.int32, (1, LANES), 1)
    first_half = (lane % DIFF_HEAD_DIM) < half_dim
    for g in range(2 * QK_WIDTH // MXU_COLS):
        t2 = _dot(xb, w_ref[:, g * MXU_COLS:(g + 1) * MXU_COLS])
        for half in range(MXU_COLS // LANES):
            t = t2[:, half * LANES:(half + 1) * LANES]
            rot = jnp.where(first_half, pltpu.roll(t, LANES - half_dim, axis=1),
                            pltpu.roll(t, half_dim, axis=1))
            r = t * cos + rot * sin
            col = g * MXU_COLS + half * LANES
            if col < QK_WIDTH:
                q_ref[:, col:col + LANES] = (r * Q_SCALE).astype(BF16)
            else:
                k_ref[:, col - QK_WIDTH:col - QK_WIDTH + LANES] = r.astype(BF16)
    v_ref[...] = _dot(xb, w_ref[:, 2 * QK_WIDTH:]).astype(BF16)


def _qkv_call(x, w, cos, sin):
    tm = QKV_TM
    rows = x.shape[0]
    pos_blocks = SEQ // tm
    row_spec = pl.BlockSpec((tm, D_MODEL), lambda i: (i, 0))
    tab_spec = pl.BlockSpec((tm, LANES), lambda i: (i % pos_blocks, 0))
    out = jax.ShapeDtypeStruct((rows, QK_WIDTH), BF16)
    return pl.pallas_call(
        _qkv_kernel,
        grid=(rows // tm,),
        in_specs=[row_spec,
                  pl.BlockSpec(w.shape, lambda i: (0, 0), pipeline_mode=pl.Buffered(1)),
                  tab_spec, tab_spec],
        out_specs=[row_spec, row_spec, row_spec],
        out_shape=[out, out, out],
        compiler_params=pltpu.CompilerParams(
            dimension_semantics=("arbitrary",), vmem_limit_bytes=VMEM_LIMIT),
        name="qkv",
    )(x, w, cos, sin)


def _attn_kernel(lq1_ref, lk1_ref, lq2_ref, lk2_ref, g_ref, q_ref, k_ref, v_ref, o_ref,
                 qs_ref, m_ref, acc_ref, *, lambda_init):
    th = ATT_TH
    tq = 2 * th
    lane = lax.broadcasted_iota(jnp.int32, (1, LANES), 1)
    is_map1 = lane < DIFF_HEAD_DIM
    lam = (jnp.exp(jnp.sum(lq1_ref[...] * lk1_ref[...], axis=-1, keepdims=True))
           - jnp.exp(jnp.sum(lq2_ref[...] * lk2_ref[...], axis=-1, keepdims=True)) + lambda_init)
    zero = jnp.zeros((), BF16)

    def chunk_mask(n_rows):
        row = lax.broadcasted_iota(jnp.int32, (n_rows, th), 0) % th
        return row // CHUNK >= lax.broadcasted_iota(jnp.int32, (n_rows, th), 1) // CHUNK

    diag = chunk_mask(2 * th)
    first_cols = jnp.concatenate([diag, jnp.ones((2 * th, th), jnp.bool_)], axis=0)

    def update(h, rows, key_start, n_keys, visible):
        ks = pl.ds(key_start, n_keys)
        hs = slice(h * LANES, (h + 1) * LANES)
        s = lax.dot_general(qs_ref[h, rows, :], k_ref[0, ks, hs], (((1,), (1,)), ((), ())),
                            preferred_element_type=F32)
        if visible is not None:
            s = jnp.where(visible, s, NEG_INF)
        m_old = m_ref[h, rows, :]
        m_cur = jnp.max(s, axis=-1, keepdims=True)
        m_new = jnp.maximum(m_old, jnp.broadcast_to(m_cur, m_old.shape))
        alpha = jnp.exp2(m_old - m_new)
        e = jnp.exp2(s - jnp.concatenate([m_new] * (n_keys // LANES), axis=1)).astype(BF16)
        v_ext = jnp.concatenate([v_ref[0, ks, hs], jnp.ones((n_keys, LANES), BF16)], axis=1)
        acc_ref[h, rows, :] = (jnp.concatenate([alpha, alpha], axis=1) * acc_ref[h, rows, :]
                               + _dot(e, v_ext))
        m_ref[h, rows, :] = m_new

    all_rows = slice(0, 4 * th)
    second_half = slice(2 * th, 4 * th)

    def q_block(i, carry):
        q0 = pl.multiple_of(i * tq, tq)
        for h in range(ATT_HP):
            for half in range(2):
                q = q_ref[0, pl.ds(q0 + half * th, th), h * LANES:(h + 1) * LANES]
                qs_ref[h, (2 * half) * th:(2 * half + 1) * th, :] = jnp.where(is_map1, q, zero)
                qs_ref[h, (2 * half + 1) * th:(2 * half + 2) * th, :] = jnp.where(is_map1, zero, q)
        m_ref[...] = jnp.full(m_ref.shape, NEG_INF, F32)
        acc_ref[...] = jnp.zeros(acc_ref.shape, F32)

        def off_diag(j, c):
            for h in range(ATT_HP):
                update(h, all_rows, pl.multiple_of(j * tq, tq), tq, None)
            return c

        lax.fori_loop(0, i, off_diag, 0)
        for h in range(ATT_HP):
            update(h, all_rows, q0, th, first_cols)
            update(h, second_half, q0 + th, th, diag)

        for h in range(ATT_HP):
            for half in range(2):
                a1 = acc_ref[h, (2 * half) * th:(2 * half + 1) * th, :]
                a2 = acc_ref[h, (2 * half + 1) * th:(2 * half + 2) * th, :]
                o = (a1[:, :LANES] * (1.0 / a1[:, LANES:])
                     - a2[:, :LANES] * (lam / a2[:, LANES:]))
                o = o * lax.rsqrt(jnp.mean(o * o, axis=-1, keepdims=True) + LN_EPS) * g_ref[...]
                o = o * (1.0 - lambda_init)
                o_ref[0, pl.ds(q0 + half * th, th), h * LANES:(h + 1) * LANES] = o.astype(BF16)
        return carry

    lax.fori_loop(0, SEQ // tq, q_block, 0)


def _attn_call(lq1, lk1, lq2, lk2, g, q, k, v, lambda_init):
    const = lambda b, h: (0, 0)
    small = lambda a: pl.BlockSpec(a.shape, const)
    head_spec = pl.BlockSpec((1, SEQ, ATT_HP * LANES), lambda b, h: (b, 0, h))
    return pl.pallas_call(
        functools.partial(_attn_kernel, lambda_init=lambda_init),
        grid=(BATCH, DIFF_HEADS // ATT_HP),
        in_specs=[small(lq1), small(lk1), small(lq2), small(lk2), small(g),
                  head_spec, head_spec, head_spec],
        out_specs=head_spec,
        out_shape=jax.ShapeDtypeStruct((BATCH, SEQ, V_WIDTH), BF16),
        scratch_shapes=[pltpu.VMEM((ATT_HP, 4 * ATT_TH, LANES), BF16),
                        pltpu.VMEM((ATT_HP, 4 * ATT_TH, LANES), F32),
                        pltpu.VMEM((ATT_HP, 4 * ATT_TH, 2 * LANES), F32)],
        compiler_params=pltpu.CompilerParams(
            dimension_semantics=("arbitrary", "arbitrary"), vmem_limit_bytes=VMEM_LIMIT),
        name="diff_attn",
    )(lq1, lk1, lq2, lk2, g, q, k, v)


def _block_diag(w):
    same_block = np.eye(LRU_BLOCKS, dtype=np.float32)[:, None, :, None]
    return (w[:, :, None, :] * same_block).reshape(B_WIDTH, B_WIDTH)


def _rope_tables():
    half_dim = DIFF_HEAD_DIM // 2
    pos = np.arange(SEQ, dtype=np.float64)
    inv_freq = ROPE_THETA ** (-np.arange(0, DIFF_HEAD_DIM, 2, dtype=np.float64) / DIFF_HEAD_DIM)
    lane = np.arange(LANES)
    ang = pos[:, None] * inv_freq[lane % half_dim][None, :]
    sign = np.where(lane % DIFF_HEAD_DIM < half_dim, -1.0, 1.0)
    return (jnp.asarray(np.cos(ang), dtype=F32), jnp.asarray(np.sin(ang) * sign[None, :], dtype=F32))


def kernel(x, even_w_in, even_b_in, even_conv_w, even_conv_b, even_cnorm_g, even_cnorm_b,
           even_lru_conv_w, even_lru_conv_b, even_w_a, even_b_a, even_w_x, even_b_x,
           even_lru_lambda, even_w_out, odd_w_qkv, odd_lambda_q1, odd_lambda_k1,
           odd_lambda_q2, odd_lambda_k2, odd_subln_g, odd_w_out, mix_ln_g, mix_ln_b,
           ffn_w_gate, ffn_w_up, ffn_w_down, ffn_ln_g, ffn_ln_b):
    row = lambda a: a.reshape(1, -1)
    rows = BATCH * SEQ

    def post(m, xres, w_out, layer):
        return _post_call(m.reshape(rows, -1), xres.reshape(rows, D_MODEL), w_out.astype(BF16),
                          row(mix_ln_g[layer]), row(mix_ln_b[layer]),
                          ffn_w_gate[layer].astype(BF16), ffn_w_up[layer].astype(BF16),
                          ffn_w_down[layer].astype(BF16),
                          row(ffn_ln_g[layer]), row(ffn_ln_b[layer]))

    w_gate = jnp.concatenate([_block_diag(even_w_a[0]), _block_diag(even_w_x[0])], axis=1)
    b_gate = jnp.concatenate([even_b_a[0], even_b_x[0]]).reshape(1, -1)
    conv_w = jnp.broadcast_to(even_conv_w[0][:, None, :], (CONV_WIDTH, SUBLANES, A_WIDTH))
    m0 = _mixer_call(x, even_w_in[0].astype(BF16), row(even_b_in[0]), conv_w,
                     row(even_conv_b[0]), row(even_cnorm_g[0]), row(even_cnorm_b[0]),
                     even_lru_conv_w[0], row(even_lru_conv_b[0]), w_gate.astype(BF16), b_gate,
                     row(even_lru_lambda[0]))
    x1 = post(m0, x, even_w_out[0], 0)

    lambda_init = 0.8 - 0.6 * math.exp(-0.3 * 1)
    cos, sin = _rope_tables()
    q, k, v = _qkv_call(x1, odd_w_qkv[0].astype(BF16), cos, sin)
    shape3 = (BATCH, SEQ, QK_WIDTH)
    o = _attn_call(row(odd_lambda_q1[0]), row(odd_lambda_k1[0]), row(odd_lambda_q2[0]),
                   row(odd_lambda_k2[0]), row(odd_subln_g[0]),
                   q.reshape(shape3), k.reshape(shape3), v.reshape(shape3), lambda_init)
    out = post(o, x1, odd_w_out[0], 1)
    return out.reshape(BATCH, SEQ, D_MODEL)
```

```python
import functools
import math

import jax
import jax.numpy as jnp
import numpy as np
from jax import lax
from jax.experimental import pallas as pl
from jax.experimental.pallas import tpu as pltpu

F32 = jnp.float32
BF16 = jnp.bfloat16

D_MODEL = 1024
BATCH = 8
SEQ = 2048
DEPTH = 2
CHUNK = 64
A_WIDTH = 512
B_WIDTH = 512
CONV_WIDTH = 31
LRU_BLOCKS = 8
LRU_BLOCK_DIM = 64
LRU_CONV_WIDTH = 4
LRU_C = 8.0
IN_WIDTH = 2 * A_WIDTH + 2 * B_WIDTH
DIFF_HEADS = 8
DIFF_HEAD_DIM = 64
DIFF_V_DIM = 128
QK_WIDTH = 1024
V_WIDTH = 1024
ROPE_THETA = 10000.0
D_FF = 2816
LN_EPS = 1e-5
DN_ALPHA = (2 * DEPTH) ** 0.25
NEG_INF = -1e30
Q_SCALE = DIFF_HEAD_DIM ** -0.5 * math.log2(math.e)

LANES = 128
SUBLANES = 8
MXU_COLS = 256
VMEM_LIMIT = 56 * 1024 * 1024

MIX_TS = 512
CONV_HALO = 32
CONV_ROWS = 32
CONV_BLK = 64
LRU_HALO = 8
POST_TM = 512
FF_CHUNK = 256
QKV_TM = 512
ATT_HP = 4
ATT_TH = 256


def _ln(x, g, b):
    mu = jnp.mean(x, axis=-1, keepdims=True)
    xc = x - mu
    var = jnp.mean(xc * xc, axis=-1, keepdims=True)
    return xc * lax.rsqrt(var + LN_EPS) * g + b


def _dot(a, b):
    return jnp.dot(a, b, preferred_element_type=F32)


def _mixer_kernel(x_ref, w_in_ref, b_in_ref, conv_w_ref, conv_b_ref, cn_g_ref, cn_b_ref,
                  lconv_w_ref, lconv_b_ref, w_gate_ref, b_gate_ref, lam_ref,
                  out_ref, a_ext, a_sh, b_ext, h_carry, bgate_ref, gates_ref, au_ref, xb_ref,
                  conv_ref):
    ts = MIX_TS
    s = pl.program_id(1)

    n_groups = A_WIDTH // LANES

    @pl.when(s == 0)
    def _():
        a_ext[:, 0:CONV_HALO, :] = jnp.zeros((n_groups, CONV_HALO, LANES), F32)
        a_ext[:, CONV_HALO + ts:, :] = jnp.zeros((n_groups, CONV_BLK, LANES), F32)
        b_ext[0:LRU_HALO, :] = jnp.zeros((LRU_HALO, B_WIDTH), F32)
        h_carry[...] = jnp.zeros_like(h_carry)

    xb_ref[...] = x_ref[0].astype(BF16)
    ha = _dot(xb_ref[...], w_in_ref[:, 0:2 * A_WIDTH]) + b_in_ref[:, 0:2 * A_WIDTH]
    glu = ha[:, 0:A_WIDTH] * jax.nn.sigmoid(ha[:, A_WIDTH:])
    for g in range(n_groups):
        a_ext[g, CONV_HALO:CONV_HALO + ts, :] = glu[:, g * LANES:(g + 1) * LANES]

    off = CONV_HALO - (CONV_WIDTH - 1)

    for g in range(n_groups):
        def shift_body(i, carry, g=g):
            base = pl.multiple_of(i * CONV_BLK, CONV_BLK)
            win = a_ext[g, pl.ds(base, CONV_BLK + SUBLANES), :]
            for r in range(1, SUBLANES):
                rolled = pltpu.roll(win, CONV_BLK + SUBLANES - r, axis=0)
                a_sh[r - 1, g, pl.ds(base, CONV_BLK), :] = rolled[0:CONV_BLK, :]
            return carry

        lax.fori_loop(0, (ts + CONV_BLK) // CONV_BLK, shift_body, 0)

    for g in range(n_groups):
        sl = slice(g * LANES, (g + 1) * LANES)
        taps = [conv_w_ref[j, :, sl] for j in range(CONV_WIDTH)]
        bias = conv_b_ref[:, sl]

        def conv_body(i, carry, g=g, sl=sl, taps=taps, bias=bias):
            base = pl.multiple_of(i * CONV_BLK, CONV_BLK)
            acc = jnp.broadcast_to(bias, (CONV_BLK, LANES))
            for j in range(CONV_WIDTH):
                q, r = divmod(off + j, SUBLANES)
                rows = pl.ds(base + q * SUBLANES, CONV_BLK)
                win = a_ext[g, rows, :] if r == 0 else a_sh[r - 1, g, rows, :]
                acc = acc + (win.reshape(-1, SUBLANES, LANES) * taps[j][None]).reshape(CONV_BLK, LANES)
            conv_ref[pl.ds(base, CONV_BLK), sl] = acc
            return carry

        lax.fori_loop(0, ts // CONV_BLK, conv_body, 0)
    a_ext[:, 0:CONV_HALO, :] = a_ext[:, ts:ts + CONV_HALO, :]

    def norm_body(i, carry):
        base = pl.multiple_of(i * CONV_ROWS, CONV_ROWS)
        ya = jax.nn.silu(_ln(conv_ref[pl.ds(base, CONV_ROWS), :], cn_g_ref[...], cn_b_ref[...]))
        out_ref[0, pl.ds(base, CONV_ROWS), 0:A_WIDTH] = ya.astype(BF16)
        return carry

    lax.fori_loop(0, ts // CONV_ROWS, norm_body, 0, unroll=True)

    hb = _dot(xb_ref[...], w_in_ref[:, 2 * A_WIDTH:]) + b_in_ref[:, 2 * A_WIDTH:]
    bgate_ref[...] = hb[:, 0:B_WIDTH]
    b_ext[LRU_HALO:LRU_HALO + ts, :] = hb[:, B_WIDTH:]
    loff = LRU_HALO - (LRU_CONV_WIDTH - 1)
    xc = jnp.broadcast_to(lconv_b_ref[...], (ts, B_WIDTH))
    for j in range(LRU_CONV_WIDTH):
        xc = xc + lconv_w_ref[j:j + 1, :] * b_ext[loff + j:loff + j + ts, :]
    b_ext[0:LRU_HALO, :] = b_ext[ts:ts + LRU_HALO, :]
    gates_ref[...] = _dot(xc.astype(BF16), w_gate_ref[...]) + b_gate_ref[...]
    b_ext[LRU_HALO:LRU_HALO + ts, :] = xc

    def scan_steps(a_cum, u_cum, axis, steps):
        idx = lax.broadcasted_iota(jnp.int32, a_cum.shape, axis)
        for step in steps:
            keep = idx >= step
            a_prev = jnp.where(keep, pltpu.roll(a_cum, step, axis=axis), 1.0)
            u_prev = jnp.where(keep, pltpu.roll(u_cum, step, axis=axis), 0.0)
            u_cum = a_cum * u_prev + u_cum
            a_cum = a_cum * a_prev
        return a_cum, u_cum

    nblk = ts // SUBLANES
    for g in range(B_WIDTH // LANES):
        sl = slice(g * LANES, (g + 1) * LANES)
        xg = b_ext[LRU_HALO:LRU_HALO + ts, sl]
        gate_r = jax.nn.sigmoid(gates_ref[:, sl])
        gate_i = jax.nn.sigmoid(gates_ref[:, B_WIDTH + g * LANES:B_WIDTH + (g + 1) * LANES])
        lam = lam_ref[:, sl]
        neg = -lam
        softplus = jnp.maximum(neg, 0.0) + jnp.log1p(jnp.exp(-jnp.abs(neg)))
        log_a = (-LRU_C * gate_r) * softplus
        th = jnp.tanh(log_a)
        y = -2.0 * th
        scale = jnp.where(y > 0.0, y * lax.rsqrt(y), 0.0) * lax.rsqrt(1.0 - th)
        a_blk, u_blk = scan_steps(jnp.exp(log_a).reshape(nblk, SUBLANES, LANES),
                                  (scale * (gate_i * xg)).reshape(nblk, SUBLANES, LANES),
                                  1, (1, 2, 4))
        au_ref[0] = a_blk.reshape(ts, LANES)
        au_ref[1] = u_blk.reshape(ts, LANES)
        last = pl.ds(SUBLANES - 1, nblk, stride=SUBLANES)
        a_end, u_end = scan_steps(au_ref[0, last, :], au_ref[1, last, :], 0,
                                  [1 << k for k in range(nblk.bit_length() - 1)])
        h_prev = h_carry[0:1, sl]
        h_end = a_end * h_prev + u_end
        h_carry[:, sl] = jnp.broadcast_to(h_end[nblk - 1:nblk, :], (SUBLANES, LANES))
        row0 = lax.broadcasted_iota(jnp.int32, (nblk, LANES), 0) == 0
        h_in = jnp.where(row0, h_prev, pltpu.roll(h_end, 1, axis=0))
        hg = (a_blk * h_in[:, None, :] + u_blk).reshape(ts, LANES)
        yb = hg * jax.nn.gelu(bgate_ref[:, sl])
        out_ref[0, :, A_WIDTH + g * LANES:A_WIDTH + (g + 1) * LANES] = yb.astype(BF16)


def _mixer_call(x, w_in, b_in, conv_w, conv_b, cn_g, cn_b, lconv_w, lconv_b, w_gate, b_gate, lam):
    ts = MIX_TS
    full = lambda a: pl.BlockSpec(a.shape, lambda b, s, nd=a.ndim: (0,) * nd,
                                  pipeline_mode=pl.Buffered(1))
    return pl.pallas_call(
        _mixer_kernel,
        grid=(BATCH, SEQ // ts),
        in_specs=[pl.BlockSpec((1, ts, D_MODEL), lambda b, s: (b, s, 0)),
                  full(w_in), full(b_in), full(conv_w), full(conv_b), full(cn_g), full(cn_b),
                  full(lconv_w), full(lconv_b), full(w_gate), full(b_gate), full(lam)],
        out_specs=pl.BlockSpec((1, ts, A_WIDTH + B_WIDTH), lambda b, s: (b, s, 0)),
        out_shape=jax.ShapeDtypeStruct((BATCH, SEQ, A_WIDTH + B_WIDTH), BF16),
        scratch_shapes=[pltpu.VMEM((A_WIDTH // LANES, CONV_HALO + ts + CONV_BLK, LANES), F32),
                        pltpu.VMEM((SUBLANES - 1, A_WIDTH // LANES, ts + CONV_BLK, LANES), F32),
                        pltpu.VMEM((LRU_HALO + ts, B_WIDTH), F32),
                        pltpu.VMEM((SUBLANES, B_WIDTH), F32),
                        pltpu.VMEM((ts, B_WIDTH), F32),
                        pltpu.VMEM((ts, 2 * B_WIDTH), F32),
                        pltpu.VMEM((2, ts, LANES), F32),
                        pltpu.VMEM((ts, D_MODEL), BF16),
                        pltpu.VMEM((ts, A_WIDTH), F32)],
        compiler_params=pltpu.CompilerParams(
            dimension_semantics=("arbitrary", "arbitrary"), vmem_limit_bytes=VMEM_LIMIT),
        name="mixer0",
    )(x, w_in, b_in, conv_w, conv_b, cn_g, cn_b, lconv_w, lconv_b, w_gate, b_gate, lam)


def _post_kernel(m_ref, x_ref, w_out_ref, g1_ref, b1_ref, wg_ref, wu_ref, wd_ref, g2_ref, b2_ref,
                 out_ref, x1_ref, xb_ref, acc_ref):
    y = _dot(m_ref[...], w_out_ref[...])
    x1 = _ln(DN_ALPHA * x_ref[...] + y, g1_ref[...], b1_ref[...])
    x1_ref[...] = x1
    xb_ref[...] = x1.astype(BF16)
    for c in range(D_FF // FF_CHUNK):
        cs = slice(c * FF_CHUNK, (c + 1) * FF_CHUNK)
        gate = _dot(xb_ref[...], wg_ref[:, cs])
        up = _dot(xb_ref[...], wu_ref[:, cs])
        act = (jax.nn.silu(gate) * up).astype(BF16)
        contrib = _dot(act, wd_ref[cs, :])
        if c == 0:
            acc_ref[...] = contrib
        else:
            acc_ref[...] += contrib
    out_ref[...] = _ln(DN_ALPHA * x1_ref[...] + acc_ref[...], g2_ref[...], b2_ref[...])


def _post_call(m, x, w_out, g1, b1, wg, wu, wd, g2, b2, layer):
    tm = POST_TM
    rows = m.shape[0]
    const = lambda i: (0, 0)
    full = lambda a: pl.BlockSpec(a.shape, const, pipeline_mode=pl.Buffered(1))
    of_layer = lambda a: pl.BlockSpec((None,) + a.shape[1:], lambda i: (layer, 0, 0),
                                      pipeline_mode=pl.Buffered(1))
    return pl.pallas_call(
        _post_kernel,
        grid=(rows // tm,),
        in_specs=[pl.BlockSpec((tm, D_MODEL), lambda i: (i, 0)),
                  pl.BlockSpec((tm, D_MODEL), lambda i: (i, 0)),
                  full(w_out), full(g1), full(b1), of_layer(wg), of_layer(wu), of_layer(wd),
                  full(g2), full(b2)],
        out_specs=pl.BlockSpec((tm, D_MODEL), lambda i: (i, 0)),
        out_shape=jax.ShapeDtypeStruct((rows, D_MODEL), F32),
        scratch_shapes=[pltpu.VMEM((tm, D_MODEL), F32),
                        pltpu.VMEM((tm, D_MODEL), BF16),
                        pltpu.VMEM((tm, D_MODEL), F32)],
        compiler_params=pltpu.CompilerParams(
            dimension_semantics=("arbitrary",), vmem_limit_bytes=VMEM_LIMIT),
        name="post",
    )(m, x, w_out, g1, b1, wg, wu, wd, g2, b2)


def _qkv_kernel(x_ref, w_ref, cos_ref, sin_ref, q_ref, k_ref, v_ref):
    xb = x_ref[...].astype(BF16)
    cos = cos_ref[...]
    sin = sin_ref[...]
    half_dim = DIFF_HEAD_DIM // 2
    lane = lax.broadcasted_iota(jnp.int32, (1, LANES), 1)
    first_half = (lane % DIFF_HEAD_DIM) < half_dim
    for g in range(2 * QK_WIDTH // MXU_COLS):
        t2 = _dot(xb, w_ref[:, g * MXU_COLS:(g + 1) * MXU_COLS])
        for half in range(MXU_COLS // LANES):
            t = t2[:, half * LANES:(half + 1) * LANES]
            rot = jnp.where(first_half, pltpu.roll(t, LANES - half_dim, axis=1),
                            pltpu.roll(t, half_dim, axis=1))
            r = t * cos + rot * sin
            col = g * MXU_COLS + half * LANES
            if col < QK_WIDTH:
                q_ref[:, col:col + LANES] = (r * Q_SCALE).astype(BF16)
            else:
                k_ref[:, col - QK_WIDTH:col - QK_WIDTH + LANES] = r.astype(BF16)
    v_ref[...] = _dot(xb, w_ref[:, 2 * QK_WIDTH:]).astype(BF16)


def _qkv_call(x, w, cos, sin):
    tm = QKV_TM
    rows = x.shape[0]
    pos_blocks = SEQ // tm
    row_spec = pl.BlockSpec((tm, D_MODEL), lambda i: (i, 0))
    tab_spec = pl.BlockSpec((tm, LANES), lambda i: (i % pos_blocks, 0))
    out = jax.ShapeDtypeStruct((rows, QK_WIDTH), BF16)
    return pl.pallas_call(
        _qkv_kernel,
        grid=(rows // tm,),
        in_specs=[row_spec,
                  pl.BlockSpec(w.shape, lambda i: (0, 0), pipeline_mode=pl.Buffered(1)),
                  tab_spec, tab_spec],
        out_specs=[row_spec, row_spec, row_spec],
        out_shape=[out, out, out],
        compiler_params=pltpu.CompilerParams(
            dimension_semantics=("arbitrary",), vmem_limit_bytes=VMEM_LIMIT),
        name="qkv",
    )(x, w, cos, sin)


def _attn_kernel(lq1_ref, lk1_ref, lq2_ref, lk2_ref, g_ref, q_ref, k_ref, v_ref, o_ref,
                 qs_ref, m_ref, acc_ref, *, lambda_init):
    th = ATT_TH
    tq = 2 * th
    lane = lax.broadcasted_iota(jnp.int32, (1, LANES), 1)
    is_map1 = lane < DIFF_HEAD_DIM
    lam = (jnp.exp(jnp.sum(lq1_ref[...] * lk1_ref[...], axis=-1, keepdims=True))
           - jnp.exp(jnp.sum(lq2_ref[...] * lk2_ref[...], axis=-1, keepdims=True)) + lambda_init)
    zero = jnp.zeros((), BF16)

    def chunk_mask(n_rows):
        row = lax.broadcasted_iota(jnp.int32, (n_rows, th), 0) % th
        return row // CHUNK >= lax.broadcasted_iota(jnp.int32, (n_rows, th), 1) // CHUNK

    diag = chunk_mask(2 * th)
    first_cols = jnp.concatenate([diag, jnp.ones((2 * th, th), jnp.bool_)], axis=0)

    def update(h, rows, key_start, n_keys, visible):
        ks = pl.ds(key_start, n_keys)
        hs = slice(h * LANES, (h + 1) * LANES)
        s = lax.dot_general(qs_ref[h, rows, :], k_ref[0, ks, hs], (((1,), (1,)), ((), ())),
                            preferred_element_type=F32)
        if visible is not None:
            s = jnp.where(visible, s, NEG_INF)
        m_old = m_ref[h, rows, :]
        m_cur = jnp.max(s, axis=-1, keepdims=True)
        m_new = jnp.maximum(m_old, jnp.broadcast_to(m_cur, m_old.shape))
        alpha = jnp.exp2(m_old - m_new)
        e = jnp.exp2(s - jnp.concatenate([m_new] * (n_keys // LANES), axis=1)).astype(BF16)
        v_ext = jnp.concatenate([v_ref[0, ks, hs], jnp.ones((n_keys, LANES), BF16)], axis=1)
        acc_ref[h, rows, :] = (jnp.concatenate([alpha, alpha], axis=1) * acc_ref[h, rows, :]
                               + _dot(e, v_ext))
        m_ref[h, rows, :] = m_new

    all_rows = slice(0, 4 * th)
    second_half = slice(2 * th, 4 * th)

    def q_block(i, carry):
        q0 = pl.multiple_of(i * tq, tq)
        for h in range(ATT_HP):
            for half in range(2):
                q = q_ref[0, pl.ds(q0 + half * th, th), h * LANES:(h + 1) * LANES]
                qs_ref[h, (2 * half) * th:(2 * half + 1) * th, :] = jnp.where(is_map1, q, zero)
                qs_ref[h, (2 * half + 1) * th:(2 * half + 2) * th, :] = jnp.where(is_map1, zero, q)
        m_ref[...] = jnp.full(m_ref.shape, NEG_INF, F32)
        acc_ref[...] = jnp.zeros(acc_ref.shape, F32)

        def off_diag(j, c):
            for h in range(ATT_HP):
                update(h, all_rows, pl.multiple_of(j * tq, tq), tq, None)
            return c

        lax.fori_loop(0, i, off_diag, 0)
        for h in range(ATT_HP):
            update(h, all_rows, q0, th, first_cols)
            update(h, second_half, q0 + th, th, diag)

        for h in range(ATT_HP):
            for half in range(2):
                a1 = acc_ref[h, (2 * half) * th:(2 * half + 1) * th, :]
                a2 = acc_ref[h, (2 * half + 1) * th:(2 * half + 2) * th, :]
                o = (a1[:, :LANES] * (1.0 / a1[:, LANES:])
                     - a2[:, :LANES] * (lam / a2[:, LANES:]))
                o = o * lax.rsqrt(jnp.mean(o * o, axis=-1, keepdims=True) + LN_EPS) * g_ref[...]
                o = o * (1.0 - lambda_init)
                o_ref[0, pl.ds(q0 + half * th, th), h * LANES:(h + 1) * LANES] = o.astype(BF16)
        return carry

    lax.fori_loop(0, SEQ // tq, q_block, 0)


def _attn_call(lq1, lk1, lq2, lk2, g, q, k, v, lambda_init):
    const = lambda b, h: (0, 0)
    small = lambda a: pl.BlockSpec(a.shape, const)
    head_spec = pl.BlockSpec((1, SEQ, ATT_HP * LANES), lambda b, h: (b, 0, h))
    return pl.pallas_call(
        functools.partial(_attn_kernel, lambda_init=lambda_init),
        grid=(BATCH, DIFF_HEADS // ATT_HP),
        in_specs=[small(lq1), small(lk1), small(lq2), small(lk2), small(g),
                  head_spec, head_spec, head_spec],
        out_specs=head_spec,
        out_shape=jax.ShapeDtypeStruct((BATCH, SEQ, V_WIDTH), BF16),
        scratch_shapes=[pltpu.VMEM((ATT_HP, 4 * ATT_TH, LANES), BF16),
                        pltpu.VMEM((ATT_HP, 4 * ATT_TH, LANES), F32),
                        pltpu.VMEM((ATT_HP, 4 * ATT_TH, 2 * LANES), F32)],
        compiler_params=pltpu.CompilerParams(
            dimension_semantics=("arbitrary", "arbitrary"), vmem_limit_bytes=VMEM_LIMIT),
        name="diff_attn",
    )(lq1, lk1, lq2, lk2, g, q, k, v)


def _block_diag(w):
    same_block = np.eye(LRU_BLOCKS, dtype=np.float32)[:, None, :, None]
    return (w[:, :, None, :] * same_block).reshape(B_WIDTH, B_WIDTH)


def _rope_tables():
    half_dim = DIFF_HEAD_DIM // 2
    pos = np.arange(SEQ, dtype=np.float64)
    inv_freq = ROPE_THETA ** (-np.arange(0, DIFF_HEAD_DIM, 2, dtype=np.float64) / DIFF_HEAD_DIM)
    lane = np.arange(LANES)
    ang = pos[:, None] * inv_freq[lane % half_dim][None, :]
    sign = np.where(lane % DIFF_HEAD_DIM < half_dim, -1.0, 1.0)
    return (jnp.asarray(np.cos(ang), dtype=F32), jnp.asarray(np.sin(ang) * sign[None, :], dtype=F32))


def kernel(x, even_w_in, even_b_in, even_conv_w, even_conv_b, even_cnorm_g, even_cnorm_b,
           even_lru_conv_w, even_lru_conv_b, even_w_a, even_b_a, even_w_x, even_b_x,
           even_lru_lambda, even_w_out, odd_w_qkv, odd_lambda_q1, odd_lambda_k1,
           odd_lambda_q2, odd_lambda_k2, odd_subln_g, odd_w_out, mix_ln_g, mix_ln_b,
           ffn_w_gate, ffn_w_up, ffn_w_down, ffn_ln_g, ffn_ln_b):
    row = lambda a: a.reshape(1, -1)
    rows = BATCH * SEQ

    wg_all = ffn_w_gate.astype(BF16)
    wu_all = ffn_w_up.astype(BF16)
    wd_all = ffn_w_down.astype(BF16)

    def post(m, xres, w_out, layer):
        return _post_call(m.reshape(rows, -1), xres.reshape(rows, D_MODEL), w_out.astype(BF16),
                          row(mix_ln_g[layer]), row(mix_ln_b[layer]), wg_all, wu_all, wd_all,
                          row(ffn_ln_g[layer]), row(ffn_ln_b[layer]), layer)

    w_gate = jnp.concatenate([_block_diag(even_w_a[0]), _block_diag(even_w_x[0])], axis=1)
    b_gate = jnp.concatenate([even_b_a[0], even_b_x[0]]).reshape(1, -1)
    conv_w = jnp.broadcast_to(even_conv_w[0][:, None, :], (CONV_WIDTH, SUBLANES, A_WIDTH))
    m0 = _mixer_call(x, even_w_in[0].astype(BF16), row(even_b_in[0]), conv_w,
                     row(even_conv_b[0]), row(even_cnorm_g[0]), row(even_cnorm_b[0]),
                     even_lru_conv_w[0], row(even_lru_conv_b[0]), w_gate.astype(BF16), b_gate,
                     row(even_lru_lambda[0]))
    x1 = post(m0, x, even_w_out[0], 0)

    lambda_init = 0.8 - 0.6 * math.exp(-0.3 * 1)
    cos, sin = _rope_tables()
    q, k, v = _qkv_call(x1, odd_w_qkv[0].astype(BF16), cos, sin)
    shape3 = (BATCH, SEQ, QK_WIDTH)
    o = _attn_call(row(odd_lambda_q1[0]), row(odd_lambda_k1[0]), row(odd_lambda_q2[0]),
                   row(odd_lambda_k2[0]), row(odd_subln_g[0]),
                   q.reshape(shape3), k.reshape(shape3), v.reshape(shape3), lambda_init)
    out = post(o, x1, odd_w_out[0], 1)
    return out.reshape(BATCH, SEQ, D_MODEL)
```

```python
import functools
import math

import jax
import jax.numpy as jnp
import numpy as np
from jax import lax
from jax.experimental import pallas as pl
from jax.experimental.pallas import tpu as pltpu

F32 = jnp.float32
BF16 = jnp.bfloat16

D_MODEL = 1024
BATCH = 8
SEQ = 2048
DEPTH = 2
CHUNK = 64
A_WIDTH = 512
B_WIDTH = 512
CONV_WIDTH = 31
LRU_BLOCKS = 8
LRU_BLOCK_DIM = 64
LRU_CONV_WIDTH = 4
LRU_C = 8.0
IN_WIDTH = 2 * A_WIDTH + 2 * B_WIDTH
DIFF_HEADS = 8
DIFF_HEAD_DIM = 64
DIFF_V_DIM = 128
QK_WIDTH = 1024
V_WIDTH = 1024
ROPE_THETA = 10000.0
D_FF = 2816
LN_EPS = 1e-5
DN_ALPHA = (2 * DEPTH) ** 0.25
NEG_INF = -1e30
Q_SCALE = DIFF_HEAD_DIM ** -0.5 * math.log2(math.e)

LANES = 128
SUBLANES = 8
MXU_COLS = 256
VMEM_LIMIT = 56 * 1024 * 1024

MIX_TS = 512
CONV_HALO = 32
CONV_ROWS = 32
CONV_BLK = 64
LRU_HALO = 8
POST_TM = 512
FF_CHUNK = 256
QKV_TM = 512
ATT_HP = 8
ATT_TH = 256


def _ln(x, g, b):
    mu = jnp.mean(x, axis=-1, keepdims=True)
    xc = x - mu
    var = jnp.mean(xc * xc, axis=-1, keepdims=True)
    return xc * lax.rsqrt(var + LN_EPS) * g + b


def _dot(a, b):
    return jnp.dot(a, b, preferred_element_type=F32)


def _mixer_kernel(x_ref, w_in_ref, b_in_ref, conv_w_ref, conv_b_ref, cn_g_ref, cn_b_ref,
                  lconv_w_ref, lconv_b_ref, w_gate_ref, b_gate_ref, lam_ref,
                  out_ref, a_ext, a_sh, b_ext, h_carry, bgate_ref, gates_ref, au_ref, xb_ref,
                  conv_ref):
    ts = MIX_TS
    s = pl.program_id(1)

    n_groups = A_WIDTH // LANES

    @pl.when(s == 0)
    def _():
        a_ext[:, 0:CONV_HALO, :] = jnp.zeros((n_groups, CONV_HALO, LANES), F32)
        a_ext[:, CONV_HALO + ts:, :] = jnp.zeros((n_groups, CONV_BLK, LANES), F32)
        b_ext[0:LRU_HALO, :] = jnp.zeros((LRU_HALO, B_WIDTH), F32)
        h_carry[...] = jnp.zeros_like(h_carry)

    xb_ref[...] = x_ref[0].astype(BF16)
    ha = _dot(xb_ref[...], w_in_ref[:, 0:2 * A_WIDTH]) + b_in_ref[:, 0:2 * A_WIDTH]
    glu = ha[:, 0:A_WIDTH] * jax.nn.sigmoid(ha[:, A_WIDTH:])
    for g in range(n_groups):
        a_ext[g, CONV_HALO:CONV_HALO + ts, :] = glu[:, g * LANES:(g + 1) * LANES]

    off = CONV_HALO - (CONV_WIDTH - 1)

    for g in range(n_groups):
        def shift_body(i, carry, g=g):
            base = pl.multiple_of(i * CONV_BLK, CONV_BLK)
            win = a_ext[g, pl.ds(base, CONV_BLK + SUBLANES), :]
            for r in range(1, SUBLANES):
                rolled = pltpu.roll(win, CONV_BLK + SUBLANES - r, axis=0)
                a_sh[r - 1, g, pl.ds(base, CONV_BLK), :] = rolled[0:CONV_BLK, :]
            return carry

        lax.fori_loop(0, (ts + CONV_BLK) // CONV_BLK, shift_body, 0)

    for g in range(n_groups):
        sl = slice(g * LANES, (g + 1) * LANES)
        taps = [conv_w_ref[j, :, sl] for j in range(CONV_WIDTH)]
        bias = conv_b_ref[:, sl]

        def conv_body(i, carry, g=g, sl=sl, taps=taps, bias=bias):
            base = pl.multiple_of(i * CONV_BLK, CONV_BLK)
            acc = jnp.broadcast_to(bias, (CONV_BLK, LANES))
            for j in range(CONV_WIDTH):
                q, r = divmod(off + j, SUBLANES)
                rows = pl.ds(base + q * SUBLANES, CONV_BLK)
                win = a_ext[g, rows, :] if r == 0 else a_sh[r - 1, g, rows, :]
                acc = acc + (win.reshape(-1, SUBLANES, LANES) * taps[j][None]).reshape(CONV_BLK, LANES)
            conv_ref[pl.ds(base, CONV_BLK), sl] = acc
            return carry

        lax.fori_loop(0, ts // CONV_BLK, conv_body, 0)
    a_ext[:, 0:CONV_HALO, :] = a_ext[:, ts:ts + CONV_HALO, :]

    def norm_body(i, carry):
        base = pl.multiple_of(i * CONV_ROWS, CONV_ROWS)
        ya = jax.nn.silu(_ln(conv_ref[pl.ds(base, CONV_ROWS), :], cn_g_ref[...], cn_b_ref[...]))
        out_ref[0, pl.ds(base, CONV_ROWS), 0:A_WIDTH] = ya.astype(BF16)
        return carry

    lax.fori_loop(0, ts // CONV_ROWS, norm_body, 0, unroll=True)

    hb = _dot(xb_ref[...], w_in_ref[:, 2 * A_WIDTH:]) + b_in_ref[:, 2 * A_WIDTH:]
    bgate_ref[...] = hb[:, 0:B_WIDTH]
    b_ext[LRU_HALO:LRU_HALO + ts, :] = hb[:, B_WIDTH:]
    loff = LRU_HALO - (LRU_CONV_WIDTH - 1)
    xc = jnp.broadcast_to(lconv_b_ref[...], (ts, B_WIDTH))
    for j in range(LRU_CONV_WIDTH):
        xc = xc + lconv_w_ref[j:j + 1, :] * b_ext[loff + j:loff + j + ts, :]
    b_ext[0:LRU_HALO, :] = b_ext[ts:ts + LRU_HALO, :]
    gates_ref[...] = _dot(xc.astype(BF16), w_gate_ref[...]) + b_gate_ref[...]
    b_ext[LRU_HALO:LRU_HALO + ts, :] = xc

    def scan_steps(a_cum, u_cum, axis, steps):
        idx = lax.broadcasted_iota(jnp.int32, a_cum.shape, axis)
        for step in steps:
            keep = idx >= step
            a_prev = jnp.where(keep, pltpu.roll(a_cum, step, axis=axis), 1.0)
            u_prev = jnp.where(keep, pltpu.roll(u_cum, step, axis=axis), 0.0)
            u_cum = a_cum * u_prev + u_cum
            a_cum = a_cum * a_prev
        return a_cum, u_cum

    nblk = ts // SUBLANES
    for g in range(B_WIDTH // LANES):
        sl = slice(g * LANES, (g + 1) * LANES)
        xg = b_ext[LRU_HALO:LRU_HALO + ts, sl]
        gate_r = jax.nn.sigmoid(gates_ref[:, sl])
        gate_i = jax.nn.sigmoid(gates_ref[:, B_WIDTH + g * LANES:B_WIDTH + (g + 1) * LANES])
        lam = lam_ref[:, sl]
        neg = -lam
        softplus = jnp.maximum(neg, 0.0) + jnp.log1p(jnp.exp(-jnp.abs(neg)))
        log_a = (-LRU_C * gate_r) * softplus
        th = jnp.tanh(log_a)
        y = -2.0 * th
        scale = jnp.where(y > 0.0, y * lax.rsqrt(y), 0.0) * lax.rsqrt(1.0 - th)
        a_blk, u_blk = scan_steps(jnp.exp(log_a).reshape(nblk, SUBLANES, LANES),
                                  (scale * (gate_i * xg)).reshape(nblk, SUBLANES, LANES),
                                  1, (1, 2, 4))
        au_ref[0] = a_blk.reshape(ts, LANES)
        au_ref[1] = u_blk.reshape(ts, LANES)
        last = pl.ds(SUBLANES - 1, nblk, stride=SUBLANES)
        a_end, u_end = scan_steps(au_ref[0, last, :], au_ref[1, last, :], 0,
                                  [1 << k for k in range(nblk.bit_length() - 1)])
        h_prev = h_carry[0:1, sl]
        h_end = a_end * h_prev + u_end
        h_carry[:, sl] = jnp.broadcast_to(h_end[nblk - 1:nblk, :], (SUBLANES, LANES))
        row0 = lax.broadcasted_iota(jnp.int32, (nblk, LANES), 0) == 0
        h_in = jnp.where(row0, h_prev, pltpu.roll(h_end, 1, axis=0))
        hg = (a_blk * h_in[:, None, :] + u_blk).reshape(ts, LANES)
        yb = hg * jax.nn.gelu(bgate_ref[:, sl])
        out_ref[0, :, A_WIDTH + g * LANES:A_WIDTH + (g + 1) * LANES] = yb.astype(BF16)


def _mixer_call(x, w_in, b_in, conv_w, conv_b, cn_g, cn_b, lconv_w, lconv_b, w_gate, b_gate, lam):
    ts = MIX_TS
    full = lambda a: pl.BlockSpec(a.shape, lambda b, s, nd=a.ndim: (0,) * nd,
                                  pipeline_mode=pl.Buffered(1))
    return pl.pallas_call(
        _mixer_kernel,
        grid=(BATCH, SEQ // ts),
        in_specs=[pl.BlockSpec((1, ts, D_MODEL), lambda b, s: (b, s, 0)),
                  full(w_in), full(b_in), full(conv_w), full(conv_b), full(cn_g), full(cn_b),
                  full(lconv_w), full(lconv_b), full(w_gate), full(b_gate), full(lam)],
        out_specs=pl.BlockSpec((1, ts, A_WIDTH + B_WIDTH), lambda b, s: (b, s, 0)),
        out_shape=jax.ShapeDtypeStruct((BATCH, SEQ, A_WIDTH + B_WIDTH), BF16),
        scratch_shapes=[pltpu.VMEM((A_WIDTH // LANES, CONV_HALO + ts + CONV_BLK, LANES), F32),
                        pltpu.VMEM((SUBLANES - 1, A_WIDTH // LANES, ts + CONV_BLK, LANES), F32),
                        pltpu.VMEM((LRU_HALO + ts, B_WIDTH), F32),
                        pltpu.VMEM((SUBLANES, B_WIDTH), F32),
                        pltpu.VMEM((ts, B_WIDTH), F32),
                        pltpu.VMEM((ts, 2 * B_WIDTH), F32),
                        pltpu.VMEM((2, ts, LANES), F32),
                        pltpu.VMEM((ts, D_MODEL), BF16),
                        pltpu.VMEM((ts, A_WIDTH), F32)],
        compiler_params=pltpu.CompilerParams(
            dimension_semantics=("arbitrary", "arbitrary"), vmem_limit_bytes=VMEM_LIMIT),
        name="mixer0",
    )(x, w_in, b_in, conv_w, conv_b, cn_g, cn_b, lconv_w, lconv_b, w_gate, b_gate, lam)


def _post_kernel(m_ref, x_ref, w_out_ref, g1_ref, b1_ref, wg_ref, wu_ref, wd_ref, g2_ref, b2_ref,
                 out_ref, x1_ref, xb_ref, acc_ref):
    y = _dot(m_ref[...], w_out_ref[...])
    x1 = _ln(DN_ALPHA * x_ref[...] + y, g1_ref[...], b1_ref[...])
    x1_ref[...] = x1
    xb_ref[...] = x1.astype(BF16)
    for c in range(D_FF // FF_CHUNK):
        cs = slice(c * FF_CHUNK, (c + 1) * FF_CHUNK)
        gate = _dot(xb_ref[...], wg_ref[:, cs])
        up = _dot(xb_ref[...], wu_ref[:, cs])
        act = (jax.nn.silu(gate) * up).astype(BF16)
        contrib = _dot(act, wd_ref[cs, :])
        if c == 0:
            acc_ref[...] = contrib
        else:
            acc_ref[...] += contrib
    out_ref[...] = _ln(DN_ALPHA * x1_ref[...] + acc_ref[...], g2_ref[...], b2_ref[...])


def _post_call(m, x, w_out, g1, b1, wg, wu, wd, g2, b2, layer):
    tm = POST_TM
    rows = m.shape[0]
    const = lambda i: (0, 0)
    full = lambda a: pl.BlockSpec(a.shape, const, pipeline_mode=pl.Buffered(1))
    of_layer = lambda a: pl.BlockSpec((None,) + a.shape[1:], lambda i: (layer, 0, 0),
                                      pipeline_mode=pl.Buffered(1))
    return pl.pallas_call(
        _post_kernel,
        grid=(rows // tm,),
        in_specs=[pl.BlockSpec((tm, D_MODEL), lambda i: (i, 0)),
                  pl.BlockSpec((tm, D_MODEL), lambda i: (i, 0)),
                  full(w_out), full(g1), full(b1), of_layer(wg), of_layer(wu), of_layer(wd),
                  full(g2), full(b2)],
        out_specs=pl.BlockSpec((tm, D_MODEL), lambda i: (i, 0)),
        out_shape=jax.ShapeDtypeStruct((rows, D_MODEL), F32),
        scratch_shapes=[pltpu.VMEM((tm, D_MODEL), F32),
                        pltpu.VMEM((tm, D_MODEL), BF16),
                        pltpu.VMEM((tm, D_MODEL), F32)],
        compiler_params=pltpu.CompilerParams(
            dimension_semantics=("arbitrary",), vmem_limit_bytes=VMEM_LIMIT),
        name="post",
    )(m, x, w_out, g1, b1, wg, wu, wd, g2, b2)


def _qkv_kernel(x_ref, w_ref, cos_ref, sin_ref, q_ref, k_ref, v_ref):
    xb = x_ref[...].astype(BF16)
    cos = cos_ref[...]
    sin = sin_ref[...]
    half_dim = DIFF_HEAD_DIM // 2
    lane = lax.broadcasted_iota(jnp.int32, (1, LANES), 1)
    first_half = (lane % DIFF_HEAD_DIM) < half_dim
    for g in range(2 * QK_WIDTH // MXU_COLS):
        t2 = _dot(xb, w_ref[:, g * MXU_COLS:(g + 1) * MXU_COLS])
        for half in range(MXU_COLS // LANES):
            t = t2[:, half * LANES:(half + 1) * LANES]
            rot = jnp.where(first_half, pltpu.roll(t, LANES - half_dim, axis=1),
                            pltpu.roll(t, half_dim, axis=1))
            r = t * cos + rot * sin
            col = g * MXU_COLS + half * LANES
            if col < QK_WIDTH:
                q_ref[:, col:col + LANES] = (r * Q_SCALE).astype(BF16)
            else:
                k_ref[:, col - QK_WIDTH:col - QK_WIDTH + LANES] = r.astype(BF16)
    v_ref[...] = _dot(xb, w_ref[:, 2 * QK_WIDTH:]).astype(BF16)


def _qkv_call(x, w, cos, sin):
    tm = QKV_TM
    rows = x.shape[0]
    pos_blocks = SEQ // tm
    row_spec = pl.BlockSpec((tm, D_MODEL), lambda i: (i, 0))
    tab_spec = pl.BlockSpec((tm, LANES), lambda i: (i % pos_blocks, 0))
    out = jax.ShapeDtypeStruct((rows, QK_WIDTH), BF16)
    return pl.pallas_call(
        _qkv_kernel,
        grid=(rows // tm,),
        in_specs=[row_spec,
                  pl.BlockSpec(w.shape, lambda i: (0, 0), pipeline_mode=pl.Buffered(1)),
                  tab_spec, tab_spec],
        out_specs=[row_spec, row_spec, row_spec],
        out_shape=[out, out, out],
        compiler_params=pltpu.CompilerParams(
            dimension_semantics=("arbitrary",), vmem_limit_bytes=VMEM_LIMIT),
        name="qkv",
    )(x, w, cos, sin)


def _attn_kernel(lq1_ref, lk1_ref, lq2_ref, lk2_ref, g_ref, q_ref, k_ref, v_ref, o_ref,
                 qs_ref, m_ref, acc_ref, *, lambda_init):
    th = ATT_TH
    tq = 2 * th
    lane = lax.broadcasted_iota(jnp.int32, (1, LANES), 1)
    is_map1 = lane < DIFF_HEAD_DIM
    lam = (jnp.exp(jnp.sum(lq1_ref[...] * lk1_ref[...], axis=-1, keepdims=True))
           - jnp.exp(jnp.sum(lq2_ref[...] * lk2_ref[...], axis=-1, keepdims=True)) + lambda_init)
    zero = jnp.zeros((), BF16)

    def chunk_mask(n_rows):
        row = lax.broadcasted_iota(jnp.int32, (n_rows, th), 0) % th
        return row // CHUNK >= lax.broadcasted_iota(jnp.int32, (n_rows, th), 1) // CHUNK

    diag = chunk_mask(2 * th)
    first_cols = jnp.concatenate([diag, jnp.ones((2 * th, th), jnp.bool_)], axis=0)

    def update(h, rows, key_start, n_keys, visible):
        ks = pl.ds(key_start, n_keys)
        hs = slice(h * LANES, (h + 1) * LANES)
        s = lax.dot_general(qs_ref[h, rows, :], k_ref[0, ks, hs], (((1,), (1,)), ((), ())),
                            preferred_element_type=F32)
        if visible is not None:
            s = jnp.where(visible, s, NEG_INF)
        m_old = m_ref[h, rows, :]
        m_cur = jnp.max(s, axis=-1, keepdims=True)
        m_new = jnp.maximum(m_old, jnp.broadcast_to(m_cur, m_old.shape))
        alpha = jnp.exp2(m_old - m_new)
        e = jnp.exp2(s - jnp.concatenate([m_new] * (n_keys // LANES), axis=1)).astype(BF16)
        v_ext = jnp.concatenate([v_ref[0, ks, hs], jnp.ones((n_keys, LANES), BF16)], axis=1)
        acc_ref[h, rows, :] = (jnp.concatenate([alpha, alpha], axis=1) * acc_ref[h, rows, :]
                               + _dot(e, v_ext))
        m_ref[h, rows, :] = m_new

    all_rows = slice(0, 4 * th)
    second_half = slice(2 * th, 4 * th)

    def q_block(i, carry):
        q0 = pl.multiple_of(i * tq, tq)
        for h in range(ATT_HP):
            for half in range(2):
                q = q_ref[0, pl.ds(q0 + half * th, th), h * LANES:(h + 1) * LANES]
                qs_ref[h, (2 * half) * th:(2 * half + 1) * th, :] = jnp.where(is_map1, q, zero)
                qs_ref[h, (2 * half + 1) * th:(2 * half + 2) * th, :] = jnp.where(is_map1, zero, q)
        m_ref[...] = jnp.full(m_ref.shape, NEG_INF, F32)
        acc_ref[...] = jnp.zeros(acc_ref.shape, F32)

        def off_diag(j, c):
            for h in range(ATT_HP):
                update(h, all_rows, pl.multiple_of(j * tq, tq), tq, None)
            return c

        lax.fori_loop(0, i, off_diag, 0)
        for h in range(ATT_HP):
            update(h, all_rows, q0, th, first_cols)
            update(h, second_half, q0 + th, th, diag)

        for h in range(ATT_HP):
            for half in range(2):
                a1 = acc_ref[h, (2 * half) * th:(2 * half + 1) * th, :]
                a2 = acc_ref[h, (2 * half + 1) * th:(2 * half + 2) * th, :]
                o = (a1[:, :LANES] * (1.0 / a1[:, LANES:])
                     - a2[:, :LANES] * (lam / a2[:, LANES:]))
                o = o * lax.rsqrt(jnp.mean(o * o, axis=-1, keepdims=True) + LN_EPS) * g_ref[...]
                o = o * (1.0 - lambda_init)
                o_ref[0, pl.ds(q0 + half * th, th), h * LANES:(h + 1) * LANES] = o.astype(BF16)
        return carry

    lax.fori_loop(0, SEQ // tq, q_block, 0)


def _attn_call(lq1, lk1, lq2, lk2, g, q, k, v, lambda_init):
    const = lambda b, h: (0, 0)
    small = lambda a: pl.BlockSpec(a.shape, const)
    head_spec = pl.BlockSpec((1, SEQ, ATT_HP * LANES), lambda b, h: (b, 0, h))
    return pl.pallas_call(
        functools.partial(_attn_kernel, lambda_init=lambda_init),
        grid=(BATCH, DIFF_HEADS // ATT_HP),
        in_specs=[small(lq1), small(lk1), small(lq2), small(lk2), small(g),
                  head_spec, head_spec, head_spec],
        out_specs=head_spec,
        out_shape=jax.ShapeDtypeStruct((BATCH, SEQ, V_WIDTH), BF16),
        scratch_shapes=[pltpu.VMEM((ATT_HP, 4 * ATT_TH, LANES), BF16),
                        pltpu.VMEM((ATT_HP, 4 * ATT_TH, LANES), F32),
                        pltpu.VMEM((ATT_HP, 4 * ATT_TH, 2 * LANES), F32)],
        compiler_params=pltpu.CompilerParams(
            dimension_semantics=("arbitrary", "arbitrary"), vmem_limit_bytes=VMEM_LIMIT),
        name="diff_attn",
    )(lq1, lk1, lq2, lk2, g, q, k, v)


def _block_diag(w):
    same_block = np.eye(LRU_BLOCKS, dtype=np.float32)[:, None, :, None]
    return (w[:, :, None, :] * same_block).reshape(B_WIDTH, B_WIDTH)


def _rope_tables():
    half_dim = DIFF_HEAD_DIM // 2
    pos = np.arange(SEQ, dtype=np.float64)
    inv_freq = ROPE_THETA ** (-np.arange(0, DIFF_HEAD_DIM, 2, dtype=np.float64) / DIFF_HEAD_DIM)
    lane = np.arange(LANES)
    ang = pos[:, None] * inv_freq[lane % half_dim][None, :]
    sign = np.where(lane % DIFF_HEAD_DIM < half_dim, -1.0, 1.0)
    return (jnp.asarray(np.cos(ang), dtype=F32), jnp.asarray(np.sin(ang) * sign[None, :], dtype=F32))


def kernel(x, even_w_in, even_b_in, even_conv_w, even_conv_b, even_cnorm_g, even_cnorm_b,
           even_lru_conv_w, even_lru_conv_b, even_w_a, even_b_a, even_w_x, even_b_x,
           even_lru_lambda, even_w_out, odd_w_qkv, odd_lambda_q1, odd_lambda_k1,
           odd_lambda_q2, odd_lambda_k2, odd_subln_g, odd_w_out, mix_ln_g, mix_ln_b,
           ffn_w_gate, ffn_w_up, ffn_w_down, ffn_ln_g, ffn_ln_b):
    row = lambda a: a.reshape(1, -1)
    rows = BATCH * SEQ

    wg_all = ffn_w_gate.astype(BF16)
    wu_all = ffn_w_up.astype(BF16)
    wd_all = ffn_w_down.astype(BF16)

    def post(m, xres, w_out, layer):
        return _post_call(m.reshape(rows, -1), xres.reshape(rows, D_MODEL), w_out.astype(BF16),
                          row(mix_ln_g[layer]), row(mix_ln_b[layer]), wg_all, wu_all, wd_all,
                          row(ffn_ln_g[layer]), row(ffn_ln_b[layer]), layer)

    w_gate = jnp.concatenate([_block_diag(even_w_a[0]), _block_diag(even_w_x[0])], axis=1)
    b_gate = jnp.concatenate([even_b_a[0], even_b_x[0]]).reshape(1, -1)
    conv_w = jnp.broadcast_to(even_conv_w[0][:, None, :], (CONV_WIDTH, SUBLANES, A_WIDTH))
    m0 = _mixer_call(x, even_w_in[0].astype(BF16), row(even_b_in[0]), conv_w,
                     row(even_conv_b[0]), row(even_cnorm_g[0]), row(even_cnorm_b[0]),
                     even_lru_conv_w[0], row(even_lru_conv_b[0]), w_gate.astype(BF16), b_gate,
                     row(even_lru_lambda[0]))
    x1 = post(m0, x, even_w_out[0], 0)

    lambda_init = 0.8 - 0.6 * math.exp(-0.3 * 1)
    cos, sin = _rope_tables()
    q, k, v = _qkv_call(x1, odd_w_qkv[0].astype(BF16), cos, sin)
    shape3 = (BATCH, SEQ, QK_WIDTH)
    o = _attn_call(row(odd_lambda_q1[0]), row(odd_lambda_k1[0]), row(odd_lambda_q2[0]),
                   row(odd_lambda_k2[0]), row(odd_subln_g[0]),
                   q.reshape(shape3), k.reshape(shape3), v.reshape(shape3), lambda_init)
    out = post(o, x1, odd_w_out[0], 1)
    return out.reshape(BATCH, SEQ, D_MODEL)
```

```python
import functools
import math

import jax
import jax.numpy as jnp
import numpy as np
from jax import lax
from jax.experimental import pallas as pl
from jax.experimental.pallas import tpu as pltpu

F32 = jnp.float32
BF16 = jnp.bfloat16

D_MODEL = 1024
BATCH = 8
SEQ = 2048
DEPTH = 2
CHUNK = 64
A_WIDTH = 512
B_WIDTH = 512
CONV_WIDTH = 31
LRU_BLOCKS = 8
LRU_BLOCK_DIM = 64
LRU_CONV_WIDTH = 4
LRU_C = 8.0
IN_WIDTH = 2 * A_WIDTH + 2 * B_WIDTH
DIFF_HEADS = 8
DIFF_HEAD_DIM = 64
DIFF_V_DIM = 128
QK_WIDTH = 1024
V_WIDTH = 1024
ROPE_THETA = 10000.0
D_FF = 2816
LN_EPS = 1e-5
DN_ALPHA = (2 * DEPTH) ** 0.25
NEG_INF = -1e30
Q_SCALE = DIFF_HEAD_DIM ** -0.5 * math.log2(math.e)

LANES = 128
SUBLANES = 8
MXU_COLS = 256
VMEM_LIMIT = 56 * 1024 * 1024

MIX_TS = 512
CONV_HALO = 32
CONV_ROWS = 32
CONV_BLK = 64
LRU_HALO = 8
POST_TM = 512
FF_CHUNK = 256
QKV_TM = 512
ATT_HP = 8
ATT_TH = 256


def _ln(x, g, b):
    mu = jnp.mean(x, axis=-1, keepdims=True)
    xc = x - mu
    var = jnp.mean(xc * xc, axis=-1, keepdims=True)
    return xc * lax.rsqrt(var + LN_EPS) * g + b


def _dot(a, b):
    return jnp.dot(a, b, preferred_element_type=F32)


N_CAST = 4
CAST_STEPS = 32


def _cast_slabs(cast_in, cast_out):
    for src, dst in zip(cast_in, cast_out):
        dst[...] = src[...].astype(BF16)


def _cast_specs(weights, layers, step_of):
    in_specs, out_specs, out_shapes = [], [], []
    for w, layer in zip(weights, layers):
        _, rows, cols = w.shape
        slab = rows // CAST_STEPS
        per_slab = 1
        while (slab * per_slab) % 16:
            per_slab *= 2
        slab *= per_slab
        in_specs.append(pl.BlockSpec(
            (None, slab, cols), lambda *g, l=layer, p=per_slab: (l, step_of(*g) // p, 0)))
        out_specs.append(pl.BlockSpec(
            (slab, cols), lambda *g, p=per_slab: (step_of(*g) // p, 0)))
        out_shapes.append(jax.ShapeDtypeStruct((rows, cols), BF16))
    return in_specs, out_specs, out_shapes


def _mixer_kernel(x_ref, w_in_f32_ref, b_in_ref, conv_w_ref, conv_b_ref, cn_g_ref, cn_b_ref,
                  lconv_w_ref, lconv_b_ref, w_gate_ref, b_gate_ref, lam_ref, *rest):
    cast_in, rest = rest[:N_CAST], rest[N_CAST:]
    out_ref, rest = rest[0], rest[1:]
    cast_out, rest = rest[:N_CAST], rest[N_CAST:]
    (a_ext, a_sh, b_ext, h_carry, bgate_ref, gates_ref, au_ref, xb_ref, conv_ref,
     w_in_ref) = rest
    ts = MIX_TS
    s = pl.program_id(1)

    n_groups = A_WIDTH // LANES
    _cast_slabs(cast_in, cast_out)

    @pl.when((pl.program_id(0) == 0) & (s == 0))
    def _():
        w_in_ref[...] = w_in_f32_ref[...].astype(BF16)

    @pl.when(s == 0)
    def _():
        a_ext[:, 0:CONV_HALO, :] = jnp.zeros((n_groups, CONV_HALO, LANES), F32)
        a_ext[:, CONV_HALO + ts:, :] = jnp.zeros((n_groups, CONV_BLK, LANES), F32)
        b_ext[0:LRU_HALO, :] = jnp.zeros((LRU_HALO, B_WIDTH), F32)
        h_carry[...] = jnp.zeros_like(h_carry)

    xb_ref[...] = x_ref[0].astype(BF16)
    ha = _dot(xb_ref[...], w_in_ref[:, 0:2 * A_WIDTH]) + b_in_ref[:, 0:2 * A_WIDTH]
    glu = ha[:, 0:A_WIDTH] * jax.nn.sigmoid(ha[:, A_WIDTH:])
    for g in range(n_groups):
        a_ext[g, CONV_HALO:CONV_HALO + ts, :] = glu[:, g * LANES:(g + 1) * LANES]

    off = CONV_HALO - (CONV_WIDTH - 1)

    for g in range(n_groups):
        def shift_body(i, carry, g=g):
            base = pl.multiple_of(i * CONV_BLK, CONV_BLK)
            win = a_ext[g, pl.ds(base, CONV_BLK + SUBLANES), :]
            for r in range(1, SUBLANES):
                rolled = pltpu.roll(win, CONV_BLK + SUBLANES - r, axis=0)
                a_sh[r - 1, g, pl.ds(base, CONV_BLK), :] = rolled[0:CONV_BLK, :]
            return carry

        lax.fori_loop(0, (ts + CONV_BLK) // CONV_BLK, shift_body, 0)

    for g in range(n_groups):
        sl = slice(g * LANES, (g + 1) * LANES)
        taps = [conv_w_ref[j, :, sl] for j in range(CONV_WIDTH)]
        bias = conv_b_ref[:, sl]

        def conv_body(i, carry, g=g, sl=sl, taps=taps, bias=bias):
            base = pl.multiple_of(i * CONV_BLK, CONV_BLK)
            acc = jnp.broadcast_to(bias, (CONV_BLK, LANES))
            for j in range(CONV_WIDTH):
                q, r = divmod(off + j, SUBLANES)
                rows = pl.ds(base + q * SUBLANES, CONV_BLK)
                win = a_ext[g, rows, :] if r == 0 else a_sh[r - 1, g, rows, :]
                acc = acc + (win.reshape(-1, SUBLANES, LANES) * taps[j][None]).reshape(CONV_BLK, LANES)
            conv_ref[pl.ds(base, CONV_BLK), sl] = acc
            return carry

        lax.fori_loop(0, ts // CONV_BLK, conv_body, 0)
    a_ext[:, 0:CONV_HALO, :] = a_ext[:, ts:ts + CONV_HALO, :]

    def norm_body(i, carry):
        base = pl.multiple_of(i * CONV_ROWS, CONV_ROWS)
        ya = jax.nn.silu(_ln(conv_ref[pl.ds(base, CONV_ROWS), :], cn_g_ref[...], cn_b_ref[...]))
        out_ref[0, pl.ds(base, CONV_ROWS), 0:A_WIDTH] = ya.astype(BF16)
        return carry

    lax.fori_loop(0, ts // CONV_ROWS, norm_body, 0, unroll=True)

    hb = _dot(xb_ref[...], w_in_ref[:, 2 * A_WIDTH:]) + b_in_ref[:, 2 * A_WIDTH:]
    bgate_ref[...] = hb[:, 0:B_WIDTH]
    b_ext[LRU_HALO:LRU_HALO + ts, :] = hb[:, B_WIDTH:]
    loff = LRU_HALO - (LRU_CONV_WIDTH - 1)
    xc = jnp.broadcast_to(lconv_b_ref[...], (ts, B_WIDTH))
    for j in range(LRU_CONV_WIDTH):
        xc = xc + lconv_w_ref[j:j + 1, :] * b_ext[loff + j:loff + j + ts, :]
    b_ext[0:LRU_HALO, :] = b_ext[ts:ts + LRU_HALO, :]
    gates_ref[...] = _dot(xc.astype(BF16), w_gate_ref[...]) + b_gate_ref[...]
    b_ext[LRU_HALO:LRU_HALO + ts, :] = xc

    def scan_steps(a_cum, u_cum, axis, steps):
        idx = lax.broadcasted_iota(jnp.int32, a_cum.shape, axis)
        for step in steps:
            keep = idx >= step
            a_prev = jnp.where(keep, pltpu.roll(a_cum, step, axis=axis), 1.0)
            u_prev = jnp.where(keep, pltpu.roll(u_cum, step, axis=axis), 0.0)
            u_cum = a_cum * u_prev + u_cum
            a_cum = a_cum * a_prev
        return a_cum, u_cum

    nblk = ts // SUBLANES
    for g in range(B_WIDTH // LANES):
        sl = slice(g * LANES, (g + 1) * LANES)
        xg = b_ext[LRU_HALO:LRU_HALO + ts, sl]
        gate_r = jax.nn.sigmoid(gates_ref[:, sl])
        gate_i = jax.nn.sigmoid(gates_ref[:, B_WIDTH + g * LANES:B_WIDTH + (g + 1) * LANES])
        lam = lam_ref[:, sl]
        neg = -lam
        softplus = jnp.maximum(neg, 0.0) + jnp.log1p(jnp.exp(-jnp.abs(neg)))
        log_a = (-LRU_C * gate_r) * softplus
        th = jnp.tanh(log_a)
        y = -2.0 * th
        scale = jnp.where(y > 0.0, y * lax.rsqrt(y), 0.0) * lax.rsqrt(1.0 - th)
        a_blk, u_blk = scan_steps(jnp.exp(log_a).reshape(nblk, SUBLANES, LANES),
                                  (scale * (gate_i * xg)).reshape(nblk, SUBLANES, LANES),
                                  1, (1, 2, 4))
        au_ref[0] = a_blk.reshape(ts, LANES)
        au_ref[1] = u_blk.reshape(ts, LANES)
        last = pl.ds(SUBLANES - 1, nblk, stride=SUBLANES)
        a_end, u_end = scan_steps(au_ref[0, last, :], au_ref[1, last, :], 0,
                                  [1 << k for k in range(nblk.bit_length() - 1)])
        h_prev = h_carry[0:1, sl]
        h_end = a_end * h_prev + u_end
        h_carry[:, sl] = jnp.broadcast_to(h_end[nblk - 1:nblk, :], (SUBLANES, LANES))
        row0 = lax.broadcasted_iota(jnp.int32, (nblk, LANES), 0) == 0
        h_in = jnp.where(row0, h_prev, pltpu.roll(h_end, 1, axis=0))
        hg = (a_blk * h_in[:, None, :] + u_blk).reshape(ts, LANES)
        yb = hg * jax.nn.gelu(bgate_ref[:, sl])
        out_ref[0, :, A_WIDTH + g * LANES:A_WIDTH + (g + 1) * LANES] = yb.astype(BF16)


def _mixer_call(x, w_in, b_in, conv_w, conv_b, cn_g, cn_b, lconv_w, lconv_b, w_gate, b_gate, lam,
                cast_weights, cast_layers):
    ts = MIX_TS
    seq_steps = SEQ // ts
    assert BATCH * seq_steps == CAST_STEPS
    full = lambda a: pl.BlockSpec(a.shape, lambda b, s, nd=a.ndim: (0,) * nd,
                                  pipeline_mode=pl.Buffered(1))
    cast_in, cast_out, cast_shapes = _cast_specs(cast_weights, cast_layers,
                                                 lambda b, s: b * seq_steps + s)
    return pl.pallas_call(
        _mixer_kernel,
        grid=(BATCH, seq_steps),
        in_specs=[pl.BlockSpec((1, ts, D_MODEL), lambda b, s: (b, s, 0)),
                  full(w_in), full(b_in), full(conv_w), full(conv_b), full(cn_g), full(cn_b),
                  full(lconv_w), full(lconv_b), full(w_gate), full(b_gate), full(lam)] + cast_in,
        out_specs=[pl.BlockSpec((1, ts, A_WIDTH + B_WIDTH), lambda b, s: (b, s, 0))] + cast_out,
        out_shape=[jax.ShapeDtypeStruct((BATCH, SEQ, A_WIDTH + B_WIDTH), BF16)] + cast_shapes,
        scratch_shapes=[pltpu.VMEM((A_WIDTH // LANES, CONV_HALO + ts + CONV_BLK, LANES), F32),
                        pltpu.VMEM((SUBLANES - 1, A_WIDTH // LANES, ts + CONV_BLK, LANES), F32),
                        pltpu.VMEM((LRU_HALO + ts, B_WIDTH), F32),
                        pltpu.VMEM((SUBLANES, B_WIDTH), F32),
                        pltpu.VMEM((ts, B_WIDTH), F32),
                        pltpu.VMEM((ts, 2 * B_WIDTH), F32),
                        pltpu.VMEM((2, ts, LANES), F32),
                        pltpu.VMEM((ts, D_MODEL), BF16),
                        pltpu.VMEM((ts, A_WIDTH), F32),
                        pltpu.VMEM((D_MODEL, IN_WIDTH), BF16)],
        compiler_params=pltpu.CompilerParams(
            dimension_semantics=("arbitrary", "arbitrary"), vmem_limit_bytes=VMEM_LIMIT),
        name="mixer0",
    )(x, w_in, b_in, conv_w, conv_b, cn_g, cn_b, lconv_w, lconv_b, w_gate, b_gate, lam,
      *cast_weights)


def _post_kernel(m_ref, x_ref, w_out_ref, g1_ref, b1_ref, wg_ref, wu_ref, wd_ref, g2_ref, b2_ref,
                 out_ref, x1_ref, xb_ref, acc_ref):
    y = _dot(m_ref[...], w_out_ref[...])
    x1 = _ln(DN_ALPHA * x_ref[...] + y, g1_ref[...], b1_ref[...])
    x1_ref[...] = x1
    xb_ref[...] = x1.astype(BF16)
    for c in range(D_FF // FF_CHUNK):
        cs = slice(c * FF_CHUNK, (c + 1) * FF_CHUNK)
        gate = _dot(xb_ref[...], wg_ref[:, cs])
        up = _dot(xb_ref[...], wu_ref[:, cs])
        act = (jax.nn.silu(gate) * up).astype(BF16)
        contrib = _dot(act, wd_ref[cs, :])
        if c == 0:
            acc_ref[...] = contrib
        else:
            acc_ref[...] += contrib
    out_ref[...] = _ln(DN_ALPHA * x1_ref[...] + acc_ref[...], g2_ref[...], b2_ref[...])


def _post_call(m, x, w_out, g1, b1, wg, wu, wd, g2, b2):
    tm = POST_TM
    rows = m.shape[0]
    const = lambda i: (0, 0)
    full = lambda a: pl.BlockSpec(a.shape, const, pipeline_mode=pl.Buffered(1))
    return pl.pallas_call(
        _post_kernel,
        grid=(rows // tm,),
        in_specs=[pl.BlockSpec((tm, D_MODEL), lambda i: (i, 0)),
                  pl.BlockSpec((tm, D_MODEL), lambda i: (i, 0)),
                  full(w_out), full(g1), full(b1), full(wg), full(wu), full(wd),
                  full(g2), full(b2)],
        out_specs=pl.BlockSpec((tm, D_MODEL), lambda i: (i, 0)),
        out_shape=jax.ShapeDtypeStruct((rows, D_MODEL), F32),
        scratch_shapes=[pltpu.VMEM((tm, D_MODEL), F32),
                        pltpu.VMEM((tm, D_MODEL), BF16),
                        pltpu.VMEM((tm, D_MODEL), F32)],
        compiler_params=pltpu.CompilerParams(
            dimension_semantics=("arbitrary",), vmem_limit_bytes=VMEM_LIMIT),
        name="post",
    )(m, x, w_out, g1, b1, wg, wu, wd, g2, b2)


def _qkv_kernel(x_ref, w_f32_ref, cos_ref, sin_ref, *rest):
    cast_in, rest = rest[:N_CAST], rest[N_CAST:]
    q_ref, k_ref, v_ref = rest[:3]
    cast_out, (w_ref,) = rest[3:3 + N_CAST], rest[3 + N_CAST:]
    _cast_slabs(cast_in, cast_out)

    @pl.when(pl.program_id(0) == 0)
    def _():
        w_ref[...] = w_f32_ref[...].astype(BF16)

    xb = x_ref[...].astype(BF16)
    cos = cos_ref[...]
    sin = sin_ref[...]
    half_dim = DIFF_HEAD_DIM // 2
    lane = lax.broadcasted_iota(jnp.int32, (1, LANES), 1)
    first_half = (lane % DIFF_HEAD_DIM) < half_dim
    for g in range(2 * QK_WIDTH // MXU_COLS):
        t2 = _dot(xb, w_ref[:, g * MXU_COLS:(g + 1) * MXU_COLS])
        for half in range(MXU_COLS // LANES):
            t = t2[:, half * LANES:(half + 1) * LANES]
            rot = jnp.where(first_half, pltpu.roll(t, LANES - half_dim, axis=1),
                            pltpu.roll(t, half_dim, axis=1))
            r = t * cos + rot * sin
            col = g * MXU_COLS + half * LANES
            if col < QK_WIDTH:
                q_ref[:, col:col + LANES] = (r * Q_SCALE).astype(BF16)
            else:
                k_ref[:, col - QK_WIDTH:col - QK_WIDTH + LANES] = r.astype(BF16)
    v_ref[...] = _dot(xb, w_ref[:, 2 * QK_WIDTH:]).astype(BF16)


def _qkv_call(x, w, cos, sin, cast_weights, cast_layers):
    tm = QKV_TM
    rows = x.shape[0]
    assert rows // tm == CAST_STEPS
    pos_blocks = SEQ // tm
    row_spec = pl.BlockSpec((tm, D_MODEL), lambda i: (i, 0))
    tab_spec = pl.BlockSpec((tm, LANES), lambda i: (i % pos_blocks, 0))
    out = jax.ShapeDtypeStruct((rows, QK_WIDTH), BF16)
    cast_in, cast_out, cast_shapes = _cast_specs(cast_weights, cast_layers, lambda i: i)
    return pl.pallas_call(
        _qkv_kernel,
        grid=(rows // tm,),
        in_specs=[row_spec,
                  pl.BlockSpec(w.shape, lambda i: (0, 0), pipeline_mode=pl.Buffered(1)),
                  tab_spec, tab_spec] + cast_in,
        out_specs=[row_spec, row_spec, row_spec] + cast_out,
        out_shape=[out, out, out] + cast_shapes,
        scratch_shapes=[pltpu.VMEM(w.shape, BF16)],
        compiler_params=pltpu.CompilerParams(
            dimension_semantics=("arbitrary",), vmem_limit_bytes=VMEM_LIMIT),
        name="qkv",
    )(x, w, cos, sin, *cast_weights)


def _attn_kernel(lq1_ref, lk1_ref, lq2_ref, lk2_ref, g_ref, q_ref, k_ref, v_ref, o_ref,
                 qs_ref, m_ref, acc_ref, *, lambda_init):
    th = ATT_TH
    tq = 2 * th
    lane = lax.broadcasted_iota(jnp.int32, (1, LANES), 1)
    is_map1 = lane < DIFF_HEAD_DIM
    lam = (jnp.exp(jnp.sum(lq1_ref[...] * lk1_ref[...], axis=-1, keepdims=True))
           - jnp.exp(jnp.sum(lq2_ref[...] * lk2_ref[...], axis=-1, keepdims=True)) + lambda_init)
    zero = jnp.zeros((), BF16)

    def chunk_mask(n_rows):
        row = lax.broadcasted_iota(jnp.int32, (n_rows, th), 0) % th
        return row // CHUNK >= lax.broadcasted_iota(jnp.int32, (n_rows, th), 1) // CHUNK

    diag = chunk_mask(2 * th)
    first_cols = jnp.concatenate([diag, jnp.ones((2 * th, th), jnp.bool_)], axis=0)

    def update(h, rows, key_start, n_keys, visible):
        ks = pl.ds(key_start, n_keys)
        hs = slice(h * LANES, (h + 1) * LANES)
        s = lax.dot_general(qs_ref[h, rows, :], k_ref[0, ks, hs], (((1,), (1,)), ((), ())),
                            preferred_element_type=F32)
        if visible is not None:
            s = jnp.where(visible, s, NEG_INF)
        m_old = m_ref[h, rows, :]
        m_cur = jnp.max(s, axis=-1, keepdims=True)
        m_new = jnp.maximum(m_old, jnp.broadcast_to(m_cur, m_old.shape))
        alpha = jnp.exp2(m_old - m_new)
        e = jnp.exp2(s - jnp.concatenate([m_new] * (n_keys // LANES), axis=1)).astype(BF16)
        v_ext = jnp.concatenate([v_ref[0, ks, hs], jnp.ones((n_keys, LANES), BF16)], axis=1)
        acc_ref[h, rows, :] = (jnp.concatenate([alpha, alpha], axis=1) * acc_ref[h, rows, :]
                               + _dot(e, v_ext))
        m_ref[h, rows, :] = m_new

    all_rows = slice(0, 4 * th)
    second_half = slice(2 * th, 4 * th)

    def q_block(i, carry):
        q0 = pl.multiple_of(i * tq, tq)
        for h in range(ATT_HP):
            for half in range(2):
                q = q_ref[0, pl.ds(q0 + half * th, th), h * LANES:(h + 1) * LANES]
                qs_ref[h, (2 * half) * th:(2 * half + 1) * th, :] = jnp.where(is_map1, q, zero)
                qs_ref[h, (2 * half + 1) * th:(2 * half + 2) * th, :] = jnp.where(is_map1, zero, q)
        m_ref[...] = jnp.full(m_ref.shape, NEG_INF, F32)
        acc_ref[...] = jnp.zeros(acc_ref.shape, F32)

        def off_diag(j, c):
            for h in range(ATT_HP):
                update(h, all_rows, pl.multiple_of(j * tq, tq), tq, None)
            return c

        lax.fori_loop(0, i, off_diag, 0)
        for h in range(ATT_HP):
            update(h, all_rows, q0, th, first_cols)
            update(h, second_half, q0 + th, th, diag)

        for h in range(ATT_HP):
            for half in range(2):
                a1 = acc_ref[h, (2 * half) * th:(2 * half + 1) * th, :]
                a2 = acc_ref[h, (2 * half + 1) * th:(2 * half + 2) * th, :]
                o = (a1[:, :LANES] * (1.0 / a1[:, LANES:])
                     - a2[:, :LANES] * (lam / a2[:, LANES:]))
                o = o * lax.rsqrt(jnp.mean(o * o, axis=-1, keepdims=True) + LN_EPS) * g_ref[...]
                o = o * (1.0 - lambda_init)
                o_ref[0, pl.ds(q0 + half * th, th), h * LANES:(h + 1) * LANES] = o.astype(BF16)
        return carry

    lax.fori_loop(0, SEQ // tq, q_block, 0)


def _attn_call(lq1, lk1, lq2, lk2, g, q, k, v, lambda_init):
    const = lambda b, h: (0, 0)
    small = lambda a: pl.BlockSpec(a.shape, const)
    head_spec = pl.BlockSpec((1, SEQ, ATT_HP * LANES), lambda b, h: (b, 0, h))
    return pl.pallas_call(
        functools.partial(_attn_kernel, lambda_init=lambda_init),
        grid=(BATCH, DIFF_HEADS // ATT_HP),
        in_specs=[small(lq1), small(lk1), small(lq2), small(lk2), small(g),
                  head_spec, head_spec, head_spec],
        out_specs=head_spec,
        out_shape=jax.ShapeDtypeStruct((BATCH, SEQ, V_WIDTH), BF16),
        scratch_shapes=[pltpu.VMEM((ATT_HP, 4 * ATT_TH, LANES), BF16),
                        pltpu.VMEM((ATT_HP, 4 * ATT_TH, LANES), F32),
                        pltpu.VMEM((ATT_HP, 4 * ATT_TH, 2 * LANES), F32)],
        compiler_params=pltpu.CompilerParams(
            dimension_semantics=("arbitrary", "arbitrary"), vmem_limit_bytes=VMEM_LIMIT),
        name="diff_attn",
    )(lq1, lk1, lq2, lk2, g, q, k, v)


def _block_diag(w):
    same_block = np.eye(LRU_BLOCKS, dtype=np.float32)[:, None, :, None]
    return (w[:, :, None, :] * same_block).reshape(B_WIDTH, B_WIDTH)


def _rope_tables():
    half_dim = DIFF_HEAD_DIM // 2
    pos = np.arange(SEQ, dtype=np.float64)
    inv_freq = ROPE_THETA ** (-np.arange(0, DIFF_HEAD_DIM, 2, dtype=np.float64) / DIFF_HEAD_DIM)
    lane = np.arange(LANES)
    ang = pos[:, None] * inv_freq[lane % half_dim][None, :]
    sign = np.where(lane % DIFF_HEAD_DIM < half_dim, -1.0, 1.0)
    return (jnp.asarray(np.cos(ang), dtype=F32), jnp.asarray(np.sin(ang) * sign[None, :], dtype=F32))


def kernel(x, even_w_in, even_b_in, even_conv_w, even_conv_b, even_cnorm_g, even_cnorm_b,
           even_lru_conv_w, even_lru_conv_b, even_w_a, even_b_a, even_w_x, even_b_x,
           even_lru_lambda, even_w_out, odd_w_qkv, odd_lambda_q1, odd_lambda_k1,
           odd_lambda_q2, odd_lambda_k2, odd_subln_g, odd_w_out, mix_ln_g, mix_ln_b,
           ffn_w_gate, ffn_w_up, ffn_w_down, ffn_ln_g, ffn_ln_b):
    row = lambda a: a.reshape(1, -1)
    rows = BATCH * SEQ

    def post(m, xres, weights, layer):
        w_out, wg, wu, wd = weights
        return _post_call(m.reshape(rows, -1), xres.reshape(rows, D_MODEL), w_out,
                          row(mix_ln_g[layer]), row(mix_ln_b[layer]), wg, wu, wd,
                          row(ffn_ln_g[layer]), row(ffn_ln_b[layer]))

    w_gate = jnp.concatenate([_block_diag(even_w_a[0]), _block_diag(even_w_x[0])], axis=1)
    b_gate = jnp.concatenate([even_b_a[0], even_b_x[0]]).reshape(1, -1)
    conv_w = jnp.broadcast_to(even_conv_w[0][:, None, :], (CONV_WIDTH, SUBLANES, A_WIDTH))
    m0, *tail0 = _mixer_call(x, even_w_in[0], row(even_b_in[0]), conv_w,
                             row(even_conv_b[0]), row(even_cnorm_g[0]), row(even_cnorm_b[0]),
                             even_lru_conv_w[0], row(even_lru_conv_b[0]), w_gate.astype(BF16),
                             b_gate, row(even_lru_lambda[0]),
                             [even_w_out, ffn_w_gate, ffn_w_up, ffn_w_down], [0, 0, 0, 0])
    x1 = post(m0, x, tail0, 0)

    lambda_init = 0.8 - 0.6 * math.exp(-0.3 * 1)
    cos, sin = _rope_tables()
    q, k, v, *tail1 = _qkv_call(x1, odd_w_qkv[0], cos, sin,
                                [odd_w_out, ffn_w_gate, ffn_w_up, ffn_w_down], [0, 1, 1, 1])
    shape3 = (BATCH, SEQ, QK_WIDTH)
    o = _attn_call(row(odd_lambda_q1[0]), row(odd_lambda_k1[0]), row(odd_lambda_q2[0]),
                   row(odd_lambda_k2[0]), row(odd_subln_g[0]),
                   q.reshape(shape3), k.reshape(shape3), v.reshape(shape3), lambda_init)
    out = post(o, x1, tail1, 1)
    return out.reshape(BATCH, SEQ, D_MODEL)
```

```python
import functools
import math

import jax
import jax.numpy as jnp
import numpy as np
from jax import lax
from jax.experimental import pallas as pl
from jax.experimental.pallas import tpu as pltpu

F32 = jnp.float32
BF16 = jnp.bfloat16

D_MODEL = 1024
BATCH = 8
SEQ = 2048
DEPTH = 2
CHUNK = 64
A_WIDTH = 512
B_WIDTH = 512
CONV_WIDTH = 31
LRU_BLOCKS = 8
LRU_BLOCK_DIM = 64
LRU_CONV_WIDTH = 4
LRU_C = 8.0
IN_WIDTH = 2 * A_WIDTH + 2 * B_WIDTH
DIFF_HEADS = 8
DIFF_HEAD_DIM = 64
DIFF_V_DIM = 128
QK_WIDTH = 1024
V_WIDTH = 1024
ROPE_THETA = 10000.0
D_FF = 2816
LN_EPS = 1e-5
DN_ALPHA = (2 * DEPTH) ** 0.25
NEG_INF = -1e30
Q_SCALE = DIFF_HEAD_DIM ** -0.5 * math.log2(math.e)

LANES = 128
SUBLANES = 8
MXU_COLS = 256
VMEM_LIMIT = 56 * 1024 * 1024

MIX_TS = 512
CONV_HALO = 32
CONV_ROWS = 32
CONV_BLK = 64
LRU_HALO = 8
POST_TM = 512
FF_CHUNK = 256
QKV_TM = 512
ATT_HP = 8
ATT_TH = 256


def _ln(x, g, b):
    mu = jnp.mean(x, axis=-1, keepdims=True)
    xc = x - mu
    var = jnp.mean(xc * xc, axis=-1, keepdims=True)
    return xc * lax.rsqrt(var + LN_EPS) * g + b


def _dot(a, b):
    return jnp.dot(a, b, preferred_element_type=F32)


N_CAST = 4
CAST_STEPS = 32


def _cast_slabs(cast_in, cast_out):
    for src, dst in zip(cast_in, cast_out):
        dst[...] = src[...].astype(BF16)


def _cast_specs(weights, layers, step_of):
    in_specs, out_specs, out_shapes = [], [], []
    for w, layer in zip(weights, layers):
        _, rows, cols = w.shape
        slab = rows // CAST_STEPS
        per_slab = 1
        while (slab * per_slab) % 16:
            per_slab *= 2
        slab *= per_slab
        in_specs.append(pl.BlockSpec(
            (None, slab, cols), lambda *g, l=layer, p=per_slab: (l, step_of(*g) // p, 0)))
        out_specs.append(pl.BlockSpec(
            (slab, cols), lambda *g, p=per_slab: (step_of(*g) // p, 0)))
        out_shapes.append(jax.ShapeDtypeStruct((rows, cols), BF16))
    return in_specs, out_specs, out_shapes


def _mixer_kernel(x_ref, w_in_f32_ref, b_in_ref, conv_w_ref, conv_b_ref, cn_g_ref, cn_b_ref,
                  lconv_w_ref, lconv_b_ref, w_gate_ref, b_gate_ref, lam_ref, *rest):
    cast_in, rest = rest[:N_CAST], rest[N_CAST:]
    out_ref, rest = rest[0], rest[1:]
    cast_out, rest = rest[:N_CAST], rest[N_CAST:]
    (a_ext, a_sh, b_ext, h_carry, bgate_ref, gates_ref, au_ref, xb_ref, conv_ref,
     w_in_ref) = rest
    ts = MIX_TS
    s = pl.program_id(1)

    n_groups = A_WIDTH // LANES
    _cast_slabs(cast_in, cast_out)

    @pl.when((pl.program_id(0) == 0) & (s == 0))
    def _():
        w_in_ref[...] = w_in_f32_ref[...].astype(BF16)

    @pl.when(s == 0)
    def _():
        a_ext[:, 0:CONV_HALO, :] = jnp.zeros((n_groups, CONV_HALO, LANES), F32)
        a_ext[:, CONV_HALO + ts:, :] = jnp.zeros((n_groups, CONV_BLK, LANES), F32)
        b_ext[0:LRU_HALO, :] = jnp.zeros((LRU_HALO, B_WIDTH), F32)
        h_carry[...] = jnp.zeros_like(h_carry)

    xb_ref[...] = x_ref[0].astype(BF16)
    ha = _dot(xb_ref[...], w_in_ref[:, 0:2 * A_WIDTH]) + b_in_ref[:, 0:2 * A_WIDTH]
    glu = ha[:, 0:A_WIDTH] * jax.nn.sigmoid(ha[:, A_WIDTH:])
    for g in range(n_groups):
        a_ext[g, CONV_HALO:CONV_HALO + ts, :] = glu[:, g * LANES:(g + 1) * LANES]

    off = CONV_HALO - (CONV_WIDTH - 1)

    for g in range(n_groups):
        def shift_body(i, carry, g=g):
            base = pl.multiple_of(i * CONV_BLK, CONV_BLK)
            win = a_ext[g, pl.ds(base, CONV_BLK + SUBLANES), :]
            for r in range(1, SUBLANES):
                rolled = pltpu.roll(win, CONV_BLK + SUBLANES - r, axis=0)
                a_sh[r - 1, g, pl.ds(base, CONV_BLK), :] = rolled[0:CONV_BLK, :]
            return carry

        lax.fori_loop(0, (ts + CONV_BLK) // CONV_BLK, shift_body, 0)

    for g in range(n_groups):
        sl = slice(g * LANES, (g + 1) * LANES)
        taps = [conv_w_ref[j, :, sl] for j in range(CONV_WIDTH)]
        bias = conv_b_ref[:, sl]

        def conv_body(i, carry, g=g, sl=sl, taps=taps, bias=bias):
            base = pl.multiple_of(i * CONV_BLK, CONV_BLK)
            acc = jnp.broadcast_to(bias, (CONV_BLK, LANES))
            for j in range(CONV_WIDTH):
                q, r = divmod(off + j, SUBLANES)
                rows = pl.ds(base + q * SUBLANES, CONV_BLK)
                win = a_ext[g, rows, :] if r == 0 else a_sh[r - 1, g, rows, :]
                acc = acc + (win.reshape(-1, SUBLANES, LANES) * taps[j][None]).reshape(CONV_BLK, LANES)
            conv_ref[pl.ds(base, CONV_BLK), sl] = acc
            return carry

        lax.fori_loop(0, ts // CONV_BLK, conv_body, 0, unroll=2)
    a_ext[:, 0:CONV_HALO, :] = a_ext[:, ts:ts + CONV_HALO, :]

    def norm_body(i, carry):
        base = pl.multiple_of(i * CONV_ROWS, CONV_ROWS)
        ya = jax.nn.silu(_ln(conv_ref[pl.ds(base, CONV_ROWS), :], cn_g_ref[...], cn_b_ref[...]))
        out_ref[0, pl.ds(base, CONV_ROWS), 0:A_WIDTH] = ya.astype(BF16)
        return carry

    lax.fori_loop(0, ts // CONV_ROWS, norm_body, 0, unroll=True)

    hb = _dot(xb_ref[...], w_in_ref[:, 2 * A_WIDTH:]) + b_in_ref[:, 2 * A_WIDTH:]
    bgate_ref[...] = hb[:, 0:B_WIDTH]
    b_ext[LRU_HALO:LRU_HALO + ts, :] = hb[:, B_WIDTH:]
    loff = LRU_HALO - (LRU_CONV_WIDTH - 1)
    xc = jnp.broadcast_to(lconv_b_ref[...], (ts, B_WIDTH))
    for j in range(LRU_CONV_WIDTH):
        xc = xc + lconv_w_ref[j:j + 1, :] * b_ext[loff + j:loff + j + ts, :]
    b_ext[0:LRU_HALO, :] = b_ext[ts:ts + LRU_HALO, :]
    gates_ref[...] = _dot(xc.astype(BF16), w_gate_ref[...]) + b_gate_ref[...]
    b_ext[LRU_HALO:LRU_HALO + ts, :] = xc

    def scan_steps(a_cum, u_cum, axis, steps):
        idx = lax.broadcasted_iota(jnp.int32, a_cum.shape, axis)
        for step in steps:
            keep = idx >= step
            a_prev = jnp.where(keep, pltpu.roll(a_cum, step, axis=axis), 1.0)
            u_prev = jnp.where(keep, pltpu.roll(u_cum, step, axis=axis), 0.0)
            u_cum = a_cum * u_prev + u_cum
            a_cum = a_cum * a_prev
        return a_cum, u_cum

    nblk = ts // SUBLANES
    for g in range(B_WIDTH // LANES):
        sl = slice(g * LANES, (g + 1) * LANES)
        xg = b_ext[LRU_HALO:LRU_HALO + ts, sl]
        gate_r = jax.nn.sigmoid(gates_ref[:, sl])
        gate_i = jax.nn.sigmoid(gates_ref[:, B_WIDTH + g * LANES:B_WIDTH + (g + 1) * LANES])
        lam = lam_ref[:, sl]
        neg = -lam
        softplus = jnp.maximum(neg, 0.0) + jnp.log1p(jnp.exp(-jnp.abs(neg)))
        log_a = (-LRU_C * gate_r) * softplus
        th = jnp.tanh(log_a)
        y = -2.0 * th
        scale = jnp.where(y > 0.0, y * lax.rsqrt(y), 0.0) * lax.rsqrt(1.0 - th)
        a_blk, u_blk = scan_steps(jnp.exp(log_a).reshape(nblk, SUBLANES, LANES),
                                  (scale * (gate_i * xg)).reshape(nblk, SUBLANES, LANES),
                                  1, (1, 2, 4))
        au_ref[0] = a_blk.reshape(ts, LANES)
        au_ref[1] = u_blk.reshape(ts, LANES)
        last = pl.ds(SUBLANES - 1, nblk, stride=SUBLANES)
        a_end, u_end = scan_steps(au_ref[0, last, :], au_ref[1, last, :], 0,
                                  [1 << k for k in range(nblk.bit_length() - 1)])
        h_prev = h_carry[0:1, sl]
        h_end = a_end * h_prev + u_end
        h_carry[:, sl] = jnp.broadcast_to(h_end[nblk - 1:nblk, :], (SUBLANES, LANES))
        row0 = lax.broadcasted_iota(jnp.int32, (nblk, LANES), 0) == 0
        h_in = jnp.where(row0, h_prev, pltpu.roll(h_end, 1, axis=0))
        hg = (a_blk * h_in[:, None, :] + u_blk).reshape(ts, LANES)
        yb = hg * jax.nn.gelu(bgate_ref[:, sl])
        out_ref[0, :, A_WIDTH + g * LANES:A_WIDTH + (g + 1) * LANES] = yb.astype(BF16)


def _mixer_call(x, w_in, b_in, conv_w, conv_b, cn_g, cn_b, lconv_w, lconv_b, w_gate, b_gate, lam,
                cast_weights, cast_layers):
    ts = MIX_TS
    seq_steps = SEQ // ts
    assert BATCH * seq_steps == CAST_STEPS
    full = lambda a: pl.BlockSpec(a.shape, lambda b, s, nd=a.ndim: (0,) * nd,
                                  pipeline_mode=pl.Buffered(1))
    cast_in, cast_out, cast_shapes = _cast_specs(cast_weights, cast_layers,
                                                 lambda b, s: b * seq_steps + s)
    return pl.pallas_call(
        _mixer_kernel,
        grid=(BATCH, seq_steps),
        in_specs=[pl.BlockSpec((1, ts, D_MODEL), lambda b, s: (b, s, 0)),
                  full(w_in), full(b_in), full(conv_w), full(conv_b), full(cn_g), full(cn_b),
                  full(lconv_w), full(lconv_b), full(w_gate), full(b_gate), full(lam)] + cast_in,
        out_specs=[pl.BlockSpec((1, ts, A_WIDTH + B_WIDTH), lambda b, s: (b, s, 0))] + cast_out,
        out_shape=[jax.ShapeDtypeStruct((BATCH, SEQ, A_WIDTH + B_WIDTH), BF16)] + cast_shapes,
        scratch_shapes=[pltpu.VMEM((A_WIDTH // LANES, CONV_HALO + ts + CONV_BLK, LANES), F32),
                        pltpu.VMEM((SUBLANES - 1, A_WIDTH // LANES, ts + CONV_BLK, LANES), F32),
                        pltpu.VMEM((LRU_HALO + ts, B_WIDTH), F32),
                        pltpu.VMEM((SUBLANES, B_WIDTH), F32),
                        pltpu.VMEM((ts, B_WIDTH), F32),
                        pltpu.VMEM((ts, 2 * B_WIDTH), F32),
                        pltpu.VMEM((2, ts, LANES), F32),
                        pltpu.VMEM((ts, D_MODEL), BF16),
                        pltpu.VMEM((ts, A_WIDTH), F32),
                        pltpu.VMEM((D_MODEL, IN_WIDTH), BF16)],
        compiler_params=pltpu.CompilerParams(
            dimension_semantics=("arbitrary", "arbitrary"), vmem_limit_bytes=VMEM_LIMIT),
        name="mixer0",
    )(x, w_in, b_in, conv_w, conv_b, cn_g, cn_b, lconv_w, lconv_b, w_gate, b_gate, lam,
      *cast_weights)


def _post_kernel(m_ref, x_ref, w_out_ref, g1_ref, b1_ref, wg_ref, wu_ref, wd_ref, g2_ref, b2_ref,
                 out_ref, x1_ref, xb_ref, acc_ref):
    y = _dot(m_ref[...], w_out_ref[...])
    x1 = _ln(DN_ALPHA * x_ref[...] + y, g1_ref[...], b1_ref[...])
    x1_ref[...] = x1
    xb_ref[...] = x1.astype(BF16)
    for c in range(D_FF // FF_CHUNK):
        cs = slice(c * FF_CHUNK, (c + 1) * FF_CHUNK)
        gate = _dot(xb_ref[...], wg_ref[:, cs])
        up = _dot(xb_ref[...], wu_ref[:, cs])
        act = (jax.nn.silu(gate) * up).astype(BF16)
        contrib = _dot(act, wd_ref[cs, :])
        if c == 0:
            acc_ref[...] = contrib
        else:
            acc_ref[...] += contrib
    out_ref[...] = _ln(DN_ALPHA * x1_ref[...] + acc_ref[...], g2_ref[...], b2_ref[...])


def _post_call(m, x, w_out, g1, b1, wg, wu, wd, g2, b2):
    tm = POST_TM
    rows = m.shape[0]
    const = lambda i: (0, 0)
    full = lambda a: pl.BlockSpec(a.shape, const, pipeline_mode=pl.Buffered(1))
    return pl.pallas_call(
        _post_kernel,
        grid=(rows // tm,),
        in_specs=[pl.BlockSpec((tm, D_MODEL), lambda i: (i, 0)),
                  pl.BlockSpec((tm, D_MODEL), lambda i: (i, 0)),
                  full(w_out), full(g1), full(b1), full(wg), full(wu), full(wd),
                  full(g2), full(b2)],
        out_specs=pl.BlockSpec((tm, D_MODEL), lambda i: (i, 0)),
        out_shape=jax.ShapeDtypeStruct((rows, D_MODEL), F32),
        scratch_shapes=[pltpu.VMEM((tm, D_MODEL), F32),
                        pltpu.VMEM((tm, D_MODEL), BF16),
                        pltpu.VMEM((tm, D_MODEL), F32)],
        compiler_params=pltpu.CompilerParams(
            dimension_semantics=("arbitrary",), vmem_limit_bytes=VMEM_LIMIT),
        name="post",
    )(m, x, w_out, g1, b1, wg, wu, wd, g2, b2)


def _qkv_kernel(x_ref, w_f32_ref, cos_ref, sin_ref, *rest):
    cast_in, rest = rest[:N_CAST], rest[N_CAST:]
    q_ref, k_ref, v_ref = rest[:3]
    cast_out, (w_ref,) = rest[3:3 + N_CAST], rest[3 + N_CAST:]
    _cast_slabs(cast_in, cast_out)

    @pl.when(pl.program_id(0) == 0)
    def _():
        w_ref[...] = w_f32_ref[...].astype(BF16)

    xb = x_ref[...].astype(BF16)
    cos = cos_ref[...]
    sin = sin_ref[...]
    half_dim = DIFF_HEAD_DIM // 2
    lane = lax.broadcasted_iota(jnp.int32, (1, LANES), 1)
    first_half = (lane % DIFF_HEAD_DIM) < half_dim
    for g in range(2 * QK_WIDTH // MXU_COLS):
        t2 = _dot(xb, w_ref[:, g * MXU_COLS:(g + 1) * MXU_COLS])
        for half in range(MXU_COLS // LANES):
            t = t2[:, half * LANES:(half + 1) * LANES]
            rot = jnp.where(first_half, pltpu.roll(t, LANES - half_dim, axis=1),
                            pltpu.roll(t, half_dim, axis=1))
            r = t * cos + rot * sin
            col = g * MXU_COLS + half * LANES
            if col < QK_WIDTH:
                q_ref[:, col:col + LANES] = (r * Q_SCALE).astype(BF16)
            else:
                k_ref[:, col - QK_WIDTH:col - QK_WIDTH + LANES] = r.astype(BF16)
    v_ref[...] = _dot(xb, w_ref[:, 2 * QK_WIDTH:]).astype(BF16)


def _qkv_call(x, w, cos, sin, cast_weights, cast_layers):
    tm = QKV_TM
    rows = x.shape[0]
    assert rows // tm == CAST_STEPS
    pos_blocks = SEQ // tm
    row_spec = pl.BlockSpec((tm, D_MODEL), lambda i: (i, 0))
    tab_spec = pl.BlockSpec((tm, LANES), lambda i: (i % pos_blocks, 0))
    out = jax.ShapeDtypeStruct((rows, QK_WIDTH), BF16)
    cast_in, cast_out, cast_shapes = _cast_specs(cast_weights, cast_layers, lambda i: i)
    return pl.pallas_call(
        _qkv_kernel,
        grid=(rows // tm,),
        in_specs=[row_spec,
                  pl.BlockSpec(w.shape, lambda i: (0, 0), pipeline_mode=pl.Buffered(1)),
                  tab_spec, tab_spec] + cast_in,
        out_specs=[row_spec, row_spec, row_spec] + cast_out,
        out_shape=[out, out, out] + cast_shapes,
        scratch_shapes=[pltpu.VMEM(w.shape, BF16)],
        compiler_params=pltpu.CompilerParams(
            dimension_semantics=("arbitrary",), vmem_limit_bytes=VMEM_LIMIT),
        name="qkv",
    )(x, w, cos, sin, *cast_weights)


def _attn_kernel(lq1_ref, lk1_ref, lq2_ref, lk2_ref, g_ref, q_ref, k_ref, v_ref, o_ref,
                 qs_ref, m_ref, acc_ref, *, lambda_init):
    th = ATT_TH
    tq = 2 * th
    lane = lax.broadcasted_iota(jnp.int32, (1, LANES), 1)
    is_map1 = lane < DIFF_HEAD_DIM
    lam = (jnp.exp(jnp.sum(lq1_ref[...] * lk1_ref[...], axis=-1, keepdims=True))
           - jnp.exp(jnp.sum(lq2_ref[...] * lk2_ref[...], axis=-1, keepdims=True)) + lambda_init)
    gain = g_ref[...] * (1.0 - lambda_init)
    zero = jnp.zeros((), BF16)

    def chunk_mask(n_rows):
        row = lax.broadcasted_iota(jnp.int32, (n_rows, th), 0) % th
        return row // CHUNK >= lax.broadcasted_iota(jnp.int32, (n_rows, th), 1) // CHUNK

    diag = chunk_mask(2 * th)

    def update(h, rows, key_start, n_keys, visible):
        ks = pl.ds(key_start, n_keys)
        hs = slice(h * LANES, (h + 1) * LANES)
        s = lax.dot_general(qs_ref[h, rows, :], k_ref[0, ks, hs], (((1,), (1,)), ((), ())),
                            preferred_element_type=F32)
        if visible is not None:
            n = visible.shape[0]
            masked = jnp.where(visible, s[:n], NEG_INF)
            s = masked if n == s.shape[0] else jnp.concatenate([masked, s[n:]], axis=0)
        m_old = m_ref[h, rows, :]
        m_cur = jnp.max(s, axis=-1, keepdims=True)
        m_new = jnp.maximum(m_old, jnp.broadcast_to(m_cur, m_old.shape))
        alpha = jnp.exp2(m_old - m_new)
        e = jnp.exp2(s - jnp.concatenate([m_new] * (n_keys // LANES), axis=1)).astype(BF16)
        v_ext = jnp.concatenate([v_ref[0, ks, hs], jnp.ones((n_keys, LANES), BF16)], axis=1)
        acc_ref[h, rows, :] = (jnp.concatenate([alpha, alpha], axis=1) * acc_ref[h, rows, :]
                               + _dot(e, v_ext))
        m_ref[h, rows, :] = m_new

    all_rows = slice(0, 4 * th)
    second_half = slice(2 * th, 4 * th)

    def q_block(i, carry):
        q0 = pl.multiple_of(i * tq, tq)
        for h in range(ATT_HP):
            for half in range(2):
                q = q_ref[0, pl.ds(q0 + half * th, th), h * LANES:(h + 1) * LANES]
                qs_ref[h, (2 * half) * th:(2 * half + 1) * th, :] = jnp.where(is_map1, q, zero)
                qs_ref[h, (2 * half + 1) * th:(2 * half + 2) * th, :] = jnp.where(is_map1, zero, q)
        m_ref[...] = jnp.full(m_ref.shape, NEG_INF, F32)
        acc_ref[...] = jnp.zeros(acc_ref.shape, F32)

        def off_diag(j, c):
            for h in range(ATT_HP):
                update(h, all_rows, pl.multiple_of(j * tq, tq), tq, None)
            return c

        lax.fori_loop(0, i, off_diag, 0)
        for h in range(ATT_HP):
            update(h, all_rows, q0, th, diag)
            update(h, second_half, q0 + th, th, diag)

        for h in range(ATT_HP):
            for half in range(2):
                a1 = acc_ref[h, (2 * half) * th:(2 * half + 1) * th, :]
                a2 = acc_ref[h, (2 * half + 1) * th:(2 * half + 2) * th, :]
                o = (a1[:, :LANES] * (1.0 / a1[:, LANES:])
                     - a2[:, :LANES] * (lam * (1.0 / a2[:, LANES:])))
                o = o * lax.rsqrt(jnp.mean(o * o, axis=-1, keepdims=True) + LN_EPS) * gain
                o_ref[0, pl.ds(q0 + half * th, th), h * LANES:(h + 1) * LANES] = o.astype(BF16)
        return carry

    lax.fori_loop(0, SEQ // tq, q_block, 0)


def _attn_call(lq1, lk1, lq2, lk2, g, q, k, v, lambda_init):
    const = lambda b, h: (0, 0)
    small = lambda a: pl.BlockSpec(a.shape, const)
    head_spec = pl.BlockSpec((1, SEQ, ATT_HP * LANES), lambda b, h: (b, 0, h))
    return pl.pallas_call(
        functools.partial(_attn_kernel, lambda_init=lambda_init),
        grid=(BATCH, DIFF_HEADS // ATT_HP),
        in_specs=[small(lq1), small(lk1), small(lq2), small(lk2), small(g),
                  head_spec, head_spec, head_spec],
        out_specs=head_spec,
        out_shape=jax.ShapeDtypeStruct((BATCH, SEQ, V_WIDTH), BF16),
        scratch_shapes=[pltpu.VMEM((ATT_HP, 4 * ATT_TH, LANES), BF16),
                        pltpu.VMEM((ATT_HP, 4 * ATT_TH, LANES), F32),
                        pltpu.VMEM((ATT_HP, 4 * ATT_TH, 2 * LANES), F32)],
        compiler_params=pltpu.CompilerParams(
            dimension_semantics=("arbitrary", "arbitrary"), vmem_limit_bytes=VMEM_LIMIT),
        name="diff_attn",
    )(lq1, lk1, lq2, lk2, g, q, k, v)


def _block_diag(w):
    same_block = np.eye(LRU_BLOCKS, dtype=np.float32)[:, None, :, None]
    return (w[:, :, None, :] * same_block).reshape(B_WIDTH, B_WIDTH)


def _rope_tables():
    half_dim = DIFF_HEAD_DIM // 2
    pos = np.arange(SEQ, dtype=np.float64)
    inv_freq = ROPE_THETA ** (-np.arange(0, DIFF_HEAD_DIM, 2, dtype=np.float64) / DIFF_HEAD_DIM)
    lane = np.arange(LANES)
    ang = pos[:, None] * inv_freq[lane % half_dim][None, :]
    sign = np.where(lane % DIFF_HEAD_DIM < half_dim, -1.0, 1.0)
    return (jnp.asarray(np.cos(ang), dtype=F32), jnp.asarray(np.sin(ang) * sign[None, :], dtype=F32))


def kernel(x, even_w_in, even_b_in, even_conv_w, even_conv_b, even_cnorm_g, even_cnorm_b,
           even_lru_conv_w, even_lru_conv_b, even_w_a, even_b_a, even_w_x, even_b_x,
           even_lru_lambda, even_w_out, odd_w_qkv, odd_lambda_q1, odd_lambda_k1,
           odd_lambda_q2, odd_lambda_k2, odd_subln_g, odd_w_out, mix_ln_g, mix_ln_b,
           ffn_w_gate, ffn_w_up, ffn_w_down, ffn_ln_g, ffn_ln_b):
    row = lambda a: a.reshape(1, -1)
    rows = BATCH * SEQ

    def post(m, xres, weights, layer):
        w_out, wg, wu, wd = weights
        return _post_call(m.reshape(rows, -1), xres.reshape(rows, D_MODEL), w_out,
                          row(mix_ln_g[layer]), row(mix_ln_b[layer]), wg, wu, wd,
                          row(ffn_ln_g[layer]), row(ffn_ln_b[layer]))

    w_gate = jnp.concatenate([_block_diag(even_w_a[0]), _block_diag(even_w_x[0])], axis=1)
    b_gate = jnp.concatenate([even_b_a[0], even_b_x[0]]).reshape(1, -1)
    conv_w = jnp.broadcast_to(even_conv_w[0][:, None, :], (CONV_WIDTH, SUBLANES, A_WIDTH))
    m0, *tail0 = _mixer_call(x, even_w_in[0], row(even_b_in[0]), conv_w,
                             row(even_conv_b[0]), row(even_cnorm_g[0]), row(even_cnorm_b[0]),
                             even_lru_conv_w[0], row(even_lru_conv_b[0]), w_gate.astype(BF16),
                             b_gate, row(even_lru_lambda[0]),
                             [even_w_out, ffn_w_gate, ffn_w_up, ffn_w_down], [0, 0, 0, 0])
    x1 = post(m0, x, tail0, 0)

    lambda_init = 0.8 - 0.6 * math.exp(-0.3 * 1)
    cos, sin = _rope_tables()
    q, k, v, *tail1 = _qkv_call(x1, odd_w_qkv[0], cos, sin,
                                [odd_w_out, ffn_w_gate, ffn_w_up, ffn_w_down], [0, 1, 1, 1])
    shape3 = (BATCH, SEQ, QK_WIDTH)
    o = _attn_call(row(odd_lambda_q1[0]), row(odd_lambda_k1[0]), row(odd_lambda_q2[0]),
                   row(odd_lambda_k2[0]), row(odd_subln_g[0]),
                   q.reshape(shape3), k.reshape(shape3), v.reshape(shape3), lambda_init)
    out = post(o, x1, tail1, 1)
    return out.reshape(BATCH, SEQ, D_MODEL)
```

```python
import functools
import math

import jax
import jax.numpy as jnp
import numpy as np
from jax import lax
from jax.experimental import pallas as pl
from jax.experimental.pallas import tpu as pltpu

F32 = jnp.float32
BF16 = jnp.bfloat16

D_MODEL = 1024
BATCH = 8
SEQ = 2048
DEPTH = 2
CHUNK = 64
A_WIDTH = 512
B_WIDTH = 512
CONV_WIDTH = 31
LRU_BLOCKS = 8
LRU_CONV_WIDTH = 4
LRU_C = 8.0
IN_WIDTH = 2 * A_WIDTH + 2 * B_WIDTH
DIFF_HEADS = 8
DIFF_HEAD_DIM = 64
QK_WIDTH = 1024
V_WIDTH = 1024
ROPE_THETA = 10000.0
D_FF = 2816
LN_EPS = 1e-5
DN_ALPHA = (2 * DEPTH) ** 0.25
NEG_INF = -1e30
Q_SCALE = DIFF_HEAD_DIM ** -0.5 * math.log2(math.e)

LANES = 128
SUBLANES = 8
BF16_SUBLANES = 16
MXU_COLS = 256
VMEM_LIMIT = 56 * 1024 * 1024

MIX_TS = 512
CONV_HALO = 32
CONV_ROWS = 32
CONV_BLK = 64
LRU_HALO = 8
POST_TM = 1024
FF_CHUNK = 256
QKV_TM = 1024
ATT_HP = 8
ATT_TH = 256


def _ln(x, g, b):
    mu = jnp.mean(x, axis=-1, keepdims=True)
    xc = x - mu
    var = jnp.mean(xc * xc, axis=-1, keepdims=True)
    return xc * lax.rsqrt(var + LN_EPS) * g + b


def _dot(a, b):
    return jnp.dot(a, b, preferred_element_type=F32)


N_CAST = 4


def _cast_slabs(cast_in, cast_out):
    for src, dst in zip(cast_in, cast_out):
        dst[...] = src[...].astype(BF16)


def _cast_specs(weights, layers, step_of, n_steps):
    in_specs, out_specs, out_shapes = [], [], []
    for w, layer in zip(weights, layers):
        _, rows, cols = w.shape
        slab = rows // n_steps
        per_slab = 1
        while (slab * per_slab) % BF16_SUBLANES:
            per_slab *= 2
        slab *= per_slab
        in_specs.append(pl.BlockSpec(
            (None, slab, cols), lambda *g, l=layer, p=per_slab: (l, step_of(*g) // p, 0)))
        out_specs.append(pl.BlockSpec(
            (slab, cols), lambda *g, p=per_slab: (step_of(*g) // p, 0)))
        out_shapes.append(jax.ShapeDtypeStruct((rows, cols), BF16))
    return in_specs, out_specs, out_shapes


def _mixer_kernel(x_ref, w_in_f32_ref, b_in_ref, conv_w_ref, conv_b_ref, cn_g_ref, cn_b_ref,
                  lconv_w_ref, lconv_b_ref, w_gate_ref, b_gate_ref, lam_ref, *rest):
    cast_in, rest = rest[:N_CAST], rest[N_CAST:]
    out_ref, rest = rest[0], rest[1:]
    cast_out, rest = rest[:N_CAST], rest[N_CAST:]
    (a_ext, a_sh, b_ext, h_carry, bgate_ref, gates_ref, au_ref, xb_ref, conv_ref,
     w_in_ref) = rest
    ts = MIX_TS
    s = pl.program_id(1)

    n_groups = A_WIDTH // LANES
    _cast_slabs(cast_in, cast_out)

    @pl.when((pl.program_id(0) == 0) & (s == 0))
    def _():
        w_in_ref[...] = w_in_f32_ref[...].astype(BF16)

    @pl.when(s == 0)
    def _():
        a_ext[:, 0:CONV_HALO, :] = jnp.zeros((n_groups, CONV_HALO, LANES), F32)
        a_ext[:, CONV_HALO + ts:, :] = jnp.zeros((n_groups, CONV_BLK, LANES), F32)
        b_ext[0:LRU_HALO, :] = jnp.zeros((LRU_HALO, B_WIDTH), F32)
        h_carry[...] = jnp.zeros_like(h_carry)

    xb_ref[...] = x_ref[0].astype(BF16)
    ha = _dot(xb_ref[...], w_in_ref[:, 0:2 * A_WIDTH]) + b_in_ref[:, 0:2 * A_WIDTH]
    glu = ha[:, 0:A_WIDTH] * jax.nn.sigmoid(ha[:, A_WIDTH:])
    for g in range(n_groups):
        a_ext[g, CONV_HALO:CONV_HALO + ts, :] = glu[:, g * LANES:(g + 1) * LANES]

    off = CONV_HALO - (CONV_WIDTH - 1)

    for g in range(n_groups):
        def shift_body(i, carry, g=g):
            base = pl.multiple_of(i * CONV_BLK, CONV_BLK)
            win = a_ext[g, pl.ds(base, CONV_BLK + SUBLANES), :]
            for r in range(1, SUBLANES):
                rolled = pltpu.roll(win, CONV_BLK + SUBLANES - r, axis=0)
                a_sh[r - 1, g, pl.ds(base, CONV_BLK), :] = rolled[0:CONV_BLK, :]
            return carry

        lax.fori_loop(0, (ts + CONV_BLK) // CONV_BLK, shift_body, 0)

    for g in range(n_groups):
        sl = slice(g * LANES, (g + 1) * LANES)
        taps = [conv_w_ref[j, :, sl] for j in range(CONV_WIDTH)]
        bias = conv_b_ref[:, sl]

        def conv_body(i, carry, g=g, sl=sl, taps=taps, bias=bias):
            base = pl.multiple_of(i * CONV_BLK, CONV_BLK)
            acc = jnp.broadcast_to(bias, (CONV_BLK, LANES))
            for j in range(CONV_WIDTH):
                q, r = divmod(off + j, SUBLANES)
                rows = pl.ds(base + q * SUBLANES, CONV_BLK)
                win = a_ext[g, rows, :] if r == 0 else a_sh[r - 1, g, rows, :]
                acc = acc + (win.reshape(-1, SUBLANES, LANES) * taps[j][None]).reshape(CONV_BLK, LANES)
            conv_ref[pl.ds(base, CONV_BLK), sl] = acc
            return carry

        lax.fori_loop(0, ts // CONV_BLK, conv_body, 0, unroll=2)
    a_ext[:, 0:CONV_HALO, :] = a_ext[:, ts:ts + CONV_HALO, :]

    def norm_body(i, carry):
        base = pl.multiple_of(i * CONV_ROWS, CONV_ROWS)
        ya = jax.nn.silu(_ln(conv_ref[pl.ds(base, CONV_ROWS), :], cn_g_ref[...], cn_b_ref[...]))
        out_ref[0, pl.ds(base, CONV_ROWS), 0:A_WIDTH] = ya.astype(BF16)
        return carry

    lax.fori_loop(0, ts // CONV_ROWS, norm_body, 0, unroll=True)

    hb = _dot(xb_ref[...], w_in_ref[:, 2 * A_WIDTH:]) + b_in_ref[:, 2 * A_WIDTH:]
    bgate_ref[...] = hb[:, 0:B_WIDTH]
    b_ext[LRU_HALO:LRU_HALO + ts, :] = hb[:, B_WIDTH:]
    loff = LRU_HALO - (LRU_CONV_WIDTH - 1)
    xc = jnp.broadcast_to(lconv_b_ref[...], (ts, B_WIDTH))
    for j in range(LRU_CONV_WIDTH):
        xc = xc + lconv_w_ref[j:j + 1, :] * b_ext[loff + j:loff + j + ts, :]
    b_ext[0:LRU_HALO, :] = b_ext[ts:ts + LRU_HALO, :]
    gates_ref[...] = _dot(xc.astype(BF16), w_gate_ref[...]) + b_gate_ref[...]
    b_ext[LRU_HALO:LRU_HALO + ts, :] = xc

    def scan_steps(a_cum, u_cum, axis, steps):
        idx = lax.broadcasted_iota(jnp.int32, a_cum.shape, axis)
        for step in steps:
            keep = idx >= step
            a_prev = jnp.where(keep, pltpu.roll(a_cum, step, axis=axis), 1.0)
            u_prev = jnp.where(keep, pltpu.roll(u_cum, step, axis=axis), 0.0)
            u_cum = a_cum * u_prev + u_cum
            a_cum = a_cum * a_prev
        return a_cum, u_cum

    nblk = ts // SUBLANES
    for g in range(B_WIDTH // LANES):
        sl = slice(g * LANES, (g + 1) * LANES)
        xg = b_ext[LRU_HALO:LRU_HALO + ts, sl]
        gate_r = jax.nn.sigmoid(gates_ref[:, sl])
        gate_i = jax.nn.sigmoid(gates_ref[:, B_WIDTH + g * LANES:B_WIDTH + (g + 1) * LANES])
        lam = lam_ref[:, sl]
        neg = -lam
        softplus = jnp.maximum(neg, 0.0) + jnp.log1p(jnp.exp(-jnp.abs(neg)))
        log_a = (-LRU_C * gate_r) * softplus
        th = jnp.tanh(log_a)
        y = -2.0 * th
        scale = jnp.where(y > 0.0, y * lax.rsqrt(y), 0.0) * lax.rsqrt(1.0 - th)
        a_blk, u_blk = scan_steps(jnp.exp(log_a).reshape(nblk, SUBLANES, LANES),
                                  (scale * (gate_i * xg)).reshape(nblk, SUBLANES, LANES),
                                  1, (1, 2, 4))
        au_ref[0] = a_blk.reshape(ts, LANES)
        au_ref[1] = u_blk.reshape(ts, LANES)
        last = pl.ds(SUBLANES - 1, nblk, stride=SUBLANES)
        a_end, u_end = scan_steps(au_ref[0, last, :], au_ref[1, last, :], 0,
                                  [1 << k for k in range(nblk.bit_length() - 1)])
        h_prev = h_carry[0:1, sl]
        h_end = a_end * h_prev + u_end
        h_carry[:, sl] = jnp.broadcast_to(h_end[nblk - 1:nblk, :], (SUBLANES, LANES))
        row0 = lax.broadcasted_iota(jnp.int32, (nblk, LANES), 0) == 0
        h_in = jnp.where(row0, h_prev, pltpu.roll(h_end, 1, axis=0))
        hg = (a_blk * h_in[:, None, :] + u_blk).reshape(ts, LANES)
        yb = hg * jax.nn.gelu(bgate_ref[:, sl])
        out_ref[0, :, A_WIDTH + g * LANES:A_WIDTH + (g + 1) * LANES] = yb.astype(BF16)


def _mixer_call(x, w_in, b_in, conv_w, conv_b, cn_g, cn_b, lconv_w, lconv_b, w_gate, b_gate, lam,
                cast_weights, cast_layers):
    ts = MIX_TS
    seq_steps = SEQ // ts
    full = lambda a: pl.BlockSpec(a.shape, lambda b, s, nd=a.ndim: (0,) * nd,
                                  pipeline_mode=pl.Buffered(1))
    cast_in, cast_out, cast_shapes = _cast_specs(
        cast_weights, cast_layers, lambda b, s: b * seq_steps + s, BATCH * seq_steps)
    return pl.pallas_call(
        _mixer_kernel,
        grid=(BATCH, seq_steps),
        in_specs=[pl.BlockSpec((1, ts, D_MODEL), lambda b, s: (b, s, 0)),
                  full(w_in), full(b_in), full(conv_w), full(conv_b), full(cn_g), full(cn_b),
                  full(lconv_w), full(lconv_b), full(w_gate), full(b_gate), full(lam)] + cast_in,
        out_specs=[pl.BlockSpec((1, ts, A_WIDTH + B_WIDTH), lambda b, s: (b, s, 0))] + cast_out,
        out_shape=[jax.ShapeDtypeStruct((BATCH, SEQ, A_WIDTH + B_WIDTH), BF16)] + cast_shapes,
        scratch_shapes=[pltpu.VMEM((A_WIDTH // LANES, CONV_HALO + ts + CONV_BLK, LANES), F32),
                        pltpu.VMEM((SUBLANES - 1, A_WIDTH // LANES, ts + CONV_BLK, LANES), F32),
                        pltpu.VMEM((LRU_HALO + ts, B_WIDTH), F32),
                        pltpu.VMEM((SUBLANES, B_WIDTH), F32),
                        pltpu.VMEM((ts, B_WIDTH), F32),
                        pltpu.VMEM((ts, 2 * B_WIDTH), F32),
                        pltpu.VMEM((2, ts, LANES), F32),
                        pltpu.VMEM((ts, D_MODEL), BF16),
                        pltpu.VMEM((ts, A_WIDTH), F32),
                        pltpu.VMEM((D_MODEL, IN_WIDTH), BF16)],
        compiler_params=pltpu.CompilerParams(
            dimension_semantics=("arbitrary", "arbitrary"), vmem_limit_bytes=VMEM_LIMIT),
        name="mixer0",
    )(x, w_in, b_in, conv_w, conv_b, cn_g, cn_b, lconv_w, lconv_b, w_gate, b_gate, lam,
      *cast_weights)


def _post_kernel(m_ref, x_ref, w_out_ref, g1_ref, b1_ref, wg_ref, wu_ref, wd_ref, g2_ref, b2_ref,
                 out_ref, x1_ref, xb_ref, acc_ref):
    y = _dot(m_ref[...], w_out_ref[...])
    x1 = _ln(DN_ALPHA * x_ref[...] + y, g1_ref[...], b1_ref[...])
    x1_ref[...] = x1
    xb_ref[...] = x1.astype(BF16)
    for c in range(D_FF // FF_CHUNK):
        cs = slice(c * FF_CHUNK, (c + 1) * FF_CHUNK)
        gate = _dot(xb_ref[...], wg_ref[:, cs])
        up = _dot(xb_ref[...], wu_ref[:, cs])
        act = (jax.nn.silu(gate) * up).astype(BF16)
        contrib = _dot(act, wd_ref[cs, :])
        if c == 0:
            acc_ref[...] = contrib
        else:
            acc_ref[...] += contrib
    out_ref[...] = _ln(DN_ALPHA * x1_ref[...] + acc_ref[...], g2_ref[...], b2_ref[...])


def _post_call(m, x, w_out, g1, b1, wg, wu, wd, g2, b2):
    tm = POST_TM
    rows = m.shape[0]
    const = lambda i: (0, 0)
    full = lambda a: pl.BlockSpec(a.shape, const, pipeline_mode=pl.Buffered(1))
    return pl.pallas_call(
        _post_kernel,
        grid=(rows // tm,),
        in_specs=[pl.BlockSpec((tm, D_MODEL), lambda i: (i, 0)),
                  pl.BlockSpec((tm, D_MODEL), lambda i: (i, 0)),
                  full(w_out), full(g1), full(b1), full(wg), full(wu), full(wd),
                  full(g2), full(b2)],
        out_specs=pl.BlockSpec((tm, D_MODEL), lambda i: (i, 0)),
        out_shape=jax.ShapeDtypeStruct((rows, D_MODEL), F32),
        scratch_shapes=[pltpu.VMEM((tm, D_MODEL), F32),
                        pltpu.VMEM((tm, D_MODEL), BF16),
                        pltpu.VMEM((tm, D_MODEL), F32)],
        compiler_params=pltpu.CompilerParams(
            dimension_semantics=("arbitrary",), vmem_limit_bytes=VMEM_LIMIT),
        name="post",
    )(m, x, w_out, g1, b1, wg, wu, wd, g2, b2)


def _qkv_kernel(x_ref, w_f32_ref, cos_ref, sin_ref, *rest):
    cast_in, rest = rest[:N_CAST], rest[N_CAST:]
    q_ref, k_ref, v_ref = rest[:3]
    cast_out, (w_ref,) = rest[3:3 + N_CAST], rest[3 + N_CAST:]
    _cast_slabs(cast_in, cast_out)

    @pl.when(pl.program_id(0) == 0)
    def _():
        w_ref[...] = w_f32_ref[...].astype(BF16)

    xb = x_ref[...].astype(BF16)
    cos = cos_ref[...]
    sin = sin_ref[...]
    half_dim = DIFF_HEAD_DIM // 2
    lane = lax.broadcasted_iota(jnp.int32, (1, LANES), 1)
    first_half = (lane % DIFF_HEAD_DIM) < half_dim
    for g in range(2 * QK_WIDTH // MXU_COLS):
        t2 = _dot(xb, w_ref[:, g * MXU_COLS:(g + 1) * MXU_COLS])
        for half in range(MXU_COLS // LANES):
            t = t2[:, half * LANES:(half + 1) * LANES]
            rot = jnp.where(first_half, pltpu.roll(t, LANES - half_dim, axis=1),
                            pltpu.roll(t, half_dim, axis=1))
            r = t * cos + rot * sin
            col = g * MXU_COLS + half * LANES
            if col < QK_WIDTH:
                q_ref[:, col:col + LANES] = (r * Q_SCALE).astype(BF16)
            else:
                k_ref[:, col - QK_WIDTH:col - QK_WIDTH + LANES] = r.astype(BF16)
    v_ref[...] = _dot(xb, w_ref[:, 2 * QK_WIDTH:]).astype(BF16)


def _qkv_call(x, w, cos, sin, cast_weights, cast_layers):
    tm = QKV_TM
    rows = x.shape[0]
    pos_blocks = SEQ // tm
    row_spec = pl.BlockSpec((tm, D_MODEL), lambda i: (i, 0))
    tab_spec = pl.BlockSpec((tm, LANES), lambda i: (i % pos_blocks, 0))
    out = jax.ShapeDtypeStruct((rows, QK_WIDTH), BF16)
    cast_in, cast_out, cast_shapes = _cast_specs(cast_weights, cast_layers, lambda i: i,
                                                 rows // tm)
    return pl.pallas_call(
        _qkv_kernel,
        grid=(rows // tm,),
        in_specs=[row_spec,
                  pl.BlockSpec(w.shape, lambda i: (0, 0), pipeline_mode=pl.Buffered(1)),
                  tab_spec, tab_spec] + cast_in,
        out_specs=[row_spec, row_spec, row_spec] + cast_out,
        out_shape=[out, out, out] + cast_shapes,
        scratch_shapes=[pltpu.VMEM(w.shape, BF16)],
        compiler_params=pltpu.CompilerParams(
            dimension_semantics=("arbitrary",), vmem_limit_bytes=VMEM_LIMIT),
        name="qkv",
    )(x, w, cos, sin, *cast_weights)


def _attn_kernel(lq1_ref, lk1_ref, lq2_ref, lk2_ref, g_ref, q_ref, k_ref, v_ref, o_ref,
                 qs_ref, m_ref, acc_ref, *, lambda_init):
    th = ATT_TH
    tq = 2 * th
    lane = lax.broadcasted_iota(jnp.int32, (1, LANES), 1)
    is_map1 = lane < DIFF_HEAD_DIM
    lam = (jnp.exp(jnp.sum(lq1_ref[...] * lk1_ref[...], axis=-1, keepdims=True))
           - jnp.exp(jnp.sum(lq2_ref[...] * lk2_ref[...], axis=-1, keepdims=True)) + lambda_init)
    gain = g_ref[...] * (1.0 - lambda_init)
    zero = jnp.zeros((), BF16)

    def chunk_mask(n_rows):
        row = lax.broadcasted_iota(jnp.int32, (n_rows, th), 0) % th
        return row // CHUNK >= lax.broadcasted_iota(jnp.int32, (n_rows, th), 1) // CHUNK

    diag = chunk_mask(2 * th)

    def update(h, rows, key_start, n_keys, visible):
        ks = pl.ds(key_start, n_keys)
        hs = slice(h * LANES, (h + 1) * LANES)
        s = lax.dot_general(qs_ref[h, rows, :], k_ref[0, ks, hs], (((1,), (1,)), ((), ())),
                            preferred_element_type=F32)
        if visible is not None:
            n = visible.shape[0]
            masked = jnp.where(visible, s[:n], NEG_INF)
            s = masked if n == s.shape[0] else jnp.concatenate([masked, s[n:]], axis=0)
        m_old = m_ref[h, rows, :]
        m_cur = jnp.max(s, axis=-1, keepdims=True)
        m_new = jnp.maximum(m_old, jnp.broadcast_to(m_cur, m_old.shape))
        alpha = jnp.exp2(m_old - m_new)
        e = jnp.exp2(s - jnp.concatenate([m_new] * (n_keys // LANES), axis=1)).astype(BF16)
        v_ext = jnp.concatenate([v_ref[0, ks, hs], jnp.ones((n_keys, LANES), BF16)], axis=1)
        acc_ref[h, rows, :] = (jnp.concatenate([alpha, alpha], axis=1) * acc_ref[h, rows, :]
                               + _dot(e, v_ext))
        m_ref[h, rows, :] = m_new

    all_rows = slice(0, 4 * th)
    second_half = slice(2 * th, 4 * th)

    def q_block(i, carry):
        q0 = pl.multiple_of(i * tq, tq)
        for h in range(ATT_HP):
            for half in range(2):
                q = q_ref[0, pl.ds(q0 + half * th, th), h * LANES:(h + 1) * LANES]
                qs_ref[h, (2 * half) * th:(2 * half + 1) * th, :] = jnp.where(is_map1, q, zero)
                qs_ref[h, (2 * half + 1) * th:(2 * half + 2) * th, :] = jnp.where(is_map1, zero, q)
        m_ref[...] = jnp.full(m_ref.shape, NEG_INF, F32)
        acc_ref[...] = jnp.zeros(acc_ref.shape, F32)

        def off_diag(j, c):
            for h in range(ATT_HP):
                update(h, all_rows, pl.multiple_of(j * tq, tq), tq, None)
            return c

        lax.fori_loop(0, i, off_diag, 0)
        for h in range(ATT_HP):
            update(h, all_rows, q0, th, diag)
            update(h, second_half, q0 + th, th, diag)

        for h in range(ATT_HP):
            for half in range(2):
                a1 = acc_ref[h, (2 * half) * th:(2 * half + 1) * th, :]
                a2 = acc_ref[h, (2 * half + 1) * th:(2 * half + 2) * th, :]
                o = (a1[:, :LANES] * (1.0 / a1[:, LANES:])
                     - a2[:, :LANES] * (lam * (1.0 / a2[:, LANES:])))
                o = o * lax.rsqrt(jnp.mean(o * o, axis=-1, keepdims=True) + LN_EPS) * gain
                o_ref[0, pl.ds(q0 + half * th, th), h * LANES:(h + 1) * LANES] = o.astype(BF16)
        return carry

    lax.fori_loop(0, SEQ // tq, q_block, 0)


def _attn_call(lq1, lk1, lq2, lk2, g, q, k, v, lambda_init):
    const = lambda b, h: (0, 0)
    small = lambda a: pl.BlockSpec(a.shape, const)
    head_spec = pl.BlockSpec((1, SEQ, ATT_HP * LANES), lambda b, h: (b, 0, h))
    return pl.pallas_call(
        functools.partial(_attn_kernel, lambda_init=lambda_init),
        grid=(BATCH, DIFF_HEADS // ATT_HP),
        in_specs=[small(lq1), small(lk1), small(lq2), small(lk2), small(g),
                  head_spec, head_spec, head_spec],
        out_specs=head_spec,
        out_shape=jax.ShapeDtypeStruct((BATCH, SEQ, V_WIDTH), BF16),
        scratch_shapes=[pltpu.VMEM((ATT_HP, 4 * ATT_TH, LANES), BF16),
                        pltpu.VMEM((ATT_HP, 4 * ATT_TH, LANES), F32),
                        pltpu.VMEM((ATT_HP, 4 * ATT_TH, 2 * LANES), F32)],
        compiler_params=pltpu.CompilerParams(
            dimension_semantics=("arbitrary", "arbitrary"), vmem_limit_bytes=VMEM_LIMIT),
        name="diff_attn",
    )(lq1, lk1, lq2, lk2, g, q, k, v)


def _block_diag(w):
    same_block = np.eye(LRU_BLOCKS, dtype=np.float32)[:, None, :, None]
    return (w[:, :, None, :] * same_block).reshape(B_WIDTH, B_WIDTH)


def _rope_tables():
    half_dim = DIFF_HEAD_DIM // 2
    pos = np.arange(SEQ, dtype=np.float64)
    inv_freq = ROPE_THETA ** (-np.arange(0, DIFF_HEAD_DIM, 2, dtype=np.float64) / DIFF_HEAD_DIM)
    lane = np.arange(LANES)
    ang = pos[:, None] * inv_freq[lane % half_dim][None, :]
    sign = np.where(lane % DIFF_HEAD_DIM < half_dim, -1.0, 1.0)
    return (jnp.asarray(np.cos(ang), dtype=F32), jnp.asarray(np.sin(ang) * sign[None, :], dtype=F32))


def kernel(x, even_w_in, even_b_in, even_conv_w, even_conv_b, even_cnorm_g, even_cnorm_b,
           even_lru_conv_w, even_lru_conv_b, even_w_a, even_b_a, even_w_x, even_b_x,
           even_lru_lambda, even_w_out, odd_w_qkv, odd_lambda_q1, odd_lambda_k1,
           odd_lambda_q2, odd_lambda_k2, odd_subln_g, odd_w_out, mix_ln_g, mix_ln_b,
           ffn_w_gate, ffn_w_up, ffn_w_down, ffn_ln_g, ffn_ln_b):
    row = lambda a: a.reshape(1, -1)
    rows = BATCH * SEQ

    def post(m, xres, weights, layer):
        w_out, wg, wu, wd = weights
        return _post_call(m.reshape(rows, -1), xres.reshape(rows, D_MODEL), w_out,
                          row(mix_ln_g[layer]), row(mix_ln_b[layer]), wg, wu, wd,
                          row(ffn_ln_g[layer]), row(ffn_ln_b[layer]))

    w_gate = jnp.concatenate([_block_diag(even_w_a[0]), _block_diag(even_w_x[0])], axis=1)
    b_gate = jnp.concatenate([even_b_a[0], even_b_x[0]]).reshape(1, -1)
    conv_w = jnp.broadcast_to(even_conv_w[0][:, None, :], (CONV_WIDTH, SUBLANES, A_WIDTH))
    m0, *tail0 = _mixer_call(x, even_w_in[0], row(even_b_in[0]), conv_w,
                             row(even_conv_b[0]), row(even_cnorm_g[0]), row(even_cnorm_b[0]),
                             even_lru_conv_w[0], row(even_lru_conv_b[0]), w_gate.astype(BF16),
                             b_gate, row(even_lru_lambda[0]),
                             [even_w_out, ffn_w_gate, ffn_w_up, ffn_w_down], [0, 0, 0, 0])
    x1 = post(m0, x, tail0, 0)

    lambda_init = 0.8 - 0.6 * math.exp(-0.3 * 1)
    cos, sin = _rope_tables()
    q, k, v, *tail1 = _qkv_call(x1, odd_w_qkv[0], cos, sin,
                                [odd_w_out, ffn_w_gate, ffn_w_up, ffn_w_down], [0, 1, 1, 1])
    shape3 = (BATCH, SEQ, QK_WIDTH)
    o = _attn_call(row(odd_lambda_q1[0]), row(odd_lambda_k1[0]), row(odd_lambda_q2[0]),
                   row(odd_lambda_k2[0]), row(odd_subln_g[0]),
                   q.reshape(shape3), k.reshape(shape3), v.reshape(shape3), lambda_init)
    out = post(o, x1, tail1, 1)
    return out.reshape(BATCH, SEQ, D_MODEL)
```

```python
import functools
import math

import jax
import jax.numpy as jnp
import numpy as np
from jax import lax
from jax.experimental import pallas as pl
from jax.experimental.pallas import tpu as pltpu

F32 = jnp.float32
BF16 = jnp.bfloat16

D_MODEL = 1024
BATCH = 8
SEQ = 2048
DEPTH = 2
CHUNK = 64
A_WIDTH = 512
B_WIDTH = 512
CONV_WIDTH = 31
LRU_BLOCKS = 8
LRU_CONV_WIDTH = 4
LRU_C = 8.0
IN_WIDTH = 2 * A_WIDTH + 2 * B_WIDTH
DIFF_HEADS = 8
DIFF_HEAD_DIM = 64
QK_WIDTH = 1024
V_WIDTH = 1024
ROPE_THETA = 10000.0
D_FF = 2816
LN_EPS = 1e-5
DN_ALPHA = (2 * DEPTH) ** 0.25
NEG_INF = -1e30
Q_SCALE = DIFF_HEAD_DIM ** -0.5 * math.log2(math.e)

LANES = 128
SUBLANES = 8
BF16_SUBLANES = 16
MXU_COLS = 256
VMEM_LIMIT = 56 * 1024 * 1024

MIX_TS = 512
CONV_HALO = 32
CONV_ROWS = 32
CONV_BLK = 64
LRU_HALO = 8
POST_TM = 1024
FF_CHUNK = 256
QKV_TM = 1024
ATT_HP = 8
ATT_TH = 256


def _ln(x, g, b):
    mu = jnp.mean(x, axis=-1, keepdims=True)
    xc = x - mu
    var = jnp.mean(xc * xc, axis=-1, keepdims=True)
    return xc * lax.rsqrt(var + LN_EPS) * g + b


def _dot(a, b):
    return jnp.dot(a, b, preferred_element_type=F32)


N_CAST = 4


def _cast_slabs(cast_in, cast_out):
    for src, dst in zip(cast_in, cast_out):
        dst[...] = src[...].astype(BF16)


def _cast_specs(weights, layers, step_of, n_steps):
    in_specs, out_specs, out_shapes = [], [], []
    for w, layer in zip(weights, layers):
        _, rows, cols = w.shape
        slab = rows // n_steps
        per_slab = 1
        while (slab * per_slab) % BF16_SUBLANES:
            per_slab *= 2
        slab *= per_slab
        in_specs.append(pl.BlockSpec(
            (None, slab, cols), lambda *g, l=layer, p=per_slab: (l, step_of(*g) // p, 0)))
        out_specs.append(pl.BlockSpec(
            (slab, cols), lambda *g, p=per_slab: (step_of(*g) // p, 0)))
        out_shapes.append(jax.ShapeDtypeStruct((rows, cols), BF16))
    return in_specs, out_specs, out_shapes


def _mixer_kernel(x_ref, w_in_f32_ref, b_in_ref, conv_w_ref, conv_b_ref, cn_g_ref, cn_b_ref,
                  lconv_w_ref, lconv_b_ref, w_gate_ref, b_gate_ref, lam_ref, *rest):
    cast_in, rest = rest[:N_CAST], rest[N_CAST:]
    out_ref, rest = rest[0], rest[1:]
    cast_out, rest = rest[:N_CAST], rest[N_CAST:]
    (a_ext, a_sh, b_ext, h_carry, bgate_ref, gates_ref, au_ref, xb_ref, conv_ref,
     w_in_ref) = rest
    ts = MIX_TS
    s = pl.program_id(1)

    n_groups = A_WIDTH // LANES
    _cast_slabs(cast_in, cast_out)

    @pl.when((pl.program_id(0) == 0) & (s == 0))
    def _():
        w_in_ref[...] = w_in_f32_ref[...].astype(BF16)

    @pl.when(s == 0)
    def _():
        a_ext[:, 0:CONV_HALO, :] = jnp.zeros((n_groups, CONV_HALO, LANES), F32)
        a_ext[:, CONV_HALO + ts:, :] = jnp.zeros((n_groups, CONV_BLK, LANES), F32)
        b_ext[0:LRU_HALO, :] = jnp.zeros((LRU_HALO, B_WIDTH), F32)
        h_carry[...] = jnp.zeros_like(h_carry)

    xb_ref[...] = x_ref[0].astype(BF16)
    ha = _dot(xb_ref[...], w_in_ref[:, 0:2 * A_WIDTH]) + b_in_ref[:, 0:2 * A_WIDTH]
    glu = ha[:, 0:A_WIDTH] * jax.nn.sigmoid(ha[:, A_WIDTH:])
    for g in range(n_groups):
        a_ext[g, CONV_HALO:CONV_HALO + ts, :] = glu[:, g * LANES:(g + 1) * LANES]

    off = CONV_HALO - (CONV_WIDTH - 1)

    for g in range(n_groups):
        def shift_body(i, carry, g=g):
            base = pl.multiple_of(i * CONV_BLK, CONV_BLK)
            win = a_ext[g, pl.ds(base, CONV_BLK + SUBLANES), :]
            for r in range(1, SUBLANES):
                rolled = pltpu.roll(win, CONV_BLK + SUBLANES - r, axis=0)
                a_sh[r - 1, g, pl.ds(base, CONV_BLK), :] = rolled[0:CONV_BLK, :]
            return carry

        lax.fori_loop(0, (ts + CONV_BLK) // CONV_BLK, shift_body, 0)

    for g in range(n_groups):
        sl = slice(g * LANES, (g + 1) * LANES)
        taps = [conv_w_ref[j, :, sl] for j in range(CONV_WIDTH)]
        bias = conv_b_ref[:, sl]

        def conv_body(i, carry, g=g, sl=sl, taps=taps, bias=bias):
            base = pl.multiple_of(i * CONV_BLK, CONV_BLK)
            acc = jnp.broadcast_to(bias, (CONV_BLK, LANES))
            for j in range(CONV_WIDTH):
                q, r = divmod(off + j, SUBLANES)
                rows = pl.ds(base + q * SUBLANES, CONV_BLK)
                win = a_ext[g, rows, :] if r == 0 else a_sh[r - 1, g, rows, :]
                acc = acc + (win.reshape(-1, SUBLANES, LANES) * taps[j][None]).reshape(CONV_BLK, LANES)
            conv_ref[pl.ds(base, CONV_BLK), sl] = acc
            return carry

        lax.fori_loop(0, ts // CONV_BLK, conv_body, 0, unroll=2)
    a_ext[:, 0:CONV_HALO, :] = a_ext[:, ts:ts + CONV_HALO, :]

    def norm_body(i, carry):
        base = pl.multiple_of(i * CONV_ROWS, CONV_ROWS)
        ya = jax.nn.silu(_ln(conv_ref[pl.ds(base, CONV_ROWS), :], cn_g_ref[...], cn_b_ref[...]))
        out_ref[0, pl.ds(base, CONV_ROWS), 0:A_WIDTH] = ya.astype(BF16)
        return carry

    lax.fori_loop(0, ts // CONV_ROWS, norm_body, 0, unroll=True)

    hb = _dot(xb_ref[...], w_in_ref[:, 2 * A_WIDTH:]) + b_in_ref[:, 2 * A_WIDTH:]
    bgate_ref[...] = hb[:, 0:B_WIDTH]
    b_ext[LRU_HALO:LRU_HALO + ts, :] = hb[:, B_WIDTH:]
    loff = LRU_HALO - (LRU_CONV_WIDTH - 1)
    xc = jnp.broadcast_to(lconv_b_ref[...], (ts, B_WIDTH))
    for j in range(LRU_CONV_WIDTH):
        xc = xc + lconv_w_ref[j:j + 1, :] * b_ext[loff + j:loff + j + ts, :]
    b_ext[0:LRU_HALO, :] = b_ext[ts:ts + LRU_HALO, :]
    gates_ref[...] = _dot(xc.astype(BF16), w_gate_ref[...]) + b_gate_ref[...]
    b_ext[LRU_HALO:LRU_HALO + ts, :] = xc

    def scan_steps(a_cum, u_cum, axis, steps):
        idx = lax.broadcasted_iota(jnp.int32, a_cum.shape, axis)
        for step in steps:
            keep = idx >= step
            a_prev = jnp.where(keep, pltpu.roll(a_cum, step, axis=axis), 1.0)
            u_prev = jnp.where(keep, pltpu.roll(u_cum, step, axis=axis), 0.0)
            u_cum = a_cum * u_prev + u_cum
            a_cum = a_cum * a_prev
        return a_cum, u_cum

    nblk = ts // SUBLANES
    for g in range(B_WIDTH // LANES):
        sl = slice(g * LANES, (g + 1) * LANES)
        xg = b_ext[LRU_HALO:LRU_HALO + ts, sl]
        gate_r = jax.nn.sigmoid(gates_ref[:, sl])
        gate_i = jax.nn.sigmoid(gates_ref[:, B_WIDTH + g * LANES:B_WIDTH + (g + 1) * LANES])
        lam = lam_ref[:, sl]
        neg = -lam
        softplus = jnp.maximum(neg, 0.0) + jnp.log1p(jnp.exp(-jnp.abs(neg)))
        log_a = (-LRU_C * gate_r) * softplus
        th = jnp.tanh(log_a)
        y = -2.0 * th
        scale = jnp.where(y > 0.0, y * lax.rsqrt(y), 0.0) * lax.rsqrt(1.0 - th)
        a_blk, u_blk = scan_steps(jnp.exp(log_a).reshape(nblk, SUBLANES, LANES),
                                  (scale * (gate_i * xg)).reshape(nblk, SUBLANES, LANES),
                                  1, (1, 2, 4))
        au_ref[0] = a_blk.reshape(ts, LANES)
        au_ref[1] = u_blk.reshape(ts, LANES)
        last = pl.ds(SUBLANES - 1, nblk, stride=SUBLANES)
        a_end, u_end = scan_steps(au_ref[0, last, :], au_ref[1, last, :], 0,
                                  [1 << k for k in range(nblk.bit_length() - 1)])
        h_prev = h_carry[0:1, sl]
        h_end = a_end * h_prev + u_end
        h_carry[:, sl] = jnp.broadcast_to(h_end[nblk - 1:nblk, :], (SUBLANES, LANES))
        row0 = lax.broadcasted_iota(jnp.int32, (nblk, LANES), 0) == 0
        h_in = jnp.where(row0, h_prev, pltpu.roll(h_end, 1, axis=0))
        hg = (a_blk * h_in[:, None, :] + u_blk).reshape(ts, LANES)
        yb = hg * jax.nn.gelu(bgate_ref[:, sl])
        out_ref[0, :, A_WIDTH + g * LANES:A_WIDTH + (g + 1) * LANES] = yb.astype(BF16)


def _mixer_call(x, w_in, b_in, conv_w, conv_b, cn_g, cn_b, lconv_w, lconv_b, w_gate, b_gate, lam,
                cast_weights, cast_layers):
    ts = MIX_TS
    seq_steps = SEQ // ts
    full = lambda a: pl.BlockSpec(a.shape, lambda b, s, nd=a.ndim: (0,) * nd,
                                  pipeline_mode=pl.Buffered(1))
    cast_in, cast_out, cast_shapes = _cast_specs(
        cast_weights, cast_layers, lambda b, s: b * seq_steps + s, BATCH * seq_steps)
    return pl.pallas_call(
        _mixer_kernel,
        grid=(BATCH, seq_steps),
        in_specs=[pl.BlockSpec((1, ts, D_MODEL), lambda b, s: (b, s, 0)),
                  full(w_in), full(b_in), full(conv_w), full(conv_b), full(cn_g), full(cn_b),
                  full(lconv_w), full(lconv_b), full(w_gate), full(b_gate), full(lam)] + cast_in,
        out_specs=[pl.BlockSpec((1, ts, A_WIDTH + B_WIDTH), lambda b, s: (b, s, 0))] + cast_out,
        out_shape=[jax.ShapeDtypeStruct((BATCH, SEQ, A_WIDTH + B_WIDTH), BF16)] + cast_shapes,
        scratch_shapes=[pltpu.VMEM((A_WIDTH // LANES, CONV_HALO + ts + CONV_BLK, LANES), F32),
                        pltpu.VMEM((SUBLANES - 1, A_WIDTH // LANES, ts + CONV_BLK, LANES), F32),
                        pltpu.VMEM((LRU_HALO + ts, B_WIDTH), F32),
                        pltpu.VMEM((SUBLANES, B_WIDTH), F32),
                        pltpu.VMEM((ts, B_WIDTH), F32),
                        pltpu.VMEM((ts, 2 * B_WIDTH), F32),
                        pltpu.VMEM((2, ts, LANES), F32),
                        pltpu.VMEM((ts, D_MODEL), BF16),
                        pltpu.VMEM((ts, A_WIDTH), F32),
                        pltpu.VMEM((D_MODEL, IN_WIDTH), BF16)],
        compiler_params=pltpu.CompilerParams(
            dimension_semantics=("arbitrary", "arbitrary"), vmem_limit_bytes=VMEM_LIMIT),
        name="mixer0",
    )(x, w_in, b_in, conv_w, conv_b, cn_g, cn_b, lconv_w, lconv_b, w_gate, b_gate, lam,
      *cast_weights)


def _post_kernel(m_ref, x_ref, w_out_ref, g1_ref, b1_ref, wg_ref, wu_ref, wd_ref, g2_ref, b2_ref,
                 out_ref, x1_ref, xb_ref, acc_ref):
    y = _dot(m_ref[...], w_out_ref[...])
    x1 = _ln(DN_ALPHA * x_ref[...] + y, g1_ref[...], b1_ref[...])
    x1_ref[...] = x1
    xb_ref[...] = x1.astype(BF16)
    for c in range(D_FF // FF_CHUNK):
        cs = slice(c * FF_CHUNK, (c + 1) * FF_CHUNK)
        gate = _dot(xb_ref[...], wg_ref[:, cs])
        up = _dot(xb_ref[...], wu_ref[:, cs])
        act = (jax.nn.silu(gate) * up).astype(BF16)
        contrib = _dot(act, wd_ref[cs, :])
        if c == 0:
            acc_ref[...] = contrib
        else:
            acc_ref[...] += contrib
    out_ref[...] = _ln(DN_ALPHA * x1_ref[...] + acc_ref[...], g2_ref[...], b2_ref[...])


def _post_call(m, x, w_out, g1, b1, wg, wu, wd, g2, b2):
    tm = POST_TM
    rows = m.shape[0]
    const = lambda i: (0, 0)
    full = lambda a: pl.BlockSpec(a.shape, const, pipeline_mode=pl.Buffered(1))
    return pl.pallas_call(
        _post_kernel,
        grid=(rows // tm,),
        in_specs=[pl.BlockSpec((tm, D_MODEL), lambda i: (i, 0)),
                  pl.BlockSpec((tm, D_MODEL), lambda i: (i, 0)),
                  full(w_out), full(g1), full(b1), full(wg), full(wu), full(wd),
                  full(g2), full(b2)],
        out_specs=pl.BlockSpec((tm, D_MODEL), lambda i: (i, 0)),
        out_shape=jax.ShapeDtypeStruct((rows, D_MODEL), F32),
        scratch_shapes=[pltpu.VMEM((tm, D_MODEL), F32),
                        pltpu.VMEM((tm, D_MODEL), BF16),
                        pltpu.VMEM((tm, D_MODEL), F32)],
        compiler_params=pltpu.CompilerParams(
            dimension_semantics=("arbitrary",), vmem_limit_bytes=VMEM_LIMIT),
        name="post",
    )(m, x, w_out, g1, b1, wg, wu, wd, g2, b2)


def _qkv_kernel(x_ref, w_f32_ref, cos_ref, sin_ref, *rest):
    cast_in, rest = rest[:N_CAST], rest[N_CAST:]
    q_ref, k_ref, v_ref = rest[:3]
    cast_out, (w_ref,) = rest[3:3 + N_CAST], rest[3 + N_CAST:]
    _cast_slabs(cast_in, cast_out)

    @pl.when(pl.program_id(0) == 0)
    def _():
        w_ref[...] = w_f32_ref[...].astype(BF16)

    xb = x_ref[...].astype(BF16)
    cos = cos_ref[...]
    sin = sin_ref[...]
    half_dim = DIFF_HEAD_DIM // 2
    lane = lax.broadcasted_iota(jnp.int32, (1, LANES), 1)
    first_half = (lane % DIFF_HEAD_DIM) < half_dim
    for g in range(2 * QK_WIDTH // MXU_COLS):
        t2 = _dot(xb, w_ref[:, g * MXU_COLS:(g + 1) * MXU_COLS])
        for half in range(MXU_COLS // LANES):
            t = t2[:, half * LANES:(half + 1) * LANES]
            rot = jnp.where(first_half, pltpu.roll(t, LANES - half_dim, axis=1),
                            pltpu.roll(t, half_dim, axis=1))
            r = t * cos + rot * sin
            col = g * MXU_COLS + half * LANES
            if col < QK_WIDTH:
                q_ref[:, col:col + LANES] = (r * Q_SCALE).astype(BF16)
            else:
                k_ref[:, col - QK_WIDTH:col - QK_WIDTH + LANES] = r.astype(BF16)
    v_ref[...] = _dot(xb, w_ref[:, 2 * QK_WIDTH:]).astype(BF16)


def _qkv_call(x, w, cos, sin, cast_weights, cast_layers):
    tm = QKV_TM
    rows = x.shape[0]
    pos_blocks = SEQ // tm
    row_spec = pl.BlockSpec((tm, D_MODEL), lambda i: (i, 0))
    tab_spec = pl.BlockSpec((tm, LANES), lambda i: (i % pos_blocks, 0))
    out = jax.ShapeDtypeStruct((rows, QK_WIDTH), BF16)
    cast_in, cast_out, cast_shapes = _cast_specs(cast_weights, cast_layers, lambda i: i,
                                                 rows // tm)
    return pl.pallas_call(
        _qkv_kernel,
        grid=(rows // tm,),
        in_specs=[row_spec,
                  pl.BlockSpec(w.shape, lambda i: (0, 0), pipeline_mode=pl.Buffered(1)),
                  tab_spec, tab_spec] + cast_in,
        out_specs=[row_spec, row_spec, row_spec] + cast_out,
        out_shape=[out, out, out] + cast_shapes,
        scratch_shapes=[pltpu.VMEM(w.shape, BF16)],
        compiler_params=pltpu.CompilerParams(
            dimension_semantics=("arbitrary",), vmem_limit_bytes=VMEM_LIMIT),
        name="qkv",
    )(x, w, cos, sin, *cast_weights)


def _attn_kernel(lq1_ref, lk1_ref, lq2_ref, lk2_ref, g_ref, q_ref, k_ref, v_ref, o_ref,
                 qs_ref, m_ref, acc_ref, *, lambda_init):
    th = ATT_TH
    tq = 2 * th
    lane = lax.broadcasted_iota(jnp.int32, (1, LANES), 1)
    is_map1 = lane < DIFF_HEAD_DIM
    lam = (jnp.exp(jnp.sum(lq1_ref[...] * lk1_ref[...], axis=-1, keepdims=True))
           - jnp.exp(jnp.sum(lq2_ref[...] * lk2_ref[...], axis=-1, keepdims=True)) + lambda_init)
    gain = g_ref[...] * (1.0 - lambda_init)
    zero = jnp.zeros((), BF16)

    def chunk_mask(n_rows):
        row = lax.broadcasted_iota(jnp.int32, (n_rows, th), 0) % th
        return row // CHUNK >= lax.broadcasted_iota(jnp.int32, (n_rows, th), 1) // CHUNK

    diag = chunk_mask(2 * th)

    def update(h, rows, key_start, n_keys, visible):
        ks = pl.ds(key_start, n_keys)
        hs = slice(h * LANES, (h + 1) * LANES)
        s = lax.dot_general(qs_ref[h, rows, :], k_ref[0, ks, hs], (((1,), (1,)), ((), ())),
                            preferred_element_type=F32)
        if visible is not None:
            n = visible.shape[0]
            masked = jnp.where(visible, s[:n], NEG_INF)
            s = masked if n == s.shape[0] else jnp.concatenate([masked, s[n:]], axis=0)
        m_old = m_ref[h, rows, :]
        m_cur = jnp.max(s, axis=-1, keepdims=True)
        m_new = jnp.maximum(m_old, jnp.broadcast_to(m_cur, m_old.shape))
        alpha = jnp.exp2(m_old - m_new)
        e = jnp.exp2(s - jnp.concatenate([m_new] * (n_keys // LANES), axis=1)).astype(BF16)
        v_ext = jnp.concatenate([v_ref[0, ks, hs], jnp.ones((n_keys, LANES), BF16)], axis=1)
        acc_ref[h, rows, :] = (jnp.concatenate([alpha, alpha], axis=1) * acc_ref[h, rows, :]
                               + _dot(e, v_ext))
        m_ref[h, rows, :] = m_new

    all_rows = slice(0, 4 * th)
    second_half = slice(2 * th, 4 * th)

    def q_block(i, carry):
        q0 = pl.multiple_of(i * tq, tq)
        for h in range(ATT_HP):
            for half in range(2):
                q = q_ref[0, pl.ds(q0 + half * th, th), h * LANES:(h + 1) * LANES]
                qs_ref[h, (2 * half) * th:(2 * half + 1) * th, :] = jnp.where(is_map1, q, zero)
                qs_ref[h, (2 * half + 1) * th:(2 * half + 2) * th, :] = jnp.where(is_map1, zero, q)
        m_ref[...] = jnp.full(m_ref.shape, NEG_INF, F32)
        acc_ref[...] = jnp.zeros(acc_ref.shape, F32)

        def off_diag(j):
            for h in range(ATT_HP):
                update(h, all_rows, pl.multiple_of(j * tq, tq), tq, None)

        def off_diag_pair(t, c):
            off_diag(2 * t)
            off_diag(2 * t + 1)
            return c

        lax.fori_loop(0, i // 2, off_diag_pair, 0)

        @pl.when(i % 2 == 1)
        def _():
            off_diag(i - 1)

        for h in range(ATT_HP):
            update(h, all_rows, q0, th, diag)
            update(h, second_half, q0 + th, th, diag)

        for h in range(ATT_HP):
            for half in range(2):
                a1 = acc_ref[h, (2 * half) * th:(2 * half + 1) * th, :]
                a2 = acc_ref[h, (2 * half + 1) * th:(2 * half + 2) * th, :]
                o = (a1[:, :LANES] * (1.0 / a1[:, LANES:])
                     - a2[:, :LANES] * (lam * (1.0 / a2[:, LANES:])))
                o = o * lax.rsqrt(jnp.mean(o * o, axis=-1, keepdims=True) + LN_EPS) * gain
                o_ref[0, pl.ds(q0 + half * th, th), h * LANES:(h + 1) * LANES] = o.astype(BF16)
        return carry

    lax.fori_loop(0, SEQ // tq, q_block, 0)


def _attn_call(lq1, lk1, lq2, lk2, g, q, k, v, lambda_init):
    const = lambda b, h: (0, 0)
    small = lambda a: pl.BlockSpec(a.shape, const)
    head_spec = pl.BlockSpec((1, SEQ, ATT_HP * LANES), lambda b, h: (b, 0, h))
    return pl.pallas_call(
        functools.partial(_attn_kernel, lambda_init=lambda_init),
        grid=(BATCH, DIFF_HEADS // ATT_HP),
        in_specs=[small(lq1), small(lk1), small(lq2), small(lk2), small(g),
                  head_spec, head_spec, head_spec],
        out_specs=head_spec,
        out_shape=jax.ShapeDtypeStruct((BATCH, SEQ, V_WIDTH), BF16),
        scratch_shapes=[pltpu.VMEM((ATT_HP, 4 * ATT_TH, LANES), BF16),
                        pltpu.VMEM((ATT_HP, 4 * ATT_TH, LANES), F32),
                        pltpu.VMEM((ATT_HP, 4 * ATT_TH, 2 * LANES), F32)],
        compiler_params=pltpu.CompilerParams(
            dimension_semantics=("arbitrary", "arbitrary"), vmem_limit_bytes=VMEM_LIMIT),
        name="diff_attn",
    )(lq1, lk1, lq2, lk2, g, q, k, v)


def _block_diag(w):
    same_block = np.eye(LRU_BLOCKS, dtype=np.float32)[:, None, :, None]
    return (w[:, :, None, :] * same_block).reshape(B_WIDTH, B_WIDTH)


def _rope_tables():
    half_dim = DIFF_HEAD_DIM // 2
    pos = np.arange(SEQ, dtype=np.float64)
    inv_freq = ROPE_THETA ** (-np.arange(0, DIFF_HEAD_DIM, 2, dtype=np.float64) / DIFF_HEAD_DIM)
    lane = np.arange(LANES)
    ang = pos[:, None] * inv_freq[lane % half_dim][None, :]
    sign = np.where(lane % DIFF_HEAD_DIM < half_dim, -1.0, 1.0)
    return (jnp.asarray(np.cos(ang), dtype=F32), jnp.asarray(np.sin(ang) * sign[None, :], dtype=F32))


def kernel(x, even_w_in, even_b_in, even_conv_w, even_conv_b, even_cnorm_g, even_cnorm_b,
           even_lru_conv_w, even_lru_conv_b, even_w_a, even_b_a, even_w_x, even_b_x,
           even_lru_lambda, even_w_out, odd_w_qkv, odd_lambda_q1, odd_lambda_k1,
           odd_lambda_q2, odd_lambda_k2, odd_subln_g, odd_w_out, mix_ln_g, mix_ln_b,
           ffn_w_gate, ffn_w_up, ffn_w_down, ffn_ln_g, ffn_ln_b):
    row = lambda a: a.reshape(1, -1)
    rows = BATCH * SEQ

    def post(m, xres, weights, layer):
        w_out, wg, wu, wd = weights
        return _post_call(m.reshape(rows, -1), xres.reshape(rows, D_MODEL), w_out,
                          row(mix_ln_g[layer]), row(mix_ln_b[layer]), wg, wu, wd,
                          row(ffn_ln_g[layer]), row(ffn_ln_b[layer]))

    w_gate = jnp.concatenate([_block_diag(even_w_a[0]), _block_diag(even_w_x[0])], axis=1)
    b_gate = jnp.concatenate([even_b_a[0], even_b_x[0]]).reshape(1, -1)
    conv_w = jnp.broadcast_to(even_conv_w[0][:, None, :], (CONV_WIDTH, SUBLANES, A_WIDTH))
    m0, *tail0 = _mixer_call(x, even_w_in[0], row(even_b_in[0]), conv_w,
                             row(even_conv_b[0]), row(even_cnorm_g[0]), row(even_cnorm_b[0]),
                             even_lru_conv_w[0], row(even_lru_conv_b[0]), w_gate.astype(BF16),
                             b_gate, row(even_lru_lambda[0]),
                             [even_w_out, ffn_w_gate, ffn_w_up, ffn_w_down], [0, 0, 0, 0])
    x1 = post(m0, x, tail0, 0)

    lambda_init = 0.8 - 0.6 * math.exp(-0.3 * 1)
    cos, sin = _rope_tables()
    q, k, v, *tail1 = _qkv_call(x1, odd_w_qkv[0], cos, sin,
                                [odd_w_out, ffn_w_gate, ffn_w_up, ffn_w_down], [0, 1, 1, 1])
    shape3 = (BATCH, SEQ, QK_WIDTH)
    o = _attn_call(row(odd_lambda_q1[0]), row(odd_lambda_k1[0]), row(odd_lambda_q2[0]),
                   row(odd_lambda_k2[0]), row(odd_subln_g[0]),
                   q.reshape(shape3), k.reshape(shape3), v.reshape(shape3), lambda_init)
    out = post(o, x1, tail1, 1)
    return out.reshape(BATCH, SEQ, D_MODEL)
```

```python
import functools
import math

import jax
import jax.numpy as jnp
import numpy as np
from jax import lax
from jax.experimental import pallas as pl
from jax.experimental.pallas import tpu as pltpu

F32 = jnp.float32
BF16 = jnp.bfloat16

D_MODEL = 1024
BATCH = 8
SEQ = 2048
DEPTH = 2
CHUNK = 64
A_WIDTH = 512
B_WIDTH = 512
CONV_WIDTH = 31
LRU_BLOCKS = 8
LRU_CONV_WIDTH = 4
LRU_C = 8.0
IN_WIDTH = 2 * A_WIDTH + 2 * B_WIDTH
DIFF_HEADS = 8
DIFF_HEAD_DIM = 64
QK_WIDTH = 1024
V_WIDTH = 1024
ROPE_THETA = 10000.0
D_FF = 2816
LN_EPS = 1e-5
DN_ALPHA = (2 * DEPTH) ** 0.25
NEG_INF = -1e30
Q_SCALE = DIFF_HEAD_DIM ** -0.5 * math.log2(math.e)

LANES = 128
SUBLANES = 8
BF16_SUBLANES = 16
MXU_COLS = 256
VMEM_LIMIT = 56 * 1024 * 1024

MIX_TS = 512
CONV_HALO = 32
CONV_ROWS = 32
CONV_BLK = 64
LRU_HALO = 8
POST_TM = 1024
FF_CHUNK = 512
QKV_TM = 1024
ATT_HP = 8
ATT_TH = 256


def _ln(x, g, b):
    mu = jnp.mean(x, axis=-1, keepdims=True)
    xc = x - mu
    var = jnp.mean(xc * xc, axis=-1, keepdims=True)
    return xc * lax.rsqrt(var + LN_EPS) * g + b


def _dot(a, b):
    return jnp.dot(a, b, preferred_element_type=F32)


N_CAST = 4


def _cast_slabs(cast_in, cast_out):
    for src, dst in zip(cast_in, cast_out):
        dst[...] = src[...].astype(BF16)


def _cast_specs(weights, layers, step_of, n_steps):
    in_specs, out_specs, out_shapes = [], [], []
    for w, layer in zip(weights, layers):
        _, rows, cols = w.shape
        slab = rows // n_steps
        per_slab = 1
        while (slab * per_slab) % BF16_SUBLANES:
            per_slab *= 2
        slab *= per_slab
        in_specs.append(pl.BlockSpec(
            (None, slab, cols), lambda *g, l=layer, p=per_slab: (l, step_of(*g) // p, 0)))
        out_specs.append(pl.BlockSpec(
            (slab, cols), lambda *g, p=per_slab: (step_of(*g) // p, 0)))
        out_shapes.append(jax.ShapeDtypeStruct((rows, cols), BF16))
    return in_specs, out_specs, out_shapes


def _mixer_kernel(x_ref, w_in_f32_ref, b_in_ref, conv_w_ref, conv_b_ref, cn_g_ref, cn_b_ref,
                  lconv_w_ref, lconv_b_ref, w_gate_ref, b_gate_ref, lam_ref, *rest):
    cast_in, rest = rest[:N_CAST], rest[N_CAST:]
    out_ref, rest = rest[0], rest[1:]
    cast_out, rest = rest[:N_CAST], rest[N_CAST:]
    (a_ext, a_sh, b_ext, h_carry, bgate_ref, gates_ref, au_ref, xb_ref, conv_ref,
     w_in_ref) = rest
    ts = MIX_TS
    s = pl.program_id(1)

    n_groups = A_WIDTH // LANES
    _cast_slabs(cast_in, cast_out)

    @pl.when((pl.program_id(0) == 0) & (s == 0))
    def _():
        w_in_ref[...] = w_in_f32_ref[...].astype(BF16)

    @pl.when(s == 0)
    def _():
        a_ext[:, 0:CONV_HALO, :] = jnp.zeros((n_groups, CONV_HALO, LANES), F32)
        a_ext[:, CONV_HALO + ts:, :] = jnp.zeros((n_groups, CONV_BLK, LANES), F32)
        b_ext[0:LRU_HALO, :] = jnp.zeros((LRU_HALO, B_WIDTH), F32)
        h_carry[...] = jnp.zeros_like(h_carry)

    xb_ref[...] = x_ref[0].astype(BF16)
    ha = _dot(xb_ref[...], w_in_ref[:, 0:2 * A_WIDTH]) + b_in_ref[:, 0:2 * A_WIDTH]
    glu = ha[:, 0:A_WIDTH] * jax.nn.sigmoid(ha[:, A_WIDTH:])
    for g in range(n_groups):
        a_ext[g, CONV_HALO:CONV_HALO + ts, :] = glu[:, g * LANES:(g + 1) * LANES]

    off = CONV_HALO - (CONV_WIDTH - 1)

    for g in range(n_groups):
        def shift_body(i, carry, g=g):
            base = pl.multiple_of(i * CONV_BLK, CONV_BLK)
            win = a_ext[g, pl.ds(base, CONV_BLK + SUBLANES), :]
            for r in range(1, SUBLANES):
                rolled = pltpu.roll(win, CONV_BLK + SUBLANES - r, axis=0)
                a_sh[r - 1, g, pl.ds(base, CONV_BLK), :] = rolled[0:CONV_BLK, :]
            return carry

        lax.fori_loop(0, (ts + CONV_BLK) // CONV_BLK, shift_body, 0)

    for g in range(n_groups):
        sl = slice(g * LANES, (g + 1) * LANES)
        taps = [conv_w_ref[j, :, sl] for j in range(CONV_WIDTH)]
        bias = conv_b_ref[:, sl]

        def conv_body(i, carry, g=g, sl=sl, taps=taps, bias=bias):
            base = pl.multiple_of(i * CONV_BLK, CONV_BLK)
            acc = jnp.broadcast_to(bias, (CONV_BLK, LANES))
            for j in range(CONV_WIDTH):
                q, r = divmod(off + j, SUBLANES)
                rows = pl.ds(base + q * SUBLANES, CONV_BLK)
                win = a_ext[g, rows, :] if r == 0 else a_sh[r - 1, g, rows, :]
                acc = acc + (win.reshape(-1, SUBLANES, LANES) * taps[j][None]).reshape(CONV_BLK, LANES)
            conv_ref[pl.ds(base, CONV_BLK), sl] = acc
            return carry

        lax.fori_loop(0, ts // CONV_BLK, conv_body, 0, unroll=2)
    a_ext[:, 0:CONV_HALO, :] = a_ext[:, ts:ts + CONV_HALO, :]

    def norm_body(i, carry):
        base = pl.multiple_of(i * CONV_ROWS, CONV_ROWS)
        ya = jax.nn.silu(_ln(conv_ref[pl.ds(base, CONV_ROWS), :], cn_g_ref[...], cn_b_ref[...]))
        out_ref[0, pl.ds(base, CONV_ROWS), 0:A_WIDTH] = ya.astype(BF16)
        return carry

    lax.fori_loop(0, ts // CONV_ROWS, norm_body, 0, unroll=True)

    hb = _dot(xb_ref[...], w_in_ref[:, 2 * A_WIDTH:]) + b_in_ref[:, 2 * A_WIDTH:]
    bgate_ref[...] = hb[:, 0:B_WIDTH]
    b_ext[LRU_HALO:LRU_HALO + ts, :] = hb[:, B_WIDTH:]
    loff = LRU_HALO - (LRU_CONV_WIDTH - 1)
    xc = jnp.broadcast_to(lconv_b_ref[...], (ts, B_WIDTH))
    for j in range(LRU_CONV_WIDTH):
        xc = xc + lconv_w_ref[j:j + 1, :] * b_ext[loff + j:loff + j + ts, :]
    b_ext[0:LRU_HALO, :] = b_ext[ts:ts + LRU_HALO, :]
    gates_ref[...] = _dot(xc.astype(BF16), w_gate_ref[...]) + b_gate_ref[...]
    b_ext[LRU_HALO:LRU_HALO + ts, :] = xc

    def scan_steps(a_cum, u_cum, axis, steps):
        idx = lax.broadcasted_iota(jnp.int32, a_cum.shape, axis)
        for step in steps:
            keep = idx >= step
            a_prev = jnp.where(keep, pltpu.roll(a_cum, step, axis=axis), 1.0)
            u_prev = jnp.where(keep, pltpu.roll(u_cum, step, axis=axis), 0.0)
            u_cum = a_cum * u_prev + u_cum
            a_cum = a_cum * a_prev
        return a_cum, u_cum

    nblk = ts // SUBLANES
    for g in range(B_WIDTH // LANES):
        sl = slice(g * LANES, (g + 1) * LANES)
        xg = b_ext[LRU_HALO:LRU_HALO + ts, sl]
        gate_r = jax.nn.sigmoid(gates_ref[:, sl])
        gate_i = jax.nn.sigmoid(gates_ref[:, B_WIDTH + g * LANES:B_WIDTH + (g + 1) * LANES])
        lam = lam_ref[:, sl]
        neg = -lam
        softplus = jnp.maximum(neg, 0.0) + jnp.log1p(jnp.exp(-jnp.abs(neg)))
        log_a = (-LRU_C * gate_r) * softplus
        th = jnp.tanh(log_a)
        y = -2.0 * th
        scale = jnp.where(y > 0.0, y * lax.rsqrt(y), 0.0) * lax.rsqrt(1.0 - th)
        a_blk, u_blk = scan_steps(jnp.exp(log_a).reshape(nblk, SUBLANES, LANES),
                                  (scale * (gate_i * xg)).reshape(nblk, SUBLANES, LANES),
                                  1, (1, 2, 4))
        au_ref[0] = a_blk.reshape(ts, LANES)
        au_ref[1] = u_blk.reshape(ts, LANES)
        last = pl.ds(SUBLANES - 1, nblk, stride=SUBLANES)
        a_end, u_end = scan_steps(au_ref[0, last, :], au_ref[1, last, :], 0,
                                  [1 << k for k in range(nblk.bit_length() - 1)])
        h_prev = h_carry[0:1, sl]
        h_end = a_end * h_prev + u_end
        h_carry[:, sl] = jnp.broadcast_to(h_end[nblk - 1:nblk, :], (SUBLANES, LANES))
        row0 = lax.broadcasted_iota(jnp.int32, (nblk, LANES), 0) == 0
        h_in = jnp.where(row0, h_prev, pltpu.roll(h_end, 1, axis=0))
        hg = (a_blk * h_in[:, None, :] + u_blk).reshape(ts, LANES)
        yb = hg * jax.nn.gelu(bgate_ref[:, sl])
        out_ref[0, :, A_WIDTH + g * LANES:A_WIDTH + (g + 1) * LANES] = yb.astype(BF16)


def _mixer_call(x, w_in, b_in, conv_w, conv_b, cn_g, cn_b, lconv_w, lconv_b, w_gate, b_gate, lam,
                cast_weights, cast_layers):
    ts = MIX_TS
    seq_steps = SEQ // ts
    full = lambda a: pl.BlockSpec(a.shape, lambda b, s, nd=a.ndim: (0,) * nd,
                                  pipeline_mode=pl.Buffered(1))
    cast_in, cast_out, cast_shapes = _cast_specs(
        cast_weights, cast_layers, lambda b, s: b * seq_steps + s, BATCH * seq_steps)
    return pl.pallas_call(
        _mixer_kernel,
        grid=(BATCH, seq_steps),
        in_specs=[pl.BlockSpec((1, ts, D_MODEL), lambda b, s: (b, s, 0)),
                  full(w_in), full(b_in), full(conv_w), full(conv_b), full(cn_g), full(cn_b),
                  full(lconv_w), full(lconv_b), full(w_gate), full(b_gate), full(lam)] + cast_in,
        out_specs=[pl.BlockSpec((1, ts, A_WIDTH + B_WIDTH), lambda b, s: (b, s, 0))] + cast_out,
        out_shape=[jax.ShapeDtypeStruct((BATCH, SEQ, A_WIDTH + B_WIDTH), BF16)] + cast_shapes,
        scratch_shapes=[pltpu.VMEM((A_WIDTH // LANES, CONV_HALO + ts + CONV_BLK, LANES), F32),
                        pltpu.VMEM((SUBLANES - 1, A_WIDTH // LANES, ts + CONV_BLK, LANES), F32),
                        pltpu.VMEM((LRU_HALO + ts, B_WIDTH), F32),
                        pltpu.VMEM((SUBLANES, B_WIDTH), F32),
                        pltpu.VMEM((ts, B_WIDTH), F32),
                        pltpu.VMEM((ts, 2 * B_WIDTH), F32),
                        pltpu.VMEM((2, ts, LANES), F32),
                        pltpu.VMEM((ts, D_MODEL), BF16),
                        pltpu.VMEM((ts, A_WIDTH), F32),
                        pltpu.VMEM((D_MODEL, IN_WIDTH), BF16)],
        compiler_params=pltpu.CompilerParams(
            dimension_semantics=("arbitrary", "arbitrary"), vmem_limit_bytes=VMEM_LIMIT),
        name="mixer0",
    )(x, w_in, b_in, conv_w, conv_b, cn_g, cn_b, lconv_w, lconv_b, w_gate, b_gate, lam,
      *cast_weights)


def _post_kernel(m_ref, x_ref, w_out_ref, g1_ref, b1_ref, wg_ref, wu_ref, wd_ref, g2_ref, b2_ref,
                 out_ref, x1_ref, xb_ref, acc_ref):
    y = _dot(m_ref[...], w_out_ref[...])
    x1 = _ln(DN_ALPHA * x_ref[...] + y, g1_ref[...], b1_ref[...])
    x1_ref[...] = x1
    xb_ref[...] = x1.astype(BF16)
    bounds = list(range(0, D_FF, FF_CHUNK)) + [D_FF]
    for c in range(len(bounds) - 1):
        cs = slice(bounds[c], bounds[c + 1])
        gate = _dot(xb_ref[...], wg_ref[:, cs])
        up = _dot(xb_ref[...], wu_ref[:, cs])
        act = (jax.nn.silu(gate) * up).astype(BF16)
        contrib = _dot(act, wd_ref[cs, :])
        if c == 0:
            acc_ref[...] = contrib
        else:
            acc_ref[...] += contrib
    out_ref[...] = _ln(DN_ALPHA * x1_ref[...] + acc_ref[...], g2_ref[...], b2_ref[...])


def _post_call(m, x, w_out, g1, b1, wg, wu, wd, g2, b2):
    tm = POST_TM
    rows = m.shape[0]
    const = lambda i: (0, 0)
    full = lambda a: pl.BlockSpec(a.shape, const, pipeline_mode=pl.Buffered(1))
    return pl.pallas_call(
        _post_kernel,
        grid=(rows // tm,),
        in_specs=[pl.BlockSpec((tm, D_MODEL), lambda i: (i, 0)),
                  pl.BlockSpec((tm, D_MODEL), lambda i: (i, 0)),
                  full(w_out), full(g1), full(b1), full(wg), full(wu), full(wd),
                  full(g2), full(b2)],
        out_specs=pl.BlockSpec((tm, D_MODEL), lambda i: (i, 0)),
        out_shape=jax.ShapeDtypeStruct((rows, D_MODEL), F32),
        scratch_shapes=[pltpu.VMEM((tm, D_MODEL), F32),
                        pltpu.VMEM((tm, D_MODEL), BF16),
                        pltpu.VMEM((tm, D_MODEL), F32)],
        compiler_params=pltpu.CompilerParams(
            dimension_semantics=("arbitrary",), vmem_limit_bytes=VMEM_LIMIT),
        name="post",
    )(m, x, w_out, g1, b1, wg, wu, wd, g2, b2)


def _qkv_kernel(x_ref, w_f32_ref, cos_ref, sin_ref, *rest):
    cast_in, rest = rest[:N_CAST], rest[N_CAST:]
    q_ref, k_ref, v_ref = rest[:3]
    cast_out, (w_ref,) = rest[3:3 + N_CAST], rest[3 + N_CAST:]
    _cast_slabs(cast_in, cast_out)

    @pl.when(pl.program_id(0) == 0)
    def _():
        w_ref[...] = w_f32_ref[...].astype(BF16)

    xb = x_ref[...].astype(BF16)
    cos = cos_ref[...]
    sin = sin_ref[...]
    half_dim = DIFF_HEAD_DIM // 2
    lane = lax.broadcasted_iota(jnp.int32, (1, LANES), 1)
    first_half = (lane % DIFF_HEAD_DIM) < half_dim
    for g in range(2 * QK_WIDTH // MXU_COLS):
        t2 = _dot(xb, w_ref[:, g * MXU_COLS:(g + 1) * MXU_COLS])
        for half in range(MXU_COLS // LANES):
            t = t2[:, half * LANES:(half + 1) * LANES]
            rot = jnp.where(first_half, pltpu.roll(t, LANES - half_dim, axis=1),
                            pltpu.roll(t, half_dim, axis=1))
            r = t * cos + rot * sin
            col = g * MXU_COLS + half * LANES
            if col < QK_WIDTH:
                q_ref[:, col:col + LANES] = (r * Q_SCALE).astype(BF16)
            else:
                k_ref[:, col - QK_WIDTH:col - QK_WIDTH + LANES] = r.astype(BF16)
    v_ref[...] = _dot(xb, w_ref[:, 2 * QK_WIDTH:]).astype(BF16)


def _qkv_call(x, w, cos, sin, cast_weights, cast_layers):
    tm = QKV_TM
    rows = x.shape[0]
    pos_blocks = SEQ // tm
    row_spec = pl.BlockSpec((tm, D_MODEL), lambda i: (i, 0))
    tab_spec = pl.BlockSpec((tm, LANES), lambda i: (i % pos_blocks, 0))
    out = jax.ShapeDtypeStruct((rows, QK_WIDTH), BF16)
    cast_in, cast_out, cast_shapes = _cast_specs(cast_weights, cast_layers, lambda i: i,
                                                 rows // tm)
    return pl.pallas_call(
        _qkv_kernel,
        grid=(rows // tm,),
        in_specs=[row_spec,
                  pl.BlockSpec(w.shape, lambda i: (0, 0), pipeline_mode=pl.Buffered(1)),
                  tab_spec, tab_spec] + cast_in,
        out_specs=[row_spec, row_spec, row_spec] + cast_out,
        out_shape=[out, out, out] + cast_shapes,
        scratch_shapes=[pltpu.VMEM(w.shape, BF16)],
        compiler_params=pltpu.CompilerParams(
            dimension_semantics=("arbitrary",), vmem_limit_bytes=VMEM_LIMIT),
        name="qkv",
    )(x, w, cos, sin, *cast_weights)


def _attn_kernel(lq1_ref, lk1_ref, lq2_ref, lk2_ref, g_ref, q_ref, k_ref, v_ref, o_ref,
                 qs_ref, m_ref, acc_ref, *, lambda_init):
    th = ATT_TH
    tq = 2 * th
    lane = lax.broadcasted_iota(jnp.int32, (1, LANES), 1)
    is_map1 = lane < DIFF_HEAD_DIM
    lam = (jnp.exp(jnp.sum(lq1_ref[...] * lk1_ref[...], axis=-1, keepdims=True))
           - jnp.exp(jnp.sum(lq2_ref[...] * lk2_ref[...], axis=-1, keepdims=True)) + lambda_init)
    gain = g_ref[...] * (1.0 - lambda_init)
    zero = jnp.zeros((), BF16)

    def chunk_mask(n_rows):
        row = lax.broadcasted_iota(jnp.int32, (n_rows, th), 0) % th
        return row // CHUNK >= lax.broadcasted_iota(jnp.int32, (n_rows, th), 1) // CHUNK

    diag = chunk_mask(2 * th)

    def update(h, rows, key_start, n_keys, visible):
        ks = pl.ds(key_start, n_keys)
        hs = slice(h * LANES, (h + 1) * LANES)
        s = lax.dot_general(qs_ref[h, rows, :], k_ref[0, ks, hs], (((1,), (1,)), ((), ())),
                            preferred_element_type=F32)
        if visible is not None:
            n = visible.shape[0]
            masked = jnp.where(visible, s[:n], NEG_INF)
            s = masked if n == s.shape[0] else jnp.concatenate([masked, s[n:]], axis=0)
        m_old = m_ref[h, rows, :]
        m_cur = jnp.max(s, axis=-1, keepdims=True)
        m_new = jnp.maximum(m_old, jnp.broadcast_to(m_cur, m_old.shape))
        alpha = jnp.exp2(m_old - m_new)
        e = jnp.exp2(s - jnp.concatenate([m_new] * (n_keys // LANES), axis=1)).astype(BF16)
        v_ext = jnp.concatenate([v_ref[0, ks, hs], jnp.ones((n_keys, LANES), BF16)], axis=1)
        acc_ref[h, rows, :] = (jnp.concatenate([alpha, alpha], axis=1) * acc_ref[h, rows, :]
                               + _dot(e, v_ext))
        m_ref[h, rows, :] = m_new

    all_rows = slice(0, 4 * th)
    second_half = slice(2 * th, 4 * th)

    def q_block(i, carry):
        q0 = pl.multiple_of(i * tq, tq)
        for h in range(ATT_HP):
            for half in range(2):
                q = q_ref[0, pl.ds(q0 + half * th, th), h * LANES:(h + 1) * LANES]
                qs_ref[h, (2 * half) * th:(2 * half + 1) * th, :] = jnp.where(is_map1, q, zero)
                qs_ref[h, (2 * half + 1) * th:(2 * half + 2) * th, :] = jnp.where(is_map1, zero, q)
        m_ref[...] = jnp.full(m_ref.shape, NEG_INF, F32)
        acc_ref[...] = jnp.zeros(acc_ref.shape, F32)

        def off_diag(j, c):
            for h in range(ATT_HP):
                update(h, all_rows, pl.multiple_of(j * tq, tq), tq, None)
            return c

        lax.fori_loop(0, i, off_diag, 0)
        for h in range(ATT_HP):
            update(h, all_rows, q0, th, diag)
            update(h, second_half, q0 + th, th, diag)

        for h in range(ATT_HP):
            for half in range(2):
                a1 = acc_ref[h, (2 * half) * th:(2 * half + 1) * th, :]
                a2 = acc_ref[h, (2 * half + 1) * th:(2 * half + 2) * th, :]
                o = (a1[:, :LANES] * (1.0 / a1[:, LANES:])
                     - a2[:, :LANES] * (lam * (1.0 / a2[:, LANES:])))
                o = o * lax.rsqrt(jnp.mean(o * o, axis=-1, keepdims=True) + LN_EPS) * gain
                o_ref[0, pl.ds(q0 + half * th, th), h * LANES:(h + 1) * LANES] = o.astype(BF16)
        return carry

    lax.fori_loop(0, SEQ // tq, q_block, 0)


def _attn_call(lq1, lk1, lq2, lk2, g, q, k, v, lambda_init):
    const = lambda b, h: (0, 0)
    small = lambda a: pl.BlockSpec(a.shape, const)
    head_spec = pl.BlockSpec((1, SEQ, ATT_HP * LANES), lambda b, h: (b, 0, h))
    return pl.pallas_call(
        functools.partial(_attn_kernel, lambda_init=lambda_init),
        grid=(BATCH, DIFF_HEADS // ATT_HP),
        in_specs=[small(lq1), small(lk1), small(lq2), small(lk2), small(g),
                  head_spec, head_spec, head_spec],
        out_specs=head_spec,
        out_shape=jax.ShapeDtypeStruct((BATCH, SEQ, V_WIDTH), BF16),
        scratch_shapes=[pltpu.VMEM((ATT_HP, 4 * ATT_TH, LANES), BF16),
                        pltpu.VMEM((ATT_HP, 4 * ATT_TH, LANES), F32),
                        pltpu.VMEM((ATT_HP, 4 * ATT_TH, 2 * LANES), F32)],
        compiler_params=pltpu.CompilerParams(
            dimension_semantics=("arbitrary", "arbitrary"), vmem_limit_bytes=VMEM_LIMIT),
        name="diff_attn",
    )(lq1, lk1, lq2, lk2, g, q, k, v)


def _block_diag(w):
    same_block = np.eye(LRU_BLOCKS, dtype=np.float32)[:, None, :, None]
    return (w[:, :, None, :] * same_block).reshape(B_WIDTH, B_WIDTH)


def _rope_tables():
    half_dim = DIFF_HEAD_DIM // 2
    pos = np.arange(SEQ, dtype=np.float64)
    inv_freq = ROPE_THETA ** (-np.arange(0, DIFF_HEAD_DIM, 2, dtype=np.float64) / DIFF_HEAD_DIM)
    lane = np.arange(LANES)
    ang = pos[:, None] * inv_freq[lane % half_dim][None, :]
    sign = np.where(lane % DIFF_HEAD_DIM < half_dim, -1.0, 1.0)
    return (jnp.asarray(np.cos(ang), dtype=F32), jnp.asarray(np.sin(ang) * sign[None, :], dtype=F32))


def kernel(x, even_w_in, even_b_in, even_conv_w, even_conv_b, even_cnorm_g, even_cnorm_b,
           even_lru_conv_w, even_lru_conv_b, even_w_a, even_b_a, even_w_x, even_b_x,
           even_lru_lambda, even_w_out, odd_w_qkv, odd_lambda_q1, odd_lambda_k1,
           odd_lambda_q2, odd_lambda_k2, odd_subln_g, odd_w_out, mix_ln_g, mix_ln_b,
           ffn_w_gate, ffn_w_up, ffn_w_down, ffn_ln_g, ffn_ln_b):
    row = lambda a: a.reshape(1, -1)
    rows = BATCH * SEQ

    def post(m, xres, weights, layer):
        w_out, wg, wu, wd = weights
        return _post_call(m.reshape(rows, -1), xres.reshape(rows, D_MODEL), w_out,
                          row(mix_ln_g[layer]), row(mix_ln_b[layer]), wg, wu, wd,
                          row(ffn_ln_g[layer]), row(ffn_ln_b[layer]))

    w_gate = jnp.concatenate([_block_diag(even_w_a[0]), _block_diag(even_w_x[0])], axis=1)
    b_gate = jnp.concatenate([even_b_a[0], even_b_x[0]]).reshape(1, -1)
    conv_w = jnp.broadcast_to(even_conv_w[0][:, None, :], (CONV_WIDTH, SUBLANES, A_WIDTH))
    m0, *tail0 = _mixer_call(x, even_w_in[0], row(even_b_in[0]), conv_w,
                             row(even_conv_b[0]), row(even_cnorm_g[0]), row(even_cnorm_b[0]),
                             even_lru_conv_w[0], row(even_lru_conv_b[0]), w_gate.astype(BF16),
                             b_gate, row(even_lru_lambda[0]),
                             [even_w_out, ffn_w_gate, ffn_w_up, ffn_w_down], [0, 0, 0, 0])
    x1 = post(m0, x, tail0, 0)

    lambda_init = 0.8 - 0.6 * math.exp(-0.3 * 1)
    cos, sin = _rope_tables()
    q, k, v, *tail1 = _qkv_call(x1, odd_w_qkv[0], cos, sin,
                                [odd_w_out, ffn_w_gate, ffn_w_up, ffn_w_down], [0, 1, 1, 1])
    shape3 = (BATCH, SEQ, QK_WIDTH)
    o = _attn_call(row(odd_lambda_q1[0]), row(odd_lambda_k1[0]), row(odd_lambda_q2[0]),
                   row(odd_lambda_k2[0]), row(odd_subln_g[0]),
                   q.reshape(shape3), k.reshape(shape3), v.reshape(shape3), lambda_init)
    out = post(o, x1, tail1, 1)
    return out.reshape(BATCH, SEQ, D_MODEL)
```

```python
import functools
import math

import jax
import jax.numpy as jnp
import numpy as np
from jax import lax
from jax.experimental import pallas as pl
from jax.experimental.pallas import tpu as pltpu

F32 = jnp.float32
BF16 = jnp.bfloat16

D_MODEL = 1024
BATCH = 8
SEQ = 2048
DEPTH = 2
CHUNK = 64
A_WIDTH = 512
B_WIDTH = 512
CONV_WIDTH = 31
LRU_BLOCKS = 8
LRU_CONV_WIDTH = 4
LRU_C = 8.0
IN_WIDTH = 2 * A_WIDTH + 2 * B_WIDTH
DIFF_HEADS = 8
DIFF_HEAD_DIM = 64
QK_WIDTH = 1024
V_WIDTH = 1024
ROPE_THETA = 10000.0
D_FF = 2816
LN_EPS = 1e-5
DN_ALPHA = (2 * DEPTH) ** 0.25
NEG_INF = -1e30
Q_SCALE = DIFF_HEAD_DIM ** -0.5 * math.log2(math.e)

LANES = 128
SUBLANES = 8
BF16_SUBLANES = 16
MXU_COLS = 256
VMEM_LIMIT = 56 * 1024 * 1024

MIX_TS = 512
CONV_HALO = 32
CONV_ROWS = 32
CONV_BLK = 64
PHASE_SKEW = 8
LRU_HALO = 8
POST_TM = 1024
FF_CHUNK = 256
QKV_TM = 1024
ATT_HP = 8
ATT_TH = 256


def _ln(x, g, b):
    mu = jnp.mean(x, axis=-1, keepdims=True)
    xc = x - mu
    var = jnp.mean(xc * xc, axis=-1, keepdims=True)
    return xc * lax.rsqrt(var + LN_EPS) * g + b


def _dot(a, b):
    return jnp.dot(a, b, preferred_element_type=F32)


N_CAST = 4


def _cast_slabs(cast_in, cast_out):
    for src, dst in zip(cast_in, cast_out):
        dst[...] = src[...].astype(BF16)


def _cast_specs(weights, layers, step_of, n_steps):
    in_specs, out_specs, out_shapes = [], [], []
    for w, layer in zip(weights, layers):
        _, rows, cols = w.shape
        slab = rows // n_steps
        per_slab = 1
        while (slab * per_slab) % BF16_SUBLANES:
            per_slab *= 2
        slab *= per_slab
        in_specs.append(pl.BlockSpec(
            (None, slab, cols), lambda *g, l=layer, p=per_slab: (l, step_of(*g) // p, 0)))
        out_specs.append(pl.BlockSpec(
            (slab, cols), lambda *g, p=per_slab: (step_of(*g) // p, 0)))
        out_shapes.append(jax.ShapeDtypeStruct((rows, cols), BF16))
    return in_specs, out_specs, out_shapes


def _mixer_kernel(x_ref, w_in_f32_ref, b_in_ref, conv_w_ref, conv_b_ref, cn_g_ref, cn_b_ref,
                  lconv_w_ref, lconv_b_ref, w_gate_ref, b_gate_ref, lam_ref, *rest):
    cast_in, rest = rest[:N_CAST], rest[N_CAST:]
    out_ref, rest = rest[0], rest[1:]
    cast_out, rest = rest[:N_CAST], rest[N_CAST:]
    (a_ext, a_sh, b_ext, h_carry, bgate_ref, gates_ref, au_ref, xb_ref, conv_ref,
     w_in_ref) = rest
    ts = MIX_TS
    s = pl.program_id(1)

    n_groups = A_WIDTH // LANES
    _cast_slabs(cast_in, cast_out)

    @pl.when((pl.program_id(0) == 0) & (s == 0))
    def _():
        w_in_ref[...] = w_in_f32_ref[...].astype(BF16)

    @pl.when(s == 0)
    def _():
        a_ext[:, 0:CONV_HALO, :] = jnp.zeros((n_groups, CONV_HALO, LANES), F32)
        a_ext[:, CONV_HALO + ts:, :] = jnp.zeros((n_groups, CONV_BLK, LANES), F32)
        b_ext[0:LRU_HALO, :] = jnp.zeros((LRU_HALO, B_WIDTH), F32)
        h_carry[...] = jnp.zeros_like(h_carry)

    xb_ref[...] = x_ref[0].astype(BF16)
    ha = _dot(xb_ref[...], w_in_ref[:, 0:2 * A_WIDTH]) + b_in_ref[:, 0:2 * A_WIDTH]
    glu = ha[:, 0:A_WIDTH] * jax.nn.sigmoid(ha[:, A_WIDTH:])
    for g in range(n_groups):
        a_ext[g, CONV_HALO:CONV_HALO + ts, :] = glu[:, g * LANES:(g + 1) * LANES]

    off = CONV_HALO - (CONV_WIDTH - 1)

    for g in range(n_groups):
        def shift_body(i, carry, g=g):
            base = pl.multiple_of(i * CONV_BLK, CONV_BLK)
            win = a_ext[g, pl.ds(base, CONV_BLK + SUBLANES), :]
            for r in range(1, SUBLANES):
                rolled = pltpu.roll(win, CONV_BLK + SUBLANES - r, axis=0)
                a_sh[g, r - 1, pl.ds(base, CONV_BLK), :] = rolled[0:CONV_BLK, :]
            return carry

        lax.fori_loop(0, (ts + CONV_BLK) // CONV_BLK, shift_body, 0)

    for g in range(n_groups):
        sl = slice(g * LANES, (g + 1) * LANES)
        taps = [conv_w_ref[j, :, sl] for j in range(CONV_WIDTH)]
        bias = conv_b_ref[:, sl]

        def conv_body(i, carry, g=g, sl=sl, taps=taps, bias=bias):
            base = pl.multiple_of(i * CONV_BLK, CONV_BLK)
            acc = jnp.broadcast_to(bias, (CONV_BLK, LANES))
            for j in range(CONV_WIDTH):
                q, r = divmod(off + j, SUBLANES)
                rows = pl.ds(base + q * SUBLANES, CONV_BLK)
                win = a_ext[g, rows, :] if r == 0 else a_sh[g, r - 1, rows, :]
                acc = acc + (win.reshape(-1, SUBLANES, LANES) * taps[j][None]).reshape(CONV_BLK, LANES)
            conv_ref[pl.ds(base, CONV_BLK), sl] = acc
            return carry

        lax.fori_loop(0, ts // CONV_BLK, conv_body, 0, unroll=2)
    a_ext[:, 0:CONV_HALO, :] = a_ext[:, ts:ts + CONV_HALO, :]

    def norm_body(i, carry):
        base = pl.multiple_of(i * CONV_ROWS, CONV_ROWS)
        ya = jax.nn.silu(_ln(conv_ref[pl.ds(base, CONV_ROWS), :], cn_g_ref[...], cn_b_ref[...]))
        out_ref[0, pl.ds(base, CONV_ROWS), 0:A_WIDTH] = ya.astype(BF16)
        return carry

    lax.fori_loop(0, ts // CONV_ROWS, norm_body, 0, unroll=True)

    hb = _dot(xb_ref[...], w_in_ref[:, 2 * A_WIDTH:]) + b_in_ref[:, 2 * A_WIDTH:]
    bgate_ref[...] = hb[:, 0:B_WIDTH]
    b_ext[LRU_HALO:LRU_HALO + ts, :] = hb[:, B_WIDTH:]
    loff = LRU_HALO - (LRU_CONV_WIDTH - 1)
    xc = jnp.broadcast_to(lconv_b_ref[...], (ts, B_WIDTH))
    for j in range(LRU_CONV_WIDTH):
        xc = xc + lconv_w_ref[j:j + 1, :] * b_ext[loff + j:loff + j + ts, :]
    b_ext[0:LRU_HALO, :] = b_ext[ts:ts + LRU_HALO, :]
    gates_ref[...] = _dot(xc.astype(BF16), w_gate_ref[...]) + b_gate_ref[...]
    b_ext[LRU_HALO:LRU_HALO + ts, :] = xc

    def scan_steps(a_cum, u_cum, axis, steps):
        idx = lax.broadcasted_iota(jnp.int32, a_cum.shape, axis)
        for step in steps:
            keep = idx >= step
            a_prev = jnp.where(keep, pltpu.roll(a_cum, step, axis=axis), 1.0)
            u_prev = jnp.where(keep, pltpu.roll(u_cum, step, axis=axis), 0.0)
            u_cum = a_cum * u_prev + u_cum
            a_cum = a_cum * a_prev
        return a_cum, u_cum

    nblk = ts // SUBLANES
    for g in range(B_WIDTH // LANES):
        sl = slice(g * LANES, (g + 1) * LANES)
        xg = b_ext[LRU_HALO:LRU_HALO + ts, sl]
        gate_r = jax.nn.sigmoid(gates_ref[:, sl])
        gate_i = jax.nn.sigmoid(gates_ref[:, B_WIDTH + g * LANES:B_WIDTH + (g + 1) * LANES])
        lam = lam_ref[:, sl]
        neg = -lam
        softplus = jnp.maximum(neg, 0.0) + jnp.log1p(jnp.exp(-jnp.abs(neg)))
        log_a = (-LRU_C * gate_r) * softplus
        th = jnp.tanh(log_a)
        y = -2.0 * th
        scale = jnp.where(y > 0.0, y * lax.rsqrt(y), 0.0) * lax.rsqrt(1.0 - th)
        a_blk, u_blk = scan_steps(jnp.exp(log_a).reshape(nblk, SUBLANES, LANES),
                                  (scale * (gate_i * xg)).reshape(nblk, SUBLANES, LANES),
                                  1, (1, 2, 4))
        au_ref[0] = a_blk.reshape(ts, LANES)
        au_ref[1] = u_blk.reshape(ts, LANES)
        last = pl.ds(SUBLANES - 1, nblk, stride=SUBLANES)
        a_end, u_end = scan_steps(au_ref[0, last, :], au_ref[1, last, :], 0,
                                  [1 << k for k in range(nblk.bit_length() - 1)])
        h_prev = h_carry[0:1, sl]
        h_end = a_end * h_prev + u_end
        h_carry[:, sl] = jnp.broadcast_to(h_end[nblk - 1:nblk, :], (SUBLANES, LANES))
        row0 = lax.broadcasted_iota(jnp.int32, (nblk, LANES), 0) == 0
        h_in = jnp.where(row0, h_prev, pltpu.roll(h_end, 1, axis=0))
        hg = (a_blk * h_in[:, None, :] + u_blk).reshape(ts, LANES)
        yb = hg * jax.nn.gelu(bgate_ref[:, sl])
        out_ref[0, :, A_WIDTH + g * LANES:A_WIDTH + (g + 1) * LANES] = yb.astype(BF16)


def _mixer_call(x, w_in, b_in, conv_w, conv_b, cn_g, cn_b, lconv_w, lconv_b, w_gate, b_gate, lam,
                cast_weights, cast_layers):
    ts = MIX_TS
    seq_steps = SEQ // ts
    full = lambda a: pl.BlockSpec(a.shape, lambda b, s, nd=a.ndim: (0,) * nd,
                                  pipeline_mode=pl.Buffered(1))
    cast_in, cast_out, cast_shapes = _cast_specs(
        cast_weights, cast_layers, lambda b, s: b * seq_steps + s, BATCH * seq_steps)
    return pl.pallas_call(
        _mixer_kernel,
        grid=(BATCH, seq_steps),
        in_specs=[pl.BlockSpec((1, ts, D_MODEL), lambda b, s: (b, s, 0)),
                  full(w_in), full(b_in), full(conv_w), full(conv_b), full(cn_g), full(cn_b),
                  full(lconv_w), full(lconv_b), full(w_gate), full(b_gate), full(lam)] + cast_in,
        out_specs=[pl.BlockSpec((1, ts, A_WIDTH + B_WIDTH), lambda b, s: (b, s, 0))] + cast_out,
        out_shape=[jax.ShapeDtypeStruct((BATCH, SEQ, A_WIDTH + B_WIDTH), BF16)] + cast_shapes,
        scratch_shapes=[pltpu.VMEM((A_WIDTH // LANES, CONV_HALO + ts + CONV_BLK, LANES), F32),
                        pltpu.VMEM((A_WIDTH // LANES, SUBLANES - 1, ts + CONV_BLK + PHASE_SKEW, LANES),
                                   F32),
                        pltpu.VMEM((LRU_HALO + ts, B_WIDTH), F32),
                        pltpu.VMEM((SUBLANES, B_WIDTH), F32),
                        pltpu.VMEM((ts, B_WIDTH), F32),
                        pltpu.VMEM((ts, 2 * B_WIDTH), F32),
                        pltpu.VMEM((2, ts, LANES), F32),
                        pltpu.VMEM((ts, D_MODEL), BF16),
                        pltpu.VMEM((ts, A_WIDTH), F32),
                        pltpu.VMEM((D_MODEL, IN_WIDTH), BF16)],
        compiler_params=pltpu.CompilerParams(
            dimension_semantics=("arbitrary", "arbitrary"), vmem_limit_bytes=VMEM_LIMIT),
        name="mixer0",
    )(x, w_in, b_in, conv_w, conv_b, cn_g, cn_b, lconv_w, lconv_b, w_gate, b_gate, lam,
      *cast_weights)


def _post_kernel(m_ref, x_ref, w_out_ref, g1_ref, b1_ref, wg_ref, wu_ref, wd_ref, g2_ref, b2_ref,
                 out_ref, x1_ref, xb_ref, acc_ref):
    y = _dot(m_ref[...], w_out_ref[...])
    x1 = _ln(DN_ALPHA * x_ref[...] + y, g1_ref[...], b1_ref[...])
    x1_ref[...] = x1
    xb_ref[...] = x1.astype(BF16)
    for c in range(D_FF // FF_CHUNK):
        cs = slice(c * FF_CHUNK, (c + 1) * FF_CHUNK)
        gate = _dot(xb_ref[...], wg_ref[:, cs])
        up = _dot(xb_ref[...], wu_ref[:, cs])
        act = (jax.nn.silu(gate) * up).astype(BF16)
        contrib = _dot(act, wd_ref[cs, :])
        if c == 0:
            acc_ref[...] = contrib
        else:
            acc_ref[...] += contrib
    out_ref[...] = _ln(DN_ALPHA * x1_ref[...] + acc_ref[...], g2_ref[...], b2_ref[...])


def _post_call(m, x, w_out, g1, b1, wg, wu, wd, g2, b2):
    tm = POST_TM
    rows = m.shape[0]
    const = lambda i: (0, 0)
    full = lambda a: pl.BlockSpec(a.shape, const, pipeline_mode=pl.Buffered(1))
    return pl.pallas_call(
        _post_kernel,
        grid=(rows // tm,),
        in_specs=[pl.BlockSpec((tm, D_MODEL), lambda i: (i, 0)),
                  pl.BlockSpec((tm, D_MODEL), lambda i: (i, 0)),
                  full(w_out), full(g1), full(b1), full(wg), full(wu), full(wd),
                  full(g2), full(b2)],
        out_specs=pl.BlockSpec((tm, D_MODEL), lambda i: (i, 0)),
        out_shape=jax.ShapeDtypeStruct((rows, D_MODEL), F32),
        scratch_shapes=[pltpu.VMEM((tm, D_MODEL), F32),
                        pltpu.VMEM((tm, D_MODEL), BF16),
                        pltpu.VMEM((tm, D_MODEL), F32)],
        compiler_params=pltpu.CompilerParams(
            dimension_semantics=("arbitrary",), vmem_limit_bytes=VMEM_LIMIT),
        name="post",
    )(m, x, w_out, g1, b1, wg, wu, wd, g2, b2)


def _qkv_kernel(x_ref, w_f32_ref, cos_ref, sin_ref, *rest):
    cast_in, rest = rest[:N_CAST], rest[N_CAST:]
    q_ref, k_ref, v_ref = rest[:3]
    cast_out, (w_ref,) = rest[3:3 + N_CAST], rest[3 + N_CAST:]
    _cast_slabs(cast_in, cast_out)

    @pl.when(pl.program_id(0) == 0)
    def _():
        w_ref[...] = w_f32_ref[...].astype(BF16)

    xb = x_ref[...].astype(BF16)
    cos = cos_ref[...]
    sin = sin_ref[...]
    half_dim = DIFF_HEAD_DIM // 2
    lane = lax.broadcasted_iota(jnp.int32, (1, LANES), 1)
    first_half = (lane % DIFF_HEAD_DIM) < half_dim
    for g in range(2 * QK_WIDTH // MXU_COLS):
        t2 = _dot(xb, w_ref[:, g * MXU_COLS:(g + 1) * MXU_COLS])
        for half in range(MXU_COLS // LANES):
            t = t2[:, half * LANES:(half + 1) * LANES]
            rot = jnp.where(first_half, pltpu.roll(t, LANES - half_dim, axis=1),
                            pltpu.roll(t, half_dim, axis=1))
            r = t * cos + rot * sin
            col = g * MXU_COLS + half * LANES
            if col < QK_WIDTH:
                q_ref[:, col:col + LANES] = (r * Q_SCALE).astype(BF16)
            else:
                k_ref[:, col - QK_WIDTH:col - QK_WIDTH + LANES] = r.astype(BF16)
    v_ref[...] = _dot(xb, w_ref[:, 2 * QK_WIDTH:]).astype(BF16)


def _qkv_call(x, w, cos, sin, cast_weights, cast_layers):
    tm = QKV_TM
    rows = x.shape[0]
    pos_blocks = SEQ // tm
    row_spec = pl.BlockSpec((tm, D_MODEL), lambda i: (i, 0))
    tab_spec = pl.BlockSpec((tm, LANES), lambda i: (i % pos_blocks, 0))
    out = jax.ShapeDtypeStruct((rows, QK_WIDTH), BF16)
    cast_in, cast_out, cast_shapes = _cast_specs(cast_weights, cast_layers, lambda i: i,
                                                 rows // tm)
    return pl.pallas_call(
        _qkv_kernel,
        grid=(rows // tm,),
        in_specs=[row_spec,
                  pl.BlockSpec(w.shape, lambda i: (0, 0), pipeline_mode=pl.Buffered(1)),
                  tab_spec, tab_spec] + cast_in,
        out_specs=[row_spec, row_spec, row_spec] + cast_out,
        out_shape=[out, out, out] + cast_shapes,
        scratch_shapes=[pltpu.VMEM(w.shape, BF16)],
        compiler_params=pltpu.CompilerParams(
            dimension_semantics=("arbitrary",), vmem_limit_bytes=VMEM_LIMIT),
        name="qkv",
    )(x, w, cos, sin, *cast_weights)


def _attn_kernel(lq1_ref, lk1_ref, lq2_ref, lk2_ref, g_ref, q_ref, k_ref, v_ref, o_ref,
                 qs_ref, m_ref, acc_ref, *, lambda_init):
    th = ATT_TH
    tq = 2 * th
    lane = lax.broadcasted_iota(jnp.int32, (1, LANES), 1)
    is_map1 = lane < DIFF_HEAD_DIM
    lam = (jnp.exp(jnp.sum(lq1_ref[...] * lk1_ref[...], axis=-1, keepdims=True))
           - jnp.exp(jnp.sum(lq2_ref[...] * lk2_ref[...], axis=-1, keepdims=True)) + lambda_init)
    gain = g_ref[...] * (1.0 - lambda_init)
    zero = jnp.zeros((), BF16)

    def chunk_mask(n_rows):
        row = lax.broadcasted_iota(jnp.int32, (n_rows, th), 0) % th
        return row // CHUNK >= lax.broadcasted_iota(jnp.int32, (n_rows, th), 1) // CHUNK

    diag = chunk_mask(2 * th)

    def update(h, rows, key_start, n_keys, visible):
        ks = pl.ds(key_start, n_keys)
        hs = slice(h * LANES, (h + 1) * LANES)
        s = lax.dot_general(qs_ref[h, rows, :], k_ref[0, ks, hs], (((1,), (1,)), ((), ())),
                            preferred_element_type=F32)
        if visible is not None:
            n = visible.shape[0]
            masked = jnp.where(visible, s[:n], NEG_INF)
            s = masked if n == s.shape[0] else jnp.concatenate([masked, s[n:]], axis=0)
        m_old = m_ref[h, rows, :]
        m_cur = jnp.max(s, axis=-1, keepdims=True)
        m_new = jnp.maximum(m_old, jnp.broadcast_to(m_cur, m_old.shape))
        alpha = jnp.exp2(m_old - m_new)
        e = jnp.exp2(s - jnp.concatenate([m_new] * (n_keys // LANES), axis=1)).astype(BF16)
        v_ext = jnp.concatenate([v_ref[0, ks, hs], jnp.ones((n_keys, LANES), BF16)], axis=1)
        acc_ref[h, rows, :] = (jnp.concatenate([alpha, alpha], axis=1) * acc_ref[h, rows, :]
                               + _dot(e, v_ext))
        m_ref[h, rows, :] = m_new

    all_rows = slice(0, 4 * th)
    second_half = slice(2 * th, 4 * th)

    def q_block(i, carry):
        q0 = pl.multiple_of(i * tq, tq)
        for h in range(ATT_HP):
            for half in range(2):
                q = q_ref[0, pl.ds(q0 + half * th, th), h * LANES:(h + 1) * LANES]
                qs_ref[h, (2 * half) * th:(2 * half + 1) * th, :] = jnp.where(is_map1, q, zero)
                qs_ref[h, (2 * half + 1) * th:(2 * half + 2) * th, :] = jnp.where(is_map1, zero, q)
        m_ref[...] = jnp.full(m_ref.shape, NEG_INF, F32)
        acc_ref[...] = jnp.zeros(acc_ref.shape, F32)

        def off_diag(j, c):
            for h in range(ATT_HP):
                update(h, all_rows, pl.multiple_of(j * tq, tq), tq, None)
            return c

        lax.fori_loop(0, i, off_diag, 0)
        for h in range(ATT_HP):
            update(h, all_rows, q0, th, diag)
            update(h, second_half, q0 + th, th, diag)

        for h in range(ATT_HP):
            for half in range(2):
                a1 = acc_ref[h, (2 * half) * th:(2 * half + 1) * th, :]
                a2 = acc_ref[h, (2 * half + 1) * th:(2 * half + 2) * th, :]
                o = (a1[:, :LANES] * (1.0 / a1[:, LANES:])
                     - a2[:, :LANES] * (lam * (1.0 / a2[:, LANES:])))
                o = o * lax.rsqrt(jnp.mean(o * o, axis=-1, keepdims=True) + LN_EPS) * gain
                o_ref[0, pl.ds(q0 + half * th, th), h * LANES:(h + 1) * LANES] = o.astype(BF16)
        return carry

    lax.fori_loop(0, SEQ // tq, q_block, 0)


def _attn_call(lq1, lk1, lq2, lk2, g, q, k, v, lambda_init):
    const = lambda b, h: (0, 0)
    small = lambda a: pl.BlockSpec(a.shape, const)
    head_spec = pl.BlockSpec((1, SEQ, ATT_HP * LANES), lambda b, h: (b, 0, h))
    return pl.pallas_call(
        functools.partial(_attn_kernel, lambda_init=lambda_init),
        grid=(BATCH, DIFF_HEADS // ATT_HP),
        in_specs=[small(lq1), small(lk1), small(lq2), small(lk2), small(g),
                  head_spec, head_spec, head_spec],
        out_specs=head_spec,
        out_shape=jax.ShapeDtypeStruct((BATCH, SEQ, V_WIDTH), BF16),
        scratch_shapes=[pltpu.VMEM((ATT_HP, 4 * ATT_TH, LANES), BF16),
                        pltpu.VMEM((ATT_HP, 4 * ATT_TH, LANES), F32),
                        pltpu.VMEM((ATT_HP, 4 * ATT_TH, 2 * LANES), F32)],
        compiler_params=pltpu.CompilerParams(
            dimension_semantics=("arbitrary", "arbitrary"), vmem_limit_bytes=VMEM_LIMIT),
        name="diff_attn",
    )(lq1, lk1, lq2, lk2, g, q, k, v)


def _block_diag(w):
    same_block = np.eye(LRU_BLOCKS, dtype=np.float32)[:, None, :, None]
    return (w[:, :, None, :] * same_block).reshape(B_WIDTH, B_WIDTH)


def _rope_tables():
    half_dim = DIFF_HEAD_DIM // 2
    pos = np.arange(SEQ, dtype=np.float64)
    inv_freq = ROPE_THETA ** (-np.arange(0, DIFF_HEAD_DIM, 2, dtype=np.float64) / DIFF_HEAD_DIM)
    lane = np.arange(LANES)
    ang = pos[:, None] * inv_freq[lane % half_dim][None, :]
    sign = np.where(lane % DIFF_HEAD_DIM < half_dim, -1.0, 1.0)
    return (jnp.asarray(np.cos(ang), dtype=F32), jnp.asarray(np.sin(ang) * sign[None, :], dtype=F32))


def kernel(x, even_w_in, even_b_in, even_conv_w, even_conv_b, even_cnorm_g, even_cnorm_b,
           even_lru_conv_w, even_lru_conv_b, even_w_a, even_b_a, even_w_x, even_b_x,
           even_lru_lambda, even_w_out, odd_w_qkv, odd_lambda_q1, odd_lambda_k1,
           odd_lambda_q2, odd_lambda_k2, odd_subln_g, odd_w_out, mix_ln_g, mix_ln_b,
           ffn_w_gate, ffn_w_up, ffn_w_down, ffn_ln_g, ffn_ln_b):
    row = lambda a: a.reshape(1, -1)
    rows = BATCH * SEQ

    def post(m, xres, weights, layer):
        w_out, wg, wu, wd = weights
        return _post_call(m.reshape(rows, -1), xres.reshape(rows, D_MODEL), w_out,
                          row(mix_ln_g[layer]), row(mix_ln_b[layer]), wg, wu, wd,
                          row(ffn_ln_g[layer]), row(ffn_ln_b[layer]))

    w_gate = jnp.concatenate([_block_diag(even_w_a[0]), _block_diag(even_w_x[0])], axis=1)
    b_gate = jnp.concatenate([even_b_a[0], even_b_x[0]]).reshape(1, -1)
    conv_w = jnp.broadcast_to(even_conv_w[0][:, None, :], (CONV_WIDTH, SUBLANES, A_WIDTH))
    m0, *tail0 = _mixer_call(x, even_w_in[0], row(even_b_in[0]), conv_w,
                             row(even_conv_b[0]), row(even_cnorm_g[0]), row(even_cnorm_b[0]),
                             even_lru_conv_w[0], row(even_lru_conv_b[0]), w_gate.astype(BF16),
                             b_gate, row(even_lru_lambda[0]),
                             [even_w_out, ffn_w_gate, ffn_w_up, ffn_w_down], [0, 0, 0, 0])
    x1 = post(m0, x, tail0, 0)

    lambda_init = 0.8 - 0.6 * math.exp(-0.3 * 1)
    cos, sin = _rope_tables()
    q, k, v, *tail1 = _qkv_call(x1, odd_w_qkv[0], cos, sin,
                                [odd_w_out, ffn_w_gate, ffn_w_up, ffn_w_down], [0, 1, 1, 1])
    shape3 = (BATCH, SEQ, QK_WIDTH)
    o = _attn_call(row(odd_lambda_q1[0]), row(odd_lambda_k1[0]), row(odd_lambda_q2[0]),
                   row(odd_lambda_k2[0]), row(odd_subln_g[0]),
                   q.reshape(shape3), k.reshape(shape3), v.reshape(shape3), lambda_init)
    out = post(o, x1, tail1, 1)
    return out.reshape(BATCH, SEQ, D_MODEL)
```

```python
import functools
import math

import jax
import jax.numpy as jnp
import numpy as np
from jax import lax
from jax.experimental import pallas as pl
from jax.experimental.pallas import tpu as pltpu

F32 = jnp.float32
BF16 = jnp.bfloat16

D_MODEL = 1024
BATCH = 8
SEQ = 2048
DEPTH = 2
CHUNK = 64
A_WIDTH = 512
B_WIDTH = 512
CONV_WIDTH = 31
LRU_BLOCKS = 8
LRU_CONV_WIDTH = 4
LRU_C = 8.0
IN_WIDTH = 2 * A_WIDTH + 2 * B_WIDTH
DIFF_HEADS = 8
DIFF_HEAD_DIM = 64
QK_WIDTH = 1024
V_WIDTH = 1024
ROPE_THETA = 10000.0
D_FF = 2816
LN_EPS = 1e-5
DN_ALPHA = (2 * DEPTH) ** 0.25
NEG_INF = -1e30
Q_SCALE = DIFF_HEAD_DIM ** -0.5 * math.log2(math.e)

LANES = 128
SUBLANES = 8
BF16_SUBLANES = 16
MXU_COLS = 256
VMEM_LIMIT = 56 * 1024 * 1024

MIX_TS = 512
CONV_HALO = 32
CONV_ROWS = 32
CONV_BLK = 64
LRU_HALO = 8
POST_TM = 1024
FF_CHUNK = 256
QKV_TM = 1024
ATT_HP = 8
ATT_TH = 256


def _ln(x, g, b):
    mu = jnp.mean(x, axis=-1, keepdims=True)
    xc = x - mu
    var = jnp.mean(xc * xc, axis=-1, keepdims=True)
    return xc * lax.rsqrt(var + LN_EPS) * g + b


def _dot(a, b):
    return jnp.dot(a, b, preferred_element_type=F32)


N_CAST = 4


def _cast_slabs(cast_in, cast_out):
    for src, dst in zip(cast_in, cast_out):
        dst[...] = src[...].astype(BF16)


def _cast_specs(weights, layers, step_of, n_steps):
    in_specs, out_specs, out_shapes = [], [], []
    for w, layer in zip(weights, layers):
        _, rows, cols = w.shape
        slab = rows // n_steps
        per_slab = 1
        while (slab * per_slab) % BF16_SUBLANES:
            per_slab *= 2
        slab *= per_slab
        in_specs.append(pl.BlockSpec(
            (None, slab, cols), lambda *g, l=layer, p=per_slab: (l, step_of(*g) // p, 0)))
        out_specs.append(pl.BlockSpec(
            (slab, cols), lambda *g, p=per_slab: (step_of(*g) // p, 0)))
        out_shapes.append(jax.ShapeDtypeStruct((rows, cols), BF16))
    return in_specs, out_specs, out_shapes


def _mixer_kernel(x_ref, w_in_f32_ref, b_in_ref, conv_w_ref, conv_b_ref, cn_g_ref, cn_b_ref,
                  lconv_w_ref, lconv_b_ref, w_gate_ref, b_gate_ref, lam_ref, *rest):
    cast_in, rest = rest[:N_CAST], rest[N_CAST:]
    out_ref, rest = rest[0], rest[1:]
    cast_out, rest = rest[:N_CAST], rest[N_CAST:]
    (a_ext, a_sh, b_ext, h_carry, bgate_ref, gates_ref, au_ref, xb_ref, conv_ref,
     w_in_ref) = rest
    ts = MIX_TS
    s = pl.program_id(1)

    n_groups = A_WIDTH // LANES
    _cast_slabs(cast_in, cast_out)

    @pl.when((pl.program_id(0) == 0) & (s == 0))
    def _():
        w_in_ref[...] = w_in_f32_ref[...].astype(BF16)

    @pl.when(s == 0)
    def _():
        a_ext[:, 0:CONV_HALO, :] = jnp.zeros((n_groups, CONV_HALO, LANES), F32)
        a_ext[:, CONV_HALO + ts:, :] = jnp.zeros((n_groups, CONV_BLK, LANES), F32)
        b_ext[0:LRU_HALO, :] = jnp.zeros((LRU_HALO, B_WIDTH), F32)
        h_carry[...] = jnp.zeros_like(h_carry)

    xb_ref[...] = x_ref[0].astype(BF16)
    ha = _dot(xb_ref[...], w_in_ref[:, 0:2 * A_WIDTH]) + b_in_ref[:, 0:2 * A_WIDTH]
    glu = ha[:, 0:A_WIDTH] * jax.nn.sigmoid(ha[:, A_WIDTH:])
    for g in range(n_groups):
        a_ext[g, CONV_HALO:CONV_HALO + ts, :] = glu[:, g * LANES:(g + 1) * LANES]

    off = CONV_HALO - (CONV_WIDTH - 1)

    for g in range(n_groups):
        def shift_body(i, carry, g=g):
            base = pl.multiple_of(i * CONV_BLK, CONV_BLK)
            win = a_ext[g, pl.ds(base, CONV_BLK + SUBLANES), :]
            for r in range(1, SUBLANES):
                rolled = pltpu.roll(win, CONV_BLK + SUBLANES - r, axis=0)
                a_sh[r - 1, g, pl.ds(base, CONV_BLK), :] = rolled[0:CONV_BLK, :]
            return carry

        lax.fori_loop(0, (ts + CONV_BLK) // CONV_BLK, shift_body, 0)

    for g in range(n_groups):
        sl = slice(g * LANES, (g + 1) * LANES)
        taps = [conv_w_ref[j, :, sl] for j in range(CONV_WIDTH)]
        bias = conv_b_ref[:, sl]

        def conv_body(i, carry, g=g, sl=sl, taps=taps, bias=bias):
            base = pl.multiple_of(i * CONV_BLK, CONV_BLK)
            acc = jnp.broadcast_to(bias, (CONV_BLK, LANES))
            for j in range(CONV_WIDTH):
                q, r = divmod(off + j, SUBLANES)
                rows = pl.ds(base + q * SUBLANES, CONV_BLK)
                win = a_ext[g, rows, :] if r == 0 else a_sh[r - 1, g, rows, :]
                acc = acc + (win.reshape(-1, SUBLANES, LANES) * taps[j][None]).reshape(CONV_BLK, LANES)
            conv_ref[pl.ds(base, CONV_BLK), sl] = acc
            return carry

        lax.fori_loop(0, ts // CONV_BLK, conv_body, 0, unroll=2)
    a_ext[:, 0:CONV_HALO, :] = a_ext[:, ts:ts + CONV_HALO, :]

    def norm_body(i, carry):
        base = pl.multiple_of(i * CONV_ROWS, CONV_ROWS)
        ya = jax.nn.silu(_ln(conv_ref[pl.ds(base, CONV_ROWS), :], cn_g_ref[...], cn_b_ref[...]))
        out_ref[0, pl.ds(base, CONV_ROWS), 0:A_WIDTH] = ya.astype(BF16)
        return carry

    lax.fori_loop(0, ts // CONV_ROWS, norm_body, 0, unroll=True)

    rec0 = 2 * A_WIDTH + B_WIDTH
    b_ext[LRU_HALO:LRU_HALO + ts, :] = _dot(xb_ref[...], w_in_ref[:, rec0:]) + b_in_ref[:, rec0:]
    loff = LRU_HALO - (LRU_CONV_WIDTH - 1)
    xc = jnp.broadcast_to(lconv_b_ref[...], (ts, B_WIDTH))
    for j in range(LRU_CONV_WIDTH):
        xc = xc + lconv_w_ref[j:j + 1, :] * b_ext[loff + j:loff + j + ts, :]
    b_ext[0:LRU_HALO, :] = b_ext[ts:ts + LRU_HALO, :]
    xcb = xc.astype(BF16)
    for k in range(2 * B_WIDTH // MXU_COLS):
        kin = (k % (B_WIDTH // MXU_COLS)) * MXU_COLS
        cs = slice(k * MXU_COLS, (k + 1) * MXU_COLS)
        gates_ref[:, cs] = (_dot(xcb[:, kin:kin + MXU_COLS], w_gate_ref[kin:kin + MXU_COLS, cs])
                            + b_gate_ref[:, cs])
    b_ext[LRU_HALO:LRU_HALO + ts, :] = xc
    bgate_ref[...] = _dot(xb_ref[...], w_in_ref[:, 2 * A_WIDTH:rec0]) + b_in_ref[:, 2 * A_WIDTH:rec0]

    def scan_steps(a_cum, u_cum, axis, steps):
        idx = lax.broadcasted_iota(jnp.int32, a_cum.shape, axis)
        for step in steps:
            keep = idx >= step
            a_prev = jnp.where(keep, pltpu.roll(a_cum, step, axis=axis), 1.0)
            u_prev = jnp.where(keep, pltpu.roll(u_cum, step, axis=axis), 0.0)
            u_cum = a_cum * u_prev + u_cum
            a_cum = a_cum * a_prev
        return a_cum, u_cum

    nblk = ts // SUBLANES
    for g in range(B_WIDTH // LANES):
        sl = slice(g * LANES, (g + 1) * LANES)
        xg = b_ext[LRU_HALO:LRU_HALO + ts, sl]
        gate_r = jax.nn.sigmoid(gates_ref[:, sl])
        gate_i = jax.nn.sigmoid(gates_ref[:, B_WIDTH + g * LANES:B_WIDTH + (g + 1) * LANES])
        lam = lam_ref[:, sl]
        neg = -lam
        softplus = jnp.maximum(neg, 0.0) + jnp.log1p(jnp.exp(-jnp.abs(neg)))
        log_a = (-LRU_C * gate_r) * softplus
        th = jnp.tanh(log_a)
        y = -2.0 * th
        scale = jnp.where(y > 0.0, y * lax.rsqrt(y), 0.0) * lax.rsqrt(1.0 - th)
        a_blk, u_blk = scan_steps(jnp.exp(log_a).reshape(nblk, SUBLANES, LANES),
                                  (scale * (gate_i * xg)).reshape(nblk, SUBLANES, LANES),
                                  1, (1, 2, 4))
        au_ref[0] = a_blk.reshape(ts, LANES)
        au_ref[1] = u_blk.reshape(ts, LANES)
        last = pl.ds(SUBLANES - 1, nblk, stride=SUBLANES)
        a_end, u_end = scan_steps(au_ref[0, last, :], au_ref[1, last, :], 0,
                                  [1 << k for k in range(nblk.bit_length() - 1)])
        h_prev = h_carry[0:1, sl]
        h_end = a_end * h_prev + u_end
        h_carry[:, sl] = jnp.broadcast_to(h_end[nblk - 1:nblk, :], (SUBLANES, LANES))
        row0 = lax.broadcasted_iota(jnp.int32, (nblk, LANES), 0) == 0
        h_in = jnp.where(row0, h_prev, pltpu.roll(h_end, 1, axis=0))
        hg = (a_blk * h_in[:, None, :] + u_blk).reshape(ts, LANES)
        yb = hg * jax.nn.gelu(bgate_ref[:, sl])
        out_ref[0, :, A_WIDTH + g * LANES:A_WIDTH + (g + 1) * LANES] = yb.astype(BF16)


def _mixer_call(x, w_in, b_in, conv_w, conv_b, cn_g, cn_b, lconv_w, lconv_b, w_gate, b_gate, lam,
                cast_weights, cast_layers):
    ts = MIX_TS
    seq_steps = SEQ // ts
    full = lambda a: pl.BlockSpec(a.shape, lambda b, s, nd=a.ndim: (0,) * nd,
                                  pipeline_mode=pl.Buffered(1))
    cast_in, cast_out, cast_shapes = _cast_specs(
        cast_weights, cast_layers, lambda b, s: b * seq_steps + s, BATCH * seq_steps)
    return pl.pallas_call(
        _mixer_kernel,
        grid=(BATCH, seq_steps),
        in_specs=[pl.BlockSpec((1, ts, D_MODEL), lambda b, s: (b, s, 0)),
                  full(w_in), full(b_in), full(conv_w), full(conv_b), full(cn_g), full(cn_b),
                  full(lconv_w), full(lconv_b), full(w_gate), full(b_gate), full(lam)] + cast_in,
        out_specs=[pl.BlockSpec((1, ts, A_WIDTH + B_WIDTH), lambda b, s: (b, s, 0))] + cast_out,
        out_shape=[jax.ShapeDtypeStruct((BATCH, SEQ, A_WIDTH + B_WIDTH), BF16)] + cast_shapes,
        scratch_shapes=[pltpu.VMEM((A_WIDTH // LANES, CONV_HALO + ts + CONV_BLK, LANES), F32),
                        pltpu.VMEM((SUBLANES - 1, A_WIDTH // LANES, ts + CONV_BLK, LANES), F32),
                        pltpu.VMEM((LRU_HALO + ts, B_WIDTH), F32),
                        pltpu.VMEM((SUBLANES, B_WIDTH), F32),
                        pltpu.VMEM((ts, B_WIDTH), F32),
                        pltpu.VMEM((ts, 2 * B_WIDTH), F32),
                        pltpu.VMEM((2, ts, LANES), F32),
                        pltpu.VMEM((ts, D_MODEL), BF16),
                        pltpu.VMEM((ts, A_WIDTH), F32),
                        pltpu.VMEM((D_MODEL, IN_WIDTH), BF16)],
        compiler_params=pltpu.CompilerParams(
            dimension_semantics=("arbitrary", "arbitrary"), vmem_limit_bytes=VMEM_LIMIT),
        name="mixer0",
    )(x, w_in, b_in, conv_w, conv_b, cn_g, cn_b, lconv_w, lconv_b, w_gate, b_gate, lam,
      *cast_weights)


def _post_kernel(m_ref, x_ref, w_out_ref, g1_ref, b1_ref, wg_ref, wu_ref, wd_ref, g2_ref, b2_ref,
                 out_ref, x1_ref, xb_ref, acc_ref):
    y = _dot(m_ref[...], w_out_ref[...])
    x1 = _ln(DN_ALPHA * x_ref[...] + y, g1_ref[...], b1_ref[...])
    x1_ref[...] = x1
    xb_ref[...] = x1.astype(BF16)
    for c in range(D_FF // FF_CHUNK):
        cs = slice(c * FF_CHUNK, (c + 1) * FF_CHUNK)
        gate = _dot(xb_ref[...], wg_ref[:, cs])
        up = _dot(xb_ref[...], wu_ref[:, cs])
        act = (jax.nn.silu(gate) * up).astype(BF16)
        contrib = _dot(act, wd_ref[cs, :])
        if c == 0:
            acc_ref[...] = contrib
        else:
            acc_ref[...] += contrib
    out_ref[...] = _ln(DN_ALPHA * x1_ref[...] + acc_ref[...], g2_ref[...], b2_ref[...])


def _post_call(m, x, w_out, g1, b1, wg, wu, wd, g2, b2):
    tm = POST_TM
    rows = m.shape[0]
    const = lambda i: (0, 0)
    full = lambda a: pl.BlockSpec(a.shape, const, pipeline_mode=pl.Buffered(1))
    return pl.pallas_call(
        _post_kernel,
        grid=(rows // tm,),
        in_specs=[pl.BlockSpec((tm, D_MODEL), lambda i: (i, 0)),
                  pl.BlockSpec((tm, D_MODEL), lambda i: (i, 0)),
                  full(w_out), full(g1), full(b1), full(wg), full(wu), full(wd),
                  full(g2), full(b2)],
        out_specs=pl.BlockSpec((tm, D_MODEL), lambda i: (i, 0)),
        out_shape=jax.ShapeDtypeStruct((rows, D_MODEL), F32),
        scratch_shapes=[pltpu.VMEM((tm, D_MODEL), F32),
                        pltpu.VMEM((tm, D_MODEL), BF16),
                        pltpu.VMEM((tm, D_MODEL), F32)],
        compiler_params=pltpu.CompilerParams(
            dimension_semantics=("arbitrary",), vmem_limit_bytes=VMEM_LIMIT),
        name="post",
    )(m, x, w_out, g1, b1, wg, wu, wd, g2, b2)


def _qkv_kernel(x_ref, w_f32_ref, cos_ref, sin_ref, *rest):
    cast_in, rest = rest[:N_CAST], rest[N_CAST:]
    q_ref, k_ref, v_ref = rest[:3]
    cast_out, (w_ref,) = rest[3:3 + N_CAST], rest[3 + N_CAST:]
    _cast_slabs(cast_in, cast_out)

    @pl.when(pl.program_id(0) == 0)
    def _():
        w_ref[...] = w_f32_ref[...].astype(BF16)

    xb = x_ref[...].astype(BF16)
    cos = cos_ref[...]
    sin = sin_ref[...]
    half_dim = DIFF_HEAD_DIM // 2
    lane = lax.broadcasted_iota(jnp.int32, (1, LANES), 1)
    first_half = (lane % DIFF_HEAD_DIM) < half_dim
    for g in range(2 * QK_WIDTH // MXU_COLS):
        t2 = _dot(xb, w_ref[:, g * MXU_COLS:(g + 1) * MXU_COLS])
        for half in range(MXU_COLS // LANES):
            t = t2[:, half * LANES:(half + 1) * LANES]
            rot = jnp.where(first_half, pltpu.roll(t, LANES - half_dim, axis=1),
                            pltpu.roll(t, half_dim, axis=1))
            r = t * cos + rot * sin
            col = g * MXU_COLS + half * LANES
            if col < QK_WIDTH:
                q_ref[:, col:col + LANES] = (r * Q_SCALE).astype(BF16)
            else:
                k_ref[:, col - QK_WIDTH:col - QK_WIDTH + LANES] = r.astype(BF16)
    v_ref[...] = _dot(xb, w_ref[:, 2 * QK_WIDTH:]).astype(BF16)


def _qkv_call(x, w, cos, sin, cast_weights, cast_layers):
    tm = QKV_TM
    rows = x.shape[0]
    pos_blocks = SEQ // tm
    row_spec = pl.BlockSpec((tm, D_MODEL), lambda i: (i, 0))
    tab_spec = pl.BlockSpec((tm, LANES), lambda i: (i % pos_blocks, 0))
    out = jax.ShapeDtypeStruct((rows, QK_WIDTH), BF16)
    cast_in, cast_out, cast_shapes = _cast_specs(cast_weights, cast_layers, lambda i: i,
                                                 rows // tm)
    return pl.pallas_call(
        _qkv_kernel,
        grid=(rows // tm,),
        in_specs=[row_spec,
                  pl.BlockSpec(w.shape, lambda i: (0, 0), pipeline_mode=pl.Buffered(1)),
                  tab_spec, tab_spec] + cast_in,
        out_specs=[row_spec, row_spec, row_spec] + cast_out,
        out_shape=[out, out, out] + cast_shapes,
        scratch_shapes=[pltpu.VMEM(w.shape, BF16)],
        compiler_params=pltpu.CompilerParams(
            dimension_semantics=("arbitrary",), vmem_limit_bytes=VMEM_LIMIT),
        name="qkv",
    )(x, w, cos, sin, *cast_weights)


def _attn_kernel(lq1_ref, lk1_ref, lq2_ref, lk2_ref, g_ref, q_ref, k_ref, v_ref, o_ref,
                 qs_ref, m_ref, acc_ref, *, lambda_init):
    th = ATT_TH
    tq = 2 * th
    lane = lax.broadcasted_iota(jnp.int32, (1, LANES), 1)
    is_map1 = lane < DIFF_HEAD_DIM
    lam = (jnp.exp(jnp.sum(lq1_ref[...] * lk1_ref[...], axis=-1, keepdims=True))
           - jnp.exp(jnp.sum(lq2_ref[...] * lk2_ref[...], axis=-1, keepdims=True)) + lambda_init)
    gain = g_ref[...] * (1.0 - lambda_init)
    zero = jnp.zeros((), BF16)

    def chunk_mask(n_rows):
        row = lax.broadcasted_iota(jnp.int32, (n_rows, th), 0) % th
        return row // CHUNK >= lax.broadcasted_iota(jnp.int32, (n_rows, th), 1) // CHUNK

    diag = chunk_mask(2 * th)

    def update(h, rows, key_start, n_keys, visible):
        ks = pl.ds(key_start, n_keys)
        hs = slice(h * LANES, (h + 1) * LANES)
        s = lax.dot_general(qs_ref[h, rows, :], k_ref[0, ks, hs], (((1,), (1,)), ((), ())),
                            preferred_element_type=F32)
        if visible is not None:
            n = visible.shape[0]
            masked = jnp.where(visible, s[:n], NEG_INF)
            s = masked if n == s.shape[0] else jnp.concatenate([masked, s[n:]], axis=0)
        m_old = m_ref[h, rows, :]
        m_cur = jnp.max(s, axis=-1, keepdims=True)
        m_new = jnp.maximum(m_old, jnp.broadcast_to(m_cur, m_old.shape))
        alpha = jnp.exp2(m_old - m_new)
        e = jnp.exp2(s - jnp.concatenate([m_new] * (n_keys // LANES), axis=1)).astype(BF16)
        v_ext = jnp.concatenate([v_ref[0, ks, hs], jnp.ones((n_keys, LANES), BF16)], axis=1)
        acc_ref[h, rows, :] = (jnp.concatenate([alpha, alpha], axis=1) * acc_ref[h, rows, :]
                               + _dot(e, v_ext))
        m_ref[h, rows, :] = m_new

    all_rows = slice(0, 4 * th)
    second_half = slice(2 * th, 4 * th)

    def q_block(i, carry):
        q0 = pl.multiple_of(i * tq, tq)
        for h in range(ATT_HP):
            for half in range(2):
                q = q_ref[0, pl.ds(q0 + half * th, th), h * LANES:(h + 1) * LANES]
                qs_ref[h, (2 * half) * th:(2 * half + 1) * th, :] = jnp.where(is_map1, q, zero)
                qs_ref[h, (2 * half + 1) * th:(2 * half + 2) * th, :] = jnp.where(is_map1, zero, q)
        m_ref[...] = jnp.full(m_ref.shape, NEG_INF, F32)
        acc_ref[...] = jnp.zeros(acc_ref.shape, F32)

        def off_diag(j, c):
            for h in range(ATT_HP):
                update(h, all_rows, pl.multiple_of(j * tq, tq), tq, None)
            return c

        lax.fori_loop(0, i, off_diag, 0)
        for h in range(ATT_HP):
            update(h, all_rows, q0, th, diag)
            update(h, second_half, q0 + th, th, diag)

        for h in range(ATT_HP):
            for half in range(2):
                a1 = acc_ref[h, (2 * half) * th:(2 * half + 1) * th, :]
                a2 = acc_ref[h, (2 * half + 1) * th:(2 * half + 2) * th, :]
                o = (a1[:, :LANES] * (1.0 / a1[:, LANES:])
                     - a2[:, :LANES] * (lam * (1.0 / a2[:, LANES:])))
                o = o * lax.rsqrt(jnp.mean(o * o, axis=-1, keepdims=True) + LN_EPS) * gain
                o_ref[0, pl.ds(q0 + half * th, th), h * LANES:(h + 1) * LANES] = o.astype(BF16)
        return carry

    lax.fori_loop(0, SEQ // tq, q_block, 0)


def _attn_call(lq1, lk1, lq2, lk2, g, q, k, v, lambda_init):
    const = lambda b, h: (0, 0)
    small = lambda a: pl.BlockSpec(a.shape, const)
    head_spec = pl.BlockSpec((1, SEQ, ATT_HP * LANES), lambda b, h: (b, 0, h))
    return pl.pallas_call(
        functools.partial(_attn_kernel, lambda_init=lambda_init),
        grid=(BATCH, DIFF_HEADS // ATT_HP),
        in_specs=[small(lq1), small(lk1), small(lq2), small(lk2), small(g),
                  head_spec, head_spec, head_spec],
        out_specs=head_spec,
        out_shape=jax.ShapeDtypeStruct((BATCH, SEQ, V_WIDTH), BF16),
        scratch_shapes=[pltpu.VMEM((ATT_HP, 4 * ATT_TH, LANES), BF16),
                        pltpu.VMEM((ATT_HP, 4 * ATT_TH, LANES), F32),
                        pltpu.VMEM((ATT_HP, 4 * ATT_TH, 2 * LANES), F32)],
        compiler_params=pltpu.CompilerParams(
            dimension_semantics=("arbitrary", "arbitrary"), vmem_limit_bytes=VMEM_LIMIT),
        name="diff_attn",
    )(lq1, lk1, lq2, lk2, g, q, k, v)


def _block_diag(w):
    same_block = np.eye(LRU_BLOCKS, dtype=np.float32)[:, None, :, None]
    return (w[:, :, None, :] * same_block).reshape(B_WIDTH, B_WIDTH)


def _rope_tables():
    half_dim = DIFF_HEAD_DIM // 2
    pos = np.arange(SEQ, dtype=np.float64)
    inv_freq = ROPE_THETA ** (-np.arange(0, DIFF_HEAD_DIM, 2, dtype=np.float64) / DIFF_HEAD_DIM)
    lane = np.arange(LANES)
    ang = pos[:, None] * inv_freq[lane % half_dim][None, :]
    sign = np.where(lane % DIFF_HEAD_DIM < half_dim, -1.0, 1.0)
    return (jnp.asarray(np.cos(ang), dtype=F32), jnp.asarray(np.sin(ang) * sign[None, :], dtype=F32))


def kernel(x, even_w_in, even_b_in, even_conv_w, even_conv_b, even_cnorm_g, even_cnorm_b,
           even_lru_conv_w, even_lru_conv_b, even_w_a, even_b_a, even_w_x, even_b_x,
           even_lru_lambda, even_w_out, odd_w_qkv, odd_lambda_q1, odd_lambda_k1,
           odd_lambda_q2, odd_lambda_k2, odd_subln_g, odd_w_out, mix_ln_g, mix_ln_b,
           ffn_w_gate, ffn_w_up, ffn_w_down, ffn_ln_g, ffn_ln_b):
    row = lambda a: a.reshape(1, -1)
    rows = BATCH * SEQ

    def post(m, xres, weights, layer):
        w_out, wg, wu, wd = weights
        return _post_call(m.reshape(rows, -1), xres.reshape(rows, D_MODEL), w_out,
                          row(mix_ln_g[layer]), row(mix_ln_b[layer]), wg, wu, wd,
                          row(ffn_ln_g[layer]), row(ffn_ln_b[layer]))

    w_gate = jnp.concatenate([_block_diag(even_w_a[0]), _block_diag(even_w_x[0])], axis=1)
    b_gate = jnp.concatenate([even_b_a[0], even_b_x[0]]).reshape(1, -1)
    conv_w = jnp.broadcast_to(even_conv_w[0][:, None, :], (CONV_WIDTH, SUBLANES, A_WIDTH))
    m0, *tail0 = _mixer_call(x, even_w_in[0], row(even_b_in[0]), conv_w,
                             row(even_conv_b[0]), row(even_cnorm_g[0]), row(even_cnorm_b[0]),
                             even_lru_conv_w[0], row(even_lru_conv_b[0]), w_gate.astype(BF16),
                             b_gate, row(even_lru_lambda[0]),
                             [even_w_out, ffn_w_gate, ffn_w_up, ffn_w_down], [0, 0, 0, 0])
    x1 = post(m0, x, tail0, 0)

    lambda_init = 0.8 - 0.6 * math.exp(-0.3 * 1)
    cos, sin = _rope_tables()
    q, k, v, *tail1 = _qkv_call(x1, odd_w_qkv[0], cos, sin,
                                [odd_w_out, ffn_w_gate, ffn_w_up, ffn_w_down], [0, 1, 1, 1])
    shape3 = (BATCH, SEQ, QK_WIDTH)
    o = _attn_call(row(odd_lambda_q1[0]), row(odd_lambda_k1[0]), row(odd_lambda_q2[0]),
                   row(odd_lambda_k2[0]), row(odd_subln_g[0]),
                   q.reshape(shape3), k.reshape(shape3), v.reshape(shape3), lambda_init)
    out = post(o, x1, tail1, 1)
    return out.reshape(BATCH, SEQ, D_MODEL)
```

```python
import functools
import math

import jax
import jax.numpy as jnp
import numpy as np
from jax import lax
from jax.experimental import pallas as pl
from jax.experimental.pallas import tpu as pltpu

F32 = jnp.float32
BF16 = jnp.bfloat16

D_MODEL = 1024
BATCH = 8
SEQ = 2048
DEPTH = 2
CHUNK = 64
A_WIDTH = 512
B_WIDTH = 512
CONV_WIDTH = 31
LRU_BLOCKS = 8
LRU_CONV_WIDTH = 4
LRU_C = 8.0
IN_WIDTH = 2 * A_WIDTH + 2 * B_WIDTH
DIFF_HEADS = 8
DIFF_HEAD_DIM = 64
QK_WIDTH = 1024
V_WIDTH = 1024
ROPE_THETA = 10000.0
D_FF = 2816
LN_EPS = 1e-5
DN_ALPHA = (2 * DEPTH) ** 0.25
NEG_INF = -1e30
Q_SCALE = DIFF_HEAD_DIM ** -0.5 * math.log2(math.e)

LANES = 128
SUBLANES = 8
BF16_SUBLANES = 16
MXU_COLS = 256
VMEM_LIMIT = 56 * 1024 * 1024

MIX_TS = 512
CONV_HALO = 32
CONV_ROWS = 32
CONV_BLK = 64
LRU_HALO = 8
POST_TM = 1024
FF_CHUNK = 256
QKV_TM = 1024
ATT_HP = 8
ATT_TH = 256


def _ln(x, g, b):
    mu = jnp.mean(x, axis=-1, keepdims=True)
    xc = x - mu
    var = jnp.mean(xc * xc, axis=-1, keepdims=True)
    return xc * lax.rsqrt(var + LN_EPS) * g + b


def _dot(a, b):
    return jnp.dot(a, b, preferred_element_type=F32)


N_CAST = 4


def _cast_slabs(cast_in, cast_out):
    for src, dst in zip(cast_in, cast_out):
        dst[...] = src[...].astype(BF16)


def _cast_specs(weights, layers, step_of, n_steps):
    in_specs, out_specs, out_shapes = [], [], []
    for w, layer in zip(weights, layers):
        _, rows, cols = w.shape
        slab = rows // n_steps
        per_slab = 1
        while (slab * per_slab) % BF16_SUBLANES:
            per_slab *= 2
        slab *= per_slab
        in_specs.append(pl.BlockSpec(
            (None, slab, cols), lambda *g, l=layer, p=per_slab: (l, step_of(*g) // p, 0)))
        out_specs.append(pl.BlockSpec(
            (slab, cols), lambda *g, p=per_slab: (step_of(*g) // p, 0)))
        out_shapes.append(jax.ShapeDtypeStruct((rows, cols), BF16))
    return in_specs, out_specs, out_shapes


def _mixer_kernel(x_ref, w_in_f32_ref, b_in_ref, conv_w_ref, conv_b_ref, cn_g_ref, cn_b_ref,
                  lconv_w_ref, lconv_b_ref, w_gate_ref, b_gate_ref, lam_ref, *rest):
    cast_in, rest = rest[:N_CAST], rest[N_CAST:]
    out_ref, rest = rest[0], rest[1:]
    cast_out, rest = rest[:N_CAST], rest[N_CAST:]
    (a_ext, a_sh, b_ext, h_carry, bgate_ref, gates_ref, au_ref, xb_ref, conv_ref,
     w_in_ref) = rest
    ts = MIX_TS
    s = pl.program_id(1)

    n_groups = A_WIDTH // LANES
    _cast_slabs(cast_in, cast_out)

    @pl.when((pl.program_id(0) == 0) & (s == 0))
    def _():
        w_in_ref[...] = w_in_f32_ref[...].astype(BF16)

    @pl.when(s == 0)
    def _():
        a_ext[:, 0:CONV_HALO, :] = jnp.zeros((n_groups, CONV_HALO, LANES), F32)
        a_ext[:, CONV_HALO + ts:, :] = jnp.zeros((n_groups, CONV_BLK, LANES), F32)
        b_ext[0:LRU_HALO, :] = jnp.zeros((LRU_HALO, B_WIDTH), F32)
        h_carry[...] = jnp.zeros_like(h_carry)

    xb_ref[...] = x_ref[0].astype(BF16)
    ha = _dot(xb_ref[...], w_in_ref[:, 0:2 * A_WIDTH]) + b_in_ref[:, 0:2 * A_WIDTH]
    glu = ha[:, 0:A_WIDTH] * jax.nn.sigmoid(ha[:, A_WIDTH:])
    for g in range(n_groups):
        a_ext[g, CONV_HALO:CONV_HALO + ts, :] = glu[:, g * LANES:(g + 1) * LANES]

    off = CONV_HALO - (CONV_WIDTH - 1)

    for g in range(n_groups):
        def shift_body(i, carry, g=g):
            base = pl.multiple_of(i * CONV_BLK, CONV_BLK)
            win = a_ext[g, pl.ds(base, CONV_BLK + SUBLANES), :]
            for r in range(1, SUBLANES):
                rolled = pltpu.roll(win, CONV_BLK + SUBLANES - r, axis=0)
                a_sh[r - 1, g, pl.ds(base, CONV_BLK), :] = rolled[0:CONV_BLK, :]
            return carry

        lax.fori_loop(0, (ts + CONV_BLK) // CONV_BLK, shift_body, 0)

    for g in range(n_groups):
        sl = slice(g * LANES, (g + 1) * LANES)
        taps = [conv_w_ref[j, :, sl] for j in range(CONV_WIDTH)]
        bias = conv_b_ref[:, sl]

        def conv_body(i, carry, g=g, sl=sl, taps=taps, bias=bias):
            base = pl.multiple_of(i * CONV_BLK, CONV_BLK)
            acc = jnp.broadcast_to(bias, (CONV_BLK, LANES))
            for j in range(CONV_WIDTH):
                q, r = divmod(off + j, SUBLANES)
                rows = pl.ds(base + q * SUBLANES, CONV_BLK)
                win = a_ext[g, rows, :] if r == 0 else a_sh[r - 1, g, rows, :]
                acc = acc + (win.reshape(-1, SUBLANES, LANES) * taps[j][None]).reshape(CONV_BLK, LANES)
            conv_ref[pl.ds(base, CONV_BLK), sl] = acc
            return carry

        lax.fori_loop(0, ts // CONV_BLK, conv_body, 0, unroll=2)
    a_ext[:, 0:CONV_HALO, :] = a_ext[:, ts:ts + CONV_HALO, :]

    def norm_body(i, carry):
        base = pl.multiple_of(i * CONV_ROWS, CONV_ROWS)
        ya = jax.nn.silu(_ln(conv_ref[pl.ds(base, CONV_ROWS), :], cn_g_ref[...], cn_b_ref[...]))
        out_ref[0, pl.ds(base, CONV_ROWS), 0:A_WIDTH] = ya.astype(BF16)
        return carry

    lax.fori_loop(0, ts // CONV_ROWS, norm_body, 0, unroll=True)

    rec0 = 2 * A_WIDTH + B_WIDTH
    b_ext[LRU_HALO:LRU_HALO + ts, :] = _dot(xb_ref[...], w_in_ref[:, rec0:]) + b_in_ref[:, rec0:]
    loff = LRU_HALO - (LRU_CONV_WIDTH - 1)
    xcs = []
    for blk in range(B_WIDTH // MXU_COLS):
        bs = slice(blk * MXU_COLS, (blk + 1) * MXU_COLS)
        xc = jnp.broadcast_to(lconv_b_ref[:, bs], (ts, MXU_COLS))
        for j in range(LRU_CONV_WIDTH):
            xc = xc + lconv_w_ref[j:j + 1, bs] * b_ext[loff + j:loff + j + ts, bs]
        xcs.append(xc)
        xcb = xc.astype(BF16)
        for gate in range(2):
            cs = slice(gate * B_WIDTH + blk * MXU_COLS, gate * B_WIDTH + (blk + 1) * MXU_COLS)
            gates_ref[:, cs] = _dot(xcb, w_gate_ref[bs, cs]) + b_gate_ref[:, cs]
    b_ext[0:LRU_HALO, :] = b_ext[ts:ts + LRU_HALO, :]
    for blk, xc in enumerate(xcs):
        b_ext[LRU_HALO:LRU_HALO + ts, blk * MXU_COLS:(blk + 1) * MXU_COLS] = xc
    bgate_ref[...] = _dot(xb_ref[...], w_in_ref[:, 2 * A_WIDTH:rec0]) + b_in_ref[:, 2 * A_WIDTH:rec0]

    def scan_steps(a_cum, u_cum, axis, steps):
        idx = lax.broadcasted_iota(jnp.int32, a_cum.shape, axis)
        for step in steps:
            keep = idx >= step
            a_prev = jnp.where(keep, pltpu.roll(a_cum, step, axis=axis), 1.0)
            u_prev = jnp.where(keep, pltpu.roll(u_cum, step, axis=axis), 0.0)
            u_cum = a_cum * u_prev + u_cum
            a_cum = a_cum * a_prev
        return a_cum, u_cum

    nblk = ts // SUBLANES
    for g in range(B_WIDTH // LANES):
        sl = slice(g * LANES, (g + 1) * LANES)
        xg = b_ext[LRU_HALO:LRU_HALO + ts, sl]
        gate_r = jax.nn.sigmoid(gates_ref[:, sl])
        gate_i = jax.nn.sigmoid(gates_ref[:, B_WIDTH + g * LANES:B_WIDTH + (g + 1) * LANES])
        lam = lam_ref[:, sl]
        neg = -lam
        softplus = jnp.maximum(neg, 0.0) + jnp.log1p(jnp.exp(-jnp.abs(neg)))
        log_a = (-LRU_C * gate_r) * softplus
        th = jnp.tanh(log_a)
        y = -2.0 * th
        scale = jnp.where(y > 0.0, y * lax.rsqrt(y), 0.0) * lax.rsqrt(1.0 - th)
        a_blk, u_blk = scan_steps(jnp.exp(log_a).reshape(nblk, SUBLANES, LANES),
                                  (scale * (gate_i * xg)).reshape(nblk, SUBLANES, LANES),
                                  1, (1, 2, 4))
        au_ref[0] = a_blk.reshape(ts, LANES)
        au_ref[1] = u_blk.reshape(ts, LANES)
        last = pl.ds(SUBLANES - 1, nblk, stride=SUBLANES)
        a_end, u_end = scan_steps(au_ref[0, last, :], au_ref[1, last, :], 0,
                                  [1 << k for k in range(nblk.bit_length() - 1)])
        h_prev = h_carry[0:1, sl]
        h_end = a_end * h_prev + u_end
        h_carry[:, sl] = jnp.broadcast_to(h_end[nblk - 1:nblk, :], (SUBLANES, LANES))
        row0 = lax.broadcasted_iota(jnp.int32, (nblk, LANES), 0) == 0
        h_in = jnp.where(row0, h_prev, pltpu.roll(h_end, 1, axis=0))
        hg = (a_blk * h_in[:, None, :] + u_blk).reshape(ts, LANES)
        yb = hg * jax.nn.gelu(bgate_ref[:, sl])
        out_ref[0, :, A_WIDTH + g * LANES:A_WIDTH + (g + 1) * LANES] = yb.astype(BF16)


def _mixer_call(x, w_in, b_in, conv_w, conv_b, cn_g, cn_b, lconv_w, lconv_b, w_gate, b_gate, lam,
                cast_weights, cast_layers):
    ts = MIX_TS
    seq_steps = SEQ // ts
    full = lambda a: pl.BlockSpec(a.shape, lambda b, s, nd=a.ndim: (0,) * nd,
                                  pipeline_mode=pl.Buffered(1))
    cast_in, cast_out, cast_shapes = _cast_specs(
        cast_weights, cast_layers, lambda b, s: b * seq_steps + s, BATCH * seq_steps)
    return pl.pallas_call(
        _mixer_kernel,
        grid=(BATCH, seq_steps),
        in_specs=[pl.BlockSpec((1, ts, D_MODEL), lambda b, s: (b, s, 0)),
                  full(w_in), full(b_in), full(conv_w), full(conv_b), full(cn_g), full(cn_b),
                  full(lconv_w), full(lconv_b), full(w_gate), full(b_gate), full(lam)] + cast_in,
        out_specs=[pl.BlockSpec((1, ts, A_WIDTH + B_WIDTH), lambda b, s: (b, s, 0))] + cast_out,
        out_shape=[jax.ShapeDtypeStruct((BATCH, SEQ, A_WIDTH + B_WIDTH), BF16)] + cast_shapes,
        scratch_shapes=[pltpu.VMEM((A_WIDTH // LANES, CONV_HALO + ts + CONV_BLK, LANES), F32),
                        pltpu.VMEM((SUBLANES - 1, A_WIDTH // LANES, ts + CONV_BLK, LANES), F32),
                        pltpu.VMEM((LRU_HALO + ts, B_WIDTH), F32),
                        pltpu.VMEM((SUBLANES, B_WIDTH), F32),
                        pltpu.VMEM((ts, B_WIDTH), F32),
                        pltpu.VMEM((ts, 2 * B_WIDTH), F32),
                        pltpu.VMEM((2, ts, LANES), F32),
                        pltpu.VMEM((ts, D_MODEL), BF16),
                        pltpu.VMEM((ts, A_WIDTH), F32),
                        pltpu.VMEM((D_MODEL, IN_WIDTH), BF16)],
        compiler_params=pltpu.CompilerParams(
            dimension_semantics=("arbitrary", "arbitrary"), vmem_limit_bytes=VMEM_LIMIT),
        name="mixer0",
    )(x, w_in, b_in, conv_w, conv_b, cn_g, cn_b, lconv_w, lconv_b, w_gate, b_gate, lam,
      *cast_weights)


def _post_kernel(m_ref, x_ref, w_out_ref, g1_ref, b1_ref, wg_ref, wu_ref, wd_ref, g2_ref, b2_ref,
                 out_ref, x1_ref, xb_ref, acc_ref):
    y = _dot(m_ref[...], w_out_ref[...])
    x1 = _ln(DN_ALPHA * x_ref[...] + y, g1_ref[...], b1_ref[...])
    x1_ref[...] = x1
    xb_ref[...] = x1.astype(BF16)
    for c in range(D_FF // FF_CHUNK):
        cs = slice(c * FF_CHUNK, (c + 1) * FF_CHUNK)
        gate = _dot(xb_ref[...], wg_ref[:, cs])
        up = _dot(xb_ref[...], wu_ref[:, cs])
        act = (jax.nn.silu(gate) * up).astype(BF16)
        contrib = _dot(act, wd_ref[cs, :])
        if c == 0:
            acc_ref[...] = contrib
        else:
            acc_ref[...] += contrib
    out_ref[...] = _ln(DN_ALPHA * x1_ref[...] + acc_ref[...], g2_ref[...], b2_ref[...])


def _post_call(m, x, w_out, g1, b1, wg, wu, wd, g2, b2):
    tm = POST_TM
    rows = m.shape[0]
    const = lambda i: (0, 0)
    full = lambda a: pl.BlockSpec(a.shape, const, pipeline_mode=pl.Buffered(1))
    return pl.pallas_call(
        _post_kernel,
        grid=(rows // tm,),
        in_specs=[pl.BlockSpec((tm, D_MODEL), lambda i: (i, 0)),
                  pl.BlockSpec((tm, D_MODEL), lambda i: (i, 0)),
                  full(w_out), full(g1), full(b1), full(wg), full(wu), full(wd),
                  full(g2), full(b2)],
        out_specs=pl.BlockSpec((tm, D_MODEL), lambda i: (i, 0)),
        out_shape=jax.ShapeDtypeStruct((rows, D_MODEL), F32),
        scratch_shapes=[pltpu.VMEM((tm, D_MODEL), F32),
                        pltpu.VMEM((tm, D_MODEL), BF16),
                        pltpu.VMEM((tm, D_MODEL), F32)],
        compiler_params=pltpu.CompilerParams(
            dimension_semantics=("arbitrary",), vmem_limit_bytes=VMEM_LIMIT),
        name="post",
    )(m, x, w_out, g1, b1, wg, wu, wd, g2, b2)


def _qkv_kernel(x_ref, w_f32_ref, cos_ref, sin_ref, *rest):
    cast_in, rest = rest[:N_CAST], rest[N_CAST:]
    q_ref, k_ref, v_ref = rest[:3]
    cast_out, (w_ref,) = rest[3:3 + N_CAST], rest[3 + N_CAST:]
    _cast_slabs(cast_in, cast_out)

    @pl.when(pl.program_id(0) == 0)
    def _():
        w_ref[...] = w_f32_ref[...].astype(BF16)

    xb = x_ref[...].astype(BF16)
    cos = cos_ref[...]
    sin = sin_ref[...]
    half_dim = DIFF_HEAD_DIM // 2
    lane = lax.broadcasted_iota(jnp.int32, (1, LANES), 1)
    first_half = (lane % DIFF_HEAD_DIM) < half_dim
    for g in range(2 * QK_WIDTH // MXU_COLS):
        t2 = _dot(xb, w_ref[:, g * MXU_COLS:(g + 1) * MXU_COLS])
        for half in range(MXU_COLS // LANES):
            t = t2[:, half * LANES:(half + 1) * LANES]
            rot = jnp.where(first_half, pltpu.roll(t, LANES - half_dim, axis=1),
                            pltpu.roll(t, half_dim, axis=1))
            r = t * cos + rot * sin
            col = g * MXU_COLS + half * LANES
            if col < QK_WIDTH:
                q_ref[:, col:col + LANES] = (r * Q_SCALE).astype(BF16)
            else:
                k_ref[:, col - QK_WIDTH:col - QK_WIDTH + LANES] = r.astype(BF16)
    v_ref[...] = _dot(xb, w_ref[:, 2 * QK_WIDTH:]).astype(BF16)


def _qkv_call(x, w, cos, sin, cast_weights, cast_layers):
    tm = QKV_TM
    rows = x.shape[0]
    pos_blocks = SEQ // tm
    row_spec = pl.BlockSpec((tm, D_MODEL), lambda i: (i, 0))
    tab_spec = pl.BlockSpec((tm, LANES), lambda i: (i % pos_blocks, 0))
    out = jax.ShapeDtypeStruct((rows, QK_WIDTH), BF16)
    cast_in, cast_out, cast_shapes = _cast_specs(cast_weights, cast_layers, lambda i: i,
                                                 rows // tm)
    return pl.pallas_call(
        _qkv_kernel,
        grid=(rows // tm,),
        in_specs=[row_spec,
                  pl.BlockSpec(w.shape, lambda i: (0, 0), pipeline_mode=pl.Buffered(1)),
                  tab_spec, tab_spec] + cast_in,
        out_specs=[row_spec, row_spec, row_spec] + cast_out,
        out_shape=[out, out, out] + cast_shapes,
        scratch_shapes=[pltpu.VMEM(w.shape, BF16)],
        compiler_params=pltpu.CompilerParams(
            dimension_semantics=("arbitrary",), vmem_limit_bytes=VMEM_LIMIT),
        name="qkv",
    )(x, w, cos, sin, *cast_weights)


def _attn_kernel(lq1_ref, lk1_ref, lq2_ref, lk2_ref, g_ref, q_ref, k_ref, v_ref, o_ref,
                 qs_ref, m_ref, acc_ref, *, lambda_init):
    th = ATT_TH
    tq = 2 * th
    lane = lax.broadcasted_iota(jnp.int32, (1, LANES), 1)
    is_map1 = lane < DIFF_HEAD_DIM
    lam = (jnp.exp(jnp.sum(lq1_ref[...] * lk1_ref[...], axis=-1, keepdims=True))
           - jnp.exp(jnp.sum(lq2_ref[...] * lk2_ref[...], axis=-1, keepdims=True)) + lambda_init)
    gain = g_ref[...] * (1.0 - lambda_init)
    zero = jnp.zeros((), BF16)

    def chunk_mask(n_rows):
        row = lax.broadcasted_iota(jnp.int32, (n_rows, th), 0) % th
        return row // CHUNK >= lax.broadcasted_iota(jnp.int32, (n_rows, th), 1) // CHUNK

    diag = chunk_mask(2 * th)

    def update(h, rows, key_start, n_keys, visible):
        ks = pl.ds(key_start, n_keys)
        hs = slice(h * LANES, (h + 1) * LANES)
        s = lax.dot_general(qs_ref[h, rows, :], k_ref[0, ks, hs], (((1,), (1,)), ((), ())),
                            preferred_element_type=F32)
        if visible is not None:
            n = visible.shape[0]
            masked = jnp.where(visible, s[:n], NEG_INF)
            s = masked if n == s.shape[0] else jnp.concatenate([masked, s[n:]], axis=0)
        m_old = m_ref[h, rows, :]
        m_cur = jnp.max(s, axis=-1, keepdims=True)
        m_new = jnp.maximum(m_old, jnp.broadcast_to(m_cur, m_old.shape))
        alpha = jnp.exp2(m_old - m_new)
        e = jnp.exp2(s - jnp.concatenate([m_new] * (n_keys // LANES), axis=1)).astype(BF16)
        v_ext = jnp.concatenate([v_ref[0, ks, hs], jnp.ones((n_keys, LANES), BF16)], axis=1)
        acc_ref[h, rows, :] = (jnp.concatenate([alpha, alpha], axis=1) * acc_ref[h, rows, :]
                               + _dot(e, v_ext))
        m_ref[h, rows, :] = m_new

    all_rows = slice(0, 4 * th)
    second_half = slice(2 * th, 4 * th)

    def q_block(i, carry):
        q0 = pl.multiple_of(i * tq, tq)
        for h in range(ATT_HP):
            for half in range(2):
                q = q_ref[0, pl.ds(q0 + half * th, th), h * LANES:(h + 1) * LANES]
                qs_ref[h, (2 * half) * th:(2 * half + 1) * th, :] = jnp.where(is_map1, q, zero)
                qs_ref[h, (2 * half + 1) * th:(2 * half + 2) * th, :] = jnp.where(is_map1, zero, q)
        m_ref[...] = jnp.full(m_ref.shape, NEG_INF, F32)
        acc_ref[...] = jnp.zeros(acc_ref.shape, F32)

        def off_diag(j, c):
            for h in range(ATT_HP):
                update(h, all_rows, pl.multiple_of(j * tq, tq), tq, None)
            return c

        lax.fori_loop(0, i, off_diag, 0)
        for h in range(ATT_HP):
            update(h, all_rows, q0, th, diag)
            update(h, second_half, q0 + th, th, diag)

        for h in range(ATT_HP):
            for half in range(2):
                a1 = acc_ref[h, (2 * half) * th:(2 * half + 1) * th, :]
                a2 = acc_ref[h, (2 * half + 1) * th:(2 * half + 2) * th, :]
                o = (a1[:, :LANES] * (1.0 / a1[:, LANES:])
                     - a2[:, :LANES] * (lam * (1.0 / a2[:, LANES:])))
                o = o * lax.rsqrt(jnp.mean(o * o, axis=-1, keepdims=True) + LN_EPS) * gain
                o_ref[0, pl.ds(q0 + half * th, th), h * LANES:(h + 1) * LANES] = o.astype(BF16)
        return carry

    lax.fori_loop(0, SEQ // tq, q_block, 0)


def _attn_call(lq1, lk1, lq2, lk2, g, q, k, v, lambda_init):
    const = lambda b, h: (0, 0)
    small = lambda a: pl.BlockSpec(a.shape, const)
    head_spec = pl.BlockSpec((1, SEQ, ATT_HP * LANES), lambda b, h: (b, 0, h))
    return pl.pallas_call(
        functools.partial(_attn_kernel, lambda_init=lambda_init),
        grid=(BATCH, DIFF_HEADS // ATT_HP),
        in_specs=[small(lq1), small(lk1), small(lq2), small(lk2), small(g),
                  head_spec, head_spec, head_spec],
        out_specs=head_spec,
        out_shape=jax.ShapeDtypeStruct((BATCH, SEQ, V_WIDTH), BF16),
        scratch_shapes=[pltpu.VMEM((ATT_HP, 4 * ATT_TH, LANES), BF16),
                        pltpu.VMEM((ATT_HP, 4 * ATT_TH, LANES), F32),
                        pltpu.VMEM((ATT_HP, 4 * ATT_TH, 2 * LANES), F32)],
        compiler_params=pltpu.CompilerParams(
            dimension_semantics=("arbitrary", "arbitrary"), vmem_limit_bytes=VMEM_LIMIT),
        name="diff_attn",
    )(lq1, lk1, lq2, lk2, g, q, k, v)


def _block_diag(w):
    same_block = np.eye(LRU_BLOCKS, dtype=np.float32)[:, None, :, None]
    return (w[:, :, None, :] * same_block).reshape(B_WIDTH, B_WIDTH)


def _rope_tables():
    half_dim = DIFF_HEAD_DIM // 2
    pos = np.arange(SEQ, dtype=np.float64)
    inv_freq = ROPE_THETA ** (-np.arange(0, DIFF_HEAD_DIM, 2, dtype=np.float64) / DIFF_HEAD_DIM)
    lane = np.arange(LANES)
    ang = pos[:, None] * inv_freq[lane % half_dim][None, :]
    sign = np.where(lane % DIFF_HEAD_DIM < half_dim, -1.0, 1.0)
    return (jnp.asarray(np.cos(ang), dtype=F32), jnp.asarray(np.sin(ang) * sign[None, :], dtype=F32))


def kernel(x, even_w_in, even_b_in, even_conv_w, even_conv_b, even_cnorm_g, even_cnorm_b,
           even_lru_conv_w, even_lru_conv_b, even_w_a, even_b_a, even_w_x, even_b_x,
           even_lru_lambda, even_w_out, odd_w_qkv, odd_lambda_q1, odd_lambda_k1,
           odd_lambda_q2, odd_lambda_k2, odd_subln_g, odd_w_out, mix_ln_g, mix_ln_b,
           ffn_w_gate, ffn_w_up, ffn_w_down, ffn_ln_g, ffn_ln_b):
    row = lambda a: a.reshape(1, -1)
    rows = BATCH * SEQ

    def post(m, xres, weights, layer):
        w_out, wg, wu, wd = weights
        return _post_call(m.reshape(rows, -1), xres.reshape(rows, D_MODEL), w_out,
                          row(mix_ln_g[layer]), row(mix_ln_b[layer]), wg, wu, wd,
                          row(ffn_ln_g[layer]), row(ffn_ln_b[layer]))

    w_gate = jnp.concatenate([_block_diag(even_w_a[0]), _block_diag(even_w_x[0])], axis=1)
    b_gate = jnp.concatenate([even_b_a[0], even_b_x[0]]).reshape(1, -1)
    conv_w = jnp.broadcast_to(even_conv_w[0][:, None, :], (CONV_WIDTH, SUBLANES, A_WIDTH))
    m0, *tail0 = _mixer_call(x, even_w_in[0], row(even_b_in[0]), conv_w,
                             row(even_conv_b[0]), row(even_cnorm_g[0]), row(even_cnorm_b[0]),
                             even_lru_conv_w[0], row(even_lru_conv_b[0]), w_gate.astype(BF16),
                             b_gate, row(even_lru_lambda[0]),
                             [even_w_out, ffn_w_gate, ffn_w_up, ffn_w_down], [0, 0, 0, 0])
    x1 = post(m0, x, tail0, 0)

    lambda_init = 0.8 - 0.6 * math.exp(-0.3 * 1)
    cos, sin = _rope_tables()
    q, k, v, *tail1 = _qkv_call(x1, odd_w_qkv[0], cos, sin,
                                [odd_w_out, ffn_w_gate, ffn_w_up, ffn_w_down], [0, 1, 1, 1])
    shape3 = (BATCH, SEQ, QK_WIDTH)
    o = _attn_call(row(odd_lambda_q1[0]), row(odd_lambda_k1[0]), row(odd_lambda_q2[0]),
                   row(odd_lambda_k2[0]), row(odd_subln_g[0]),
                   q.reshape(shape3), k.reshape(shape3), v.reshape(shape3), lambda_init)
    out = post(o, x1, tail1, 1)
    return out.reshape(BATCH, SEQ, D_MODEL)
```

```python
import functools
import math

import jax
import jax.numpy as jnp
import numpy as np
from jax import lax
from jax.experimental import pallas as pl
from jax.experimental.pallas import tpu as pltpu

F32 = jnp.float32
BF16 = jnp.bfloat16

D_MODEL = 1024
BATCH = 8
SEQ = 2048
DEPTH = 2
CHUNK = 64
A_WIDTH = 512
B_WIDTH = 512
CONV_WIDTH = 31
LRU_BLOCKS = 8
LRU_CONV_WIDTH = 4
LRU_C = 8.0
IN_WIDTH = 2 * A_WIDTH + 2 * B_WIDTH
DIFF_HEADS = 8
DIFF_HEAD_DIM = 64
QK_WIDTH = 1024
V_WIDTH = 1024
ROPE_THETA = 10000.0
D_FF = 2816
LN_EPS = 1e-5
DN_ALPHA = (2 * DEPTH) ** 0.25
NEG_INF = -1e30
Q_SCALE = DIFF_HEAD_DIM ** -0.5 * math.log2(math.e)

LANES = 128
SUBLANES = 8
BF16_SUBLANES = 16
MXU_COLS = 256
VMEM_LIMIT = 56 * 1024 * 1024

MIX_TS = 512
CONV_HALO = 32
CONV_ROWS = 32
CONV_BLK = 64
LRU_HALO = 8
POST_TM = 1024
FF_CHUNK = 256
QKV_TM = 1024
ATT_HP = 8
ATT_TH = 256


def _ln(x, g, b):
    mu = jnp.mean(x, axis=-1, keepdims=True)
    xc = x - mu
    var = jnp.mean(xc * xc, axis=-1, keepdims=True)
    return xc * lax.rsqrt(var + LN_EPS) * g + b


def _dot(a, b):
    return jnp.dot(a, b, preferred_element_type=F32)


N_CAST = 4


def _cast_slabs(cast_in, cast_out):
    for src, dst in zip(cast_in, cast_out):
        dst[...] = src[...].astype(BF16)


def _cast_specs(weights, layers, step_of, n_steps):
    in_specs, out_specs, out_shapes = [], [], []
    for w, layer in zip(weights, layers):
        _, rows, cols = w.shape
        slab = rows // n_steps
        per_slab = 1
        while (slab * per_slab) % BF16_SUBLANES:
            per_slab *= 2
        slab *= per_slab
        in_specs.append(pl.BlockSpec(
            (None, slab, cols), lambda *g, l=layer, p=per_slab: (l, step_of(*g) // p, 0)))
        out_specs.append(pl.BlockSpec(
            (slab, cols), lambda *g, p=per_slab: (step_of(*g) // p, 0)))
        out_shapes.append(jax.ShapeDtypeStruct((rows, cols), BF16))
    return in_specs, out_specs, out_shapes


def _mixer_kernel(x_ref, w_in_f32_ref, b_in_ref, conv_w_ref, conv_b_ref, cn_g_ref, cn_b_ref,
                  lconv_w_ref, lconv_b_ref, w_gate_ref, b_gate_ref, lam_ref, *rest):
    cast_in, rest = rest[:N_CAST], rest[N_CAST:]
    out_ref, rest = rest[0], rest[1:]
    cast_out, rest = rest[:N_CAST], rest[N_CAST:]
    (a_ext, a_sh, b_ext, h_carry, bgate_ref, gates_ref, au_ref, xb_ref, conv_ref,
     w_in_ref) = rest
    ts = MIX_TS
    s = pl.program_id(1)

    n_groups = A_WIDTH // LANES
    _cast_slabs(cast_in, cast_out)

    @pl.when((pl.program_id(0) == 0) & (s == 0))
    def _():
        w_in_ref[...] = w_in_f32_ref[...].astype(BF16)

    @pl.when(s == 0)
    def _():
        a_ext[:, 0:CONV_HALO, :] = jnp.zeros((n_groups, CONV_HALO, LANES), F32)
        a_ext[:, CONV_HALO + ts:, :] = jnp.zeros((n_groups, CONV_BLK, LANES), F32)
        b_ext[0:LRU_HALO, :] = jnp.zeros((LRU_HALO, B_WIDTH), F32)
        h_carry[...] = jnp.zeros_like(h_carry)

    xb_ref[...] = x_ref[0].astype(BF16)
    ha = _dot(xb_ref[...], w_in_ref[:, 0:2 * A_WIDTH]) + b_in_ref[:, 0:2 * A_WIDTH]
    glu = ha[:, 0:A_WIDTH] * jax.nn.sigmoid(ha[:, A_WIDTH:])
    for g in range(n_groups):
        a_ext[g, CONV_HALO:CONV_HALO + ts, :] = glu[:, g * LANES:(g + 1) * LANES]

    off = CONV_HALO - (CONV_WIDTH - 1)

    for g in range(n_groups):
        def shift_body(i, carry, g=g):
            base = pl.multiple_of(i * CONV_BLK, CONV_BLK)
            win = a_ext[g, pl.ds(base, CONV_BLK + SUBLANES), :]
            for r in range(1, SUBLANES):
                rolled = pltpu.roll(win, CONV_BLK + SUBLANES - r, axis=0)
                a_sh[r - 1, g, pl.ds(base, CONV_BLK), :] = rolled[0:CONV_BLK, :]
            return carry

        lax.fori_loop(0, (ts + CONV_BLK) // CONV_BLK, shift_body, 0, unroll=3)

    for g in range(n_groups):
        sl = slice(g * LANES, (g + 1) * LANES)
        taps = [conv_w_ref[j, :, sl] for j in range(CONV_WIDTH)]
        bias = conv_b_ref[:, sl]

        def conv_body(i, carry, g=g, sl=sl, taps=taps, bias=bias):
            base = pl.multiple_of(i * CONV_BLK, CONV_BLK)
            acc = jnp.broadcast_to(bias, (CONV_BLK, LANES))
            for j in range(CONV_WIDTH):
                q, r = divmod(off + j, SUBLANES)
                rows = pl.ds(base + q * SUBLANES, CONV_BLK)
                win = a_ext[g, rows, :] if r == 0 else a_sh[r - 1, g, rows, :]
                acc = acc + (win.reshape(-1, SUBLANES, LANES) * taps[j][None]).reshape(CONV_BLK, LANES)
            conv_ref[pl.ds(base, CONV_BLK), sl] = acc
            return carry

        lax.fori_loop(0, ts // CONV_BLK, conv_body, 0, unroll=4)
    a_ext[:, 0:CONV_HALO, :] = a_ext[:, ts:ts + CONV_HALO, :]

    def norm_body(i, carry):
        base = pl.multiple_of(i * CONV_ROWS, CONV_ROWS)
        ya = jax.nn.silu(_ln(conv_ref[pl.ds(base, CONV_ROWS), :], cn_g_ref[...], cn_b_ref[...]))
        out_ref[0, pl.ds(base, CONV_ROWS), 0:A_WIDTH] = ya.astype(BF16)
        return carry

    lax.fori_loop(0, ts // CONV_ROWS, norm_body, 0, unroll=True)

    rec0 = 2 * A_WIDTH + B_WIDTH
    b_ext[LRU_HALO:LRU_HALO + ts, :] = _dot(xb_ref[...], w_in_ref[:, rec0:]) + b_in_ref[:, rec0:]
    loff = LRU_HALO - (LRU_CONV_WIDTH - 1)
    xcs = []
    for blk in range(B_WIDTH // MXU_COLS):
        bs = slice(blk * MXU_COLS, (blk + 1) * MXU_COLS)
        xc = jnp.broadcast_to(lconv_b_ref[:, bs], (ts, MXU_COLS))
        for j in range(LRU_CONV_WIDTH):
            xc = xc + lconv_w_ref[j:j + 1, bs] * b_ext[loff + j:loff + j + ts, bs]
        xcs.append(xc)
        xcb = xc.astype(BF16)
        for gate in range(2):
            cs = slice(gate * B_WIDTH + blk * MXU_COLS, gate * B_WIDTH + (blk + 1) * MXU_COLS)
            gates_ref[:, cs] = _dot(xcb, w_gate_ref[bs, cs]) + b_gate_ref[:, cs]
    b_ext[0:LRU_HALO, :] = b_ext[ts:ts + LRU_HALO, :]
    for blk, xc in enumerate(xcs):
        b_ext[LRU_HALO:LRU_HALO + ts, blk * MXU_COLS:(blk + 1) * MXU_COLS] = xc
    bgate_ref[...] = _dot(xb_ref[...], w_in_ref[:, 2 * A_WIDTH:rec0]) + b_in_ref[:, 2 * A_WIDTH:rec0]

    def scan_steps(a_cum, u_cum, axis, steps):
        idx = lax.broadcasted_iota(jnp.int32, a_cum.shape, axis)
        for step in steps:
            keep = idx >= step
            a_prev = jnp.where(keep, pltpu.roll(a_cum, step, axis=axis), 1.0)
            u_prev = jnp.where(keep, pltpu.roll(u_cum, step, axis=axis), 0.0)
            u_cum = a_cum * u_prev + u_cum
            a_cum = a_cum * a_prev
        return a_cum, u_cum

    nblk = ts // SUBLANES
    for g in range(B_WIDTH // LANES):
        sl = slice(g * LANES, (g + 1) * LANES)
        xg = b_ext[LRU_HALO:LRU_HALO + ts, sl]
        gate_r = jax.nn.sigmoid(gates_ref[:, sl])
        gate_i = jax.nn.sigmoid(gates_ref[:, B_WIDTH + g * LANES:B_WIDTH + (g + 1) * LANES])
        lam = lam_ref[:, sl]
        neg = -lam
        softplus = jnp.maximum(neg, 0.0) + jnp.log1p(jnp.exp(-jnp.abs(neg)))
        log_a = (-LRU_C * gate_r) * softplus
        th = jnp.tanh(log_a)
        y = -2.0 * th
        scale = jnp.where(y > 0.0, y * lax.rsqrt(y), 0.0) * lax.rsqrt(1.0 - th)
        a_blk, u_blk = scan_steps(jnp.exp(log_a).reshape(nblk, SUBLANES, LANES),
                                  (scale * (gate_i * xg)).reshape(nblk, SUBLANES, LANES),
                                  1, (1, 2, 4))
        au_ref[0] = a_blk.reshape(ts, LANES)
        au_ref[1] = u_blk.reshape(ts, LANES)
        last = pl.ds(SUBLANES - 1, nblk, stride=SUBLANES)
        a_end, u_end = scan_steps(au_ref[0, last, :], au_ref[1, last, :], 0,
                                  [1 << k for k in range(nblk.bit_length() - 1)])
        h_prev = h_carry[0:1, sl]
        h_end = a_end * h_prev + u_end
        h_carry[:, sl] = jnp.broadcast_to(h_end[nblk - 1:nblk, :], (SUBLANES, LANES))
        row0 = lax.broadcasted_iota(jnp.int32, (nblk, LANES), 0) == 0
        h_in = jnp.where(row0, h_prev, pltpu.roll(h_end, 1, axis=0))
        hg = (a_blk * h_in[:, None, :] + u_blk).reshape(ts, LANES)
        yb = hg * jax.nn.gelu(bgate_ref[:, sl])
        out_ref[0, :, A_WIDTH + g * LANES:A_WIDTH + (g + 1) * LANES] = yb.astype(BF16)


def _mixer_call(x, w_in, b_in, conv_w, conv_b, cn_g, cn_b, lconv_w, lconv_b, w_gate, b_gate, lam,
                cast_weights, cast_layers):
    ts = MIX_TS
    seq_steps = SEQ // ts
    full = lambda a: pl.BlockSpec(a.shape, lambda b, s, nd=a.ndim: (0,) * nd,
                                  pipeline_mode=pl.Buffered(1))
    cast_in, cast_out, cast_shapes = _cast_specs(
        cast_weights, cast_layers, lambda b, s: b * seq_steps + s, BATCH * seq_steps)
    return pl.pallas_call(
        _mixer_kernel,
        grid=(BATCH, seq_steps),
        in_specs=[pl.BlockSpec((1, ts, D_MODEL), lambda b, s: (b, s, 0)),
                  full(w_in), full(b_in), full(conv_w), full(conv_b), full(cn_g), full(cn_b),
                  full(lconv_w), full(lconv_b), full(w_gate), full(b_gate), full(lam)] + cast_in,
        out_specs=[pl.BlockSpec((1, ts, A_WIDTH + B_WIDTH), lambda b, s: (b, s, 0))] + cast_out,
        out_shape=[jax.ShapeDtypeStruct((BATCH, SEQ, A_WIDTH + B_WIDTH), BF16)] + cast_shapes,
        scratch_shapes=[pltpu.VMEM((A_WIDTH // LANES, CONV_HALO + ts + CONV_BLK, LANES), F32),
                        pltpu.VMEM((SUBLANES - 1, A_WIDTH // LANES, ts + CONV_BLK, LANES), F32),
                        pltpu.VMEM((LRU_HALO + ts, B_WIDTH), F32),
                        pltpu.VMEM((SUBLANES, B_WIDTH), F32),
                        pltpu.VMEM((ts, B_WIDTH), F32),
                        pltpu.VMEM((ts, 2 * B_WIDTH), F32),
                        pltpu.VMEM((2, ts, LANES), F32),
                        pltpu.VMEM((ts, D_MODEL), BF16),
                        pltpu.VMEM((ts, A_WIDTH), F32),
                        pltpu.VMEM((D_MODEL, IN_WIDTH), BF16)],
        compiler_params=pltpu.CompilerParams(
            dimension_semantics=("arbitrary", "arbitrary"), vmem_limit_bytes=VMEM_LIMIT),
        name="mixer0",
    )(x, w_in, b_in, conv_w, conv_b, cn_g, cn_b, lconv_w, lconv_b, w_gate, b_gate, lam,
      *cast_weights)


def _post_kernel(m_ref, x_ref, w_out_ref, g1_ref, b1_ref, wg_ref, wu_ref, wd_ref, g2_ref, b2_ref,
                 out_ref, x1_ref, xb_ref, acc_ref):
    y = _dot(m_ref[...], w_out_ref[...])
    x1 = _ln(DN_ALPHA * x_ref[...] + y, g1_ref[...], b1_ref[...])
    x1_ref[...] = x1
    xb_ref[...] = x1.astype(BF16)
    for c in range(D_FF // FF_CHUNK):
        cs = slice(c * FF_CHUNK, (c + 1) * FF_CHUNK)
        gate = _dot(xb_ref[...], wg_ref[:, cs])
        up = _dot(xb_ref[...], wu_ref[:, cs])
        act = (jax.nn.silu(gate) * up).astype(BF16)
        contrib = _dot(act, wd_ref[cs, :])
        if c == 0:
            acc_ref[...] = contrib
        else:
            acc_ref[...] += contrib
    out_ref[...] = _ln(DN_ALPHA * x1_ref[...] + acc_ref[...], g2_ref[...], b2_ref[...])


def _post_call(m, x, w_out, g1, b1, wg, wu, wd, g2, b2):
    tm = POST_TM
    rows = m.shape[0]
    const = lambda i: (0, 0)
    full = lambda a: pl.BlockSpec(a.shape, const, pipeline_mode=pl.Buffered(1))
    return pl.pallas_call(
        _post_kernel,
        grid=(rows // tm,),
        in_specs=[pl.BlockSpec((tm, D_MODEL), lambda i: (i, 0)),
                  pl.BlockSpec((tm, D_MODEL), lambda i: (i, 0)),
                  full(w_out), full(g1), full(b1), full(wg), full(wu), full(wd),
                  full(g2), full(b2)],
        out_specs=pl.BlockSpec((tm, D_MODEL), lambda i: (i, 0)),
        out_shape=jax.ShapeDtypeStruct((rows, D_MODEL), F32),
        scratch_shapes=[pltpu.VMEM((tm, D_MODEL), F32),
                        pltpu.VMEM((tm, D_MODEL), BF16),
                        pltpu.VMEM((tm, D_MODEL), F32)],
        compiler_params=pltpu.CompilerParams(
            dimension_semantics=("arbitrary",), vmem_limit_bytes=VMEM_LIMIT),
        name="post",
    )(m, x, w_out, g1, b1, wg, wu, wd, g2, b2)


def _qkv_kernel(x_ref, w_f32_ref, cos_ref, sin_ref, *rest):
    cast_in, rest = rest[:N_CAST], rest[N_CAST:]
    q_ref, k_ref, v_ref = rest[:3]
    cast_out, (w_ref,) = rest[3:3 + N_CAST], rest[3 + N_CAST:]
    _cast_slabs(cast_in, cast_out)

    @pl.when(pl.program_id(0) == 0)
    def _():
        w_ref[...] = w_f32_ref[...].astype(BF16)

    xb = x_ref[...].astype(BF16)
    cos = cos_ref[...]
    sin = sin_ref[...]
    half_dim = DIFF_HEAD_DIM // 2
    lane = lax.broadcasted_iota(jnp.int32, (1, LANES), 1)
    first_half = (lane % DIFF_HEAD_DIM) < half_dim
    for g in range(2 * QK_WIDTH // MXU_COLS):
        t2 = _dot(xb, w_ref[:, g * MXU_COLS:(g + 1) * MXU_COLS])
        for half in range(MXU_COLS // LANES):
            t = t2[:, half * LANES:(half + 1) * LANES]
            rot = jnp.where(first_half, pltpu.roll(t, LANES - half_dim, axis=1),
                            pltpu.roll(t, half_dim, axis=1))
            r = t * cos + rot * sin
            col = g * MXU_COLS + half * LANES
            if col < QK_WIDTH:
                q_ref[:, col:col + LANES] = (r * Q_SCALE).astype(BF16)
            else:
                k_ref[:, col - QK_WIDTH:col - QK_WIDTH + LANES] = r.astype(BF16)
    v_ref[...] = _dot(xb, w_ref[:, 2 * QK_WIDTH:]).astype(BF16)


def _qkv_call(x, w, cos, sin, cast_weights, cast_layers):
    tm = QKV_TM
    rows = x.shape[0]
    pos_blocks = SEQ // tm
    row_spec = pl.BlockSpec((tm, D_MODEL), lambda i: (i, 0))
    tab_spec = pl.BlockSpec((tm, LANES), lambda i: (i % pos_blocks, 0))
    out = jax.ShapeDtypeStruct((rows, QK_WIDTH), BF16)
    cast_in, cast_out, cast_shapes = _cast_specs(cast_weights, cast_layers, lambda i: i,
                                                 rows // tm)
    return pl.pallas_call(
        _qkv_kernel,
        grid=(rows // tm,),
        in_specs=[row_spec,
                  pl.BlockSpec(w.shape, lambda i: (0, 0), pipeline_mode=pl.Buffered(1)),
                  tab_spec, tab_spec] + cast_in,
        out_specs=[row_spec, row_spec, row_spec] + cast_out,
        out_shape=[out, out, out] + cast_shapes,
        scratch_shapes=[pltpu.VMEM(w.shape, BF16)],
        compiler_params=pltpu.CompilerParams(
            dimension_semantics=("arbitrary",), vmem_limit_bytes=VMEM_LIMIT),
        name="qkv",
    )(x, w, cos, sin, *cast_weights)


def _attn_kernel(lq1_ref, lk1_ref, lq2_ref, lk2_ref, g_ref, q_ref, k_ref, v_ref, o_ref,
                 qs_ref, m_ref, acc_ref, *, lambda_init):
    th = ATT_TH
    tq = 2 * th
    lane = lax.broadcasted_iota(jnp.int32, (1, LANES), 1)
    is_map1 = lane < DIFF_HEAD_DIM
    lam = (jnp.exp(jnp.sum(lq1_ref[...] * lk1_ref[...], axis=-1, keepdims=True))
           - jnp.exp(jnp.sum(lq2_ref[...] * lk2_ref[...], axis=-1, keepdims=True)) + lambda_init)
    gain = g_ref[...] * (1.0 - lambda_init)
    zero = jnp.zeros((), BF16)

    def chunk_mask(n_rows):
        row = lax.broadcasted_iota(jnp.int32, (n_rows, th), 0) % th
        return row // CHUNK >= lax.broadcasted_iota(jnp.int32, (n_rows, th), 1) // CHUNK

    diag = chunk_mask(2 * th)

    def update(h, rows, key_start, n_keys, visible):
        ks = pl.ds(key_start, n_keys)
        hs = slice(h * LANES, (h + 1) * LANES)
        s = lax.dot_general(qs_ref[h, rows, :], k_ref[0, ks, hs], (((1,), (1,)), ((), ())),
                            preferred_element_type=F32)
        if visible is not None:
            n = visible.shape[0]
            masked = jnp.where(visible, s[:n], NEG_INF)
            s = masked if n == s.shape[0] else jnp.concatenate([masked, s[n:]], axis=0)
        m_old = m_ref[h, rows, :]
        m_cur = jnp.max(s, axis=-1, keepdims=True)
        m_new = jnp.maximum(m_old, jnp.broadcast_to(m_cur, m_old.shape))
        alpha = jnp.exp2(m_old - m_new)
        e = jnp.exp2(s - jnp.concatenate([m_new] * (n_keys // LANES), axis=1)).astype(BF16)
        v_ext = jnp.concatenate([v_ref[0, ks, hs], jnp.ones((n_keys, LANES), BF16)], axis=1)
        acc_ref[h, rows, :] = (jnp.concatenate([alpha, alpha], axis=1) * acc_ref[h, rows, :]
                               + _dot(e, v_ext))
        m_ref[h, rows, :] = m_new

    all_rows = slice(0, 4 * th)
    second_half = slice(2 * th, 4 * th)

    def q_block(i, carry):
        q0 = pl.multiple_of(i * tq, tq)
        for h in range(ATT_HP):
            for half in range(2):
                q = q_ref[0, pl.ds(q0 + half * th, th), h * LANES:(h + 1) * LANES]
                qs_ref[h, (2 * half) * th:(2 * half + 1) * th, :] = jnp.where(is_map1, q, zero)
                qs_ref[h, (2 * half + 1) * th:(2 * half + 2) * th, :] = jnp.where(is_map1, zero, q)
        m_ref[...] = jnp.full(m_ref.shape, NEG_INF, F32)
        acc_ref[...] = jnp.zeros(acc_ref.shape, F32)

        def off_diag(j, c):
            for h in range(ATT_HP):
                update(h, all_rows, pl.multiple_of(j * tq, tq), tq, None)
            return c

        lax.fori_loop(0, i, off_diag, 0)
        for h in range(ATT_HP):
            update(h, all_rows, q0, th, diag)
            update(h, second_half, q0 + th, th, diag)

        for h in range(ATT_HP):
            for half in range(2):
                a1 = acc_ref[h, (2 * half) * th:(2 * half + 1) * th, :]
                a2 = acc_ref[h, (2 * half + 1) * th:(2 * half + 2) * th, :]
                o = (a1[:, :LANES] * (1.0 / a1[:, LANES:])
                     - a2[:, :LANES] * (lam * (1.0 / a2[:, LANES:])))
                o = o * lax.rsqrt(jnp.mean(o * o, axis=-1, keepdims=True) + LN_EPS) * gain
                o_ref[0, pl.ds(q0 + half * th, th), h * LANES:(h + 1) * LANES] = o.astype(BF16)
        return carry

    lax.fori_loop(0, SEQ // tq, q_block, 0)


def _attn_call(lq1, lk1, lq2, lk2, g, q, k, v, lambda_init):
    const = lambda b, h: (0, 0)
    small = lambda a: pl.BlockSpec(a.shape, const)
    head_spec = pl.BlockSpec((1, SEQ, ATT_HP * LANES), lambda b, h: (b, 0, h))
    return pl.pallas_call(
        functools.partial(_attn_kernel, lambda_init=lambda_init),
        grid=(BATCH, DIFF_HEADS // ATT_HP),
        in_specs=[small(lq1), small(lk1), small(lq2), small(lk2), small(g),
                  head_spec, head_spec, head_spec],
        out_specs=head_spec,
        out_shape=jax.ShapeDtypeStruct((BATCH, SEQ, V_WIDTH), BF16),
        scratch_shapes=[pltpu.VMEM((ATT_HP, 4 * ATT_TH, LANES), BF16),
                        pltpu.VMEM((ATT_HP, 4 * ATT_TH, LANES), F32),
                        pltpu.VMEM((ATT_HP, 4 * ATT_TH, 2 * LANES), F32)],
        compiler_params=pltpu.CompilerParams(
            dimension_semantics=("arbitrary", "arbitrary"), vmem_limit_bytes=VMEM_LIMIT),
        name="diff_attn",
    )(lq1, lk1, lq2, lk2, g, q, k, v)


def _block_diag(w):
    same_block = np.eye(LRU_BLOCKS, dtype=np.float32)[:, None, :, None]
    return (w[:, :, None, :] * same_block).reshape(B_WIDTH, B_WIDTH)


def _rope_tables():
    half_dim = DIFF_HEAD_DIM // 2
    pos = np.arange(SEQ, dtype=np.float64)
    inv_freq = ROPE_THETA ** (-np.arange(0, DIFF_HEAD_DIM, 2, dtype=np.float64) / DIFF_HEAD_DIM)
    lane = np.arange(LANES)
    ang = pos[:, None] * inv_freq[lane % half_dim][None, :]
    sign = np.where(lane % DIFF_HEAD_DIM < half_dim, -1.0, 1.0)
    return (jnp.asarray(np.cos(ang), dtype=F32), jnp.asarray(np.sin(ang) * sign[None, :], dtype=F32))


def kernel(x, even_w_in, even_b_in, even_conv_w, even_conv_b, even_cnorm_g, even_cnorm_b,
           even_lru_conv_w, even_lru_conv_b, even_w_a, even_b_a, even_w_x, even_b_x,
           even_lru_lambda, even_w_out, odd_w_qkv, odd_lambda_q1, odd_lambda_k1,
           odd_lambda_q2, odd_lambda_k2, odd_subln_g, odd_w_out, mix_ln_g, mix_ln_b,
           ffn_w_gate, ffn_w_up, ffn_w_down, ffn_ln_g, ffn_ln_b):
    row = lambda a: a.reshape(1, -1)
    rows = BATCH * SEQ

    def post(m, xres, weights, layer):
        w_out, wg, wu, wd = weights
        return _post_call(m.reshape(rows, -1), xres.reshape(rows, D_MODEL), w_out,
                          row(mix_ln_g[layer]), row(mix_ln_b[layer]), wg, wu, wd,
                          row(ffn_ln_g[layer]), row(ffn_ln_b[layer]))

    w_gate = jnp.concatenate([_block_diag(even_w_a[0]), _block_diag(even_w_x[0])], axis=1)
    b_gate = jnp.concatenate([even_b_a[0], even_b_x[0]]).reshape(1, -1)
    conv_w = jnp.broadcast_to(even_conv_w[0][:, None, :], (CONV_WIDTH, SUBLANES, A_WIDTH))
    m0, *tail0 = _mixer_call(x, even_w_in[0], row(even_b_in[0]), conv_w,
                             row(even_conv_b[0]), row(even_cnorm_g[0]), row(even_cnorm_b[0]),
                             even_lru_conv_w[0], row(even_lru_conv_b[0]), w_gate.astype(BF16),
                             b_gate, row(even_lru_lambda[0]),
                             [even_w_out, ffn_w_gate, ffn_w_up, ffn_w_down], [0, 0, 0, 0])
    x1 = post(m0, x, tail0, 0)

    lambda_init = 0.8 - 0.6 * math.exp(-0.3 * 1)
    cos, sin = _rope_tables()
    q, k, v, *tail1 = _qkv_call(x1, odd_w_qkv[0], cos, sin,
                                [odd_w_out, ffn_w_gate, ffn_w_up, ffn_w_down], [0, 1, 1, 1])
    shape3 = (BATCH, SEQ, QK_WIDTH)
    o = _attn_call(row(odd_lambda_q1[0]), row(odd_lambda_k1[0]), row(odd_lambda_q2[0]),
                   row(odd_lambda_k2[0]), row(odd_subln_g[0]),
                   q.reshape(shape3), k.reshape(shape3), v.reshape(shape3), lambda_init)
    out = post(o, x1, tail1, 1)
    return out.reshape(BATCH, SEQ, D_MODEL)
```

```python
import functools
import math

import jax
import jax.numpy as jnp
import numpy as np
from jax import lax
from jax.experimental import pallas as pl
from jax.experimental.pallas import tpu as pltpu

F32 = jnp.float32
BF16 = jnp.bfloat16

D_MODEL = 1024
BATCH = 8
SEQ = 2048
DEPTH = 2
CHUNK = 64
A_WIDTH = 512
B_WIDTH = 512
CONV_WIDTH = 31
LRU_BLOCKS = 8
LRU_CONV_WIDTH = 4
LRU_C = 8.0
IN_WIDTH = 2 * A_WIDTH + 2 * B_WIDTH
DIFF_HEADS = 8
DIFF_HEAD_DIM = 64
QK_WIDTH = 1024
V_WIDTH = 1024
ROPE_THETA = 10000.0
D_FF = 2816
LN_EPS = 1e-5
DN_ALPHA = (2 * DEPTH) ** 0.25
NEG_INF = -1e30
Q_SCALE = DIFF_HEAD_DIM ** -0.5 * math.log2(math.e)

LANES = 128
SUBLANES = 8
BF16_SUBLANES = 16
MXU_COLS = 256
VMEM_LIMIT = 56 * 1024 * 1024

MIX_TS = 512
CONV_HALO = 32
CONV_ROWS = 32
CONV_BLK = 64
LRU_HALO = 8
POST_TM = 1024
FF_CHUNK = 256
QKV_TM = 1024
ATT_HP = 8
ATT_TH = 256


def _ln(x, g, b):
    mu = jnp.mean(x, axis=-1, keepdims=True)
    xc = x - mu
    var = jnp.mean(xc * xc, axis=-1, keepdims=True)
    return xc * lax.rsqrt(var + LN_EPS) * g + b


def _dot(a, b):
    return jnp.dot(a, b, preferred_element_type=F32)


N_CAST = 4


def _cast_slabs(cast_in, cast_out):
    for src, dst in zip(cast_in, cast_out):
        dst[...] = src[...].astype(BF16)


def _cast_specs(weights, layers, step_of, n_steps):
    in_specs, out_specs, out_shapes = [], [], []
    for w, layer in zip(weights, layers):
        _, rows, cols = w.shape
        slab = rows // n_steps
        per_slab = 1
        while (slab * per_slab) % BF16_SUBLANES:
            per_slab *= 2
        slab *= per_slab
        in_specs.append(pl.BlockSpec(
            (None, slab, cols), lambda *g, l=layer, p=per_slab: (l, step_of(*g) // p, 0)))
        out_specs.append(pl.BlockSpec(
            (slab, cols), lambda *g, p=per_slab: (step_of(*g) // p, 0)))
        out_shapes.append(jax.ShapeDtypeStruct((rows, cols), BF16))
    return in_specs, out_specs, out_shapes


def _mixer_kernel(x_ref, w_in_f32_ref, b_in_ref, conv_w_ref, conv_b_ref, cn_g_ref, cn_b_ref,
                  lconv_w_ref, lconv_b_ref, w_gate_ref, b_gate_ref, lam_ref, *rest):
    cast_in, rest = rest[:N_CAST], rest[N_CAST:]
    out_ref, rest = rest[0], rest[1:]
    cast_out, rest = rest[:N_CAST], rest[N_CAST:]
    (a_ext, a_sh, b_ext, h_carry, bgate_ref, gates_ref, au_ref, xb_ref, conv_ref,
     w_in_ref) = rest
    ts = MIX_TS
    s = pl.program_id(1)

    n_groups = A_WIDTH // LANES
    _cast_slabs(cast_in, cast_out)

    @pl.when((pl.program_id(0) == 0) & (s == 0))
    def _():
        w_in_ref[...] = w_in_f32_ref[...].astype(BF16)

    @pl.when(s == 0)
    def _():
        a_ext[:, 0:CONV_HALO, :] = jnp.zeros((n_groups, CONV_HALO, LANES), F32)
        a_ext[:, CONV_HALO + ts:, :] = jnp.zeros((n_groups, CONV_BLK, LANES), F32)
        b_ext[0:LRU_HALO, :] = jnp.zeros((LRU_HALO, B_WIDTH), F32)
        h_carry[...] = jnp.zeros_like(h_carry)

    xb_ref[...] = x_ref[0].astype(BF16)
    ha = _dot(xb_ref[...], w_in_ref[:, 0:2 * A_WIDTH]) + b_in_ref[:, 0:2 * A_WIDTH]
    glu = ha[:, 0:A_WIDTH] * jax.nn.sigmoid(ha[:, A_WIDTH:])
    for g in range(n_groups):
        a_ext[g, CONV_HALO:CONV_HALO + ts, :] = glu[:, g * LANES:(g + 1) * LANES]

    off = CONV_HALO - (CONV_WIDTH - 1)

    for g in range(n_groups):
        def shift_body(i, carry, g=g):
            base = pl.multiple_of(i * CONV_BLK, CONV_BLK)
            win = a_ext[g, pl.ds(base, CONV_BLK + SUBLANES), :]
            for r in range(1, SUBLANES):
                rolled = pltpu.roll(win, CONV_BLK + SUBLANES - r, axis=0)
                a_sh[r - 1, g, pl.ds(base, CONV_BLK), :] = rolled[0:CONV_BLK, :]
            return carry

        lax.fori_loop(0, (ts + CONV_BLK) // CONV_BLK, shift_body, 0, unroll=3)

    for g in range(n_groups):
        sl = slice(g * LANES, (g + 1) * LANES)
        taps = [conv_w_ref[j, :, sl] for j in range(CONV_WIDTH)]
        bias = conv_b_ref[:, sl]

        def conv_body(i, carry, g=g, sl=sl, taps=taps, bias=bias):
            base = pl.multiple_of(i * CONV_BLK, CONV_BLK)
            acc = jnp.broadcast_to(bias, (CONV_BLK, LANES))
            for j in range(CONV_WIDTH):
                q, r = divmod(off + j, SUBLANES)
                rows = pl.ds(base + q * SUBLANES, CONV_BLK)
                win = a_ext[g, rows, :] if r == 0 else a_sh[r - 1, g, rows, :]
                acc = acc + (win.reshape(-1, SUBLANES, LANES) * taps[j][None]).reshape(CONV_BLK, LANES)
            conv_ref[pl.ds(base, CONV_BLK), sl] = acc
            return carry

        lax.fori_loop(0, ts // CONV_BLK, conv_body, 0, unroll=4)
    a_ext[:, 0:CONV_HALO, :] = a_ext[:, ts:ts + CONV_HALO, :]

    def norm_body(i, carry):
        base = pl.multiple_of(i * CONV_ROWS, CONV_ROWS)
        ya = jax.nn.silu(_ln(conv_ref[pl.ds(base, CONV_ROWS), :], cn_g_ref[...], cn_b_ref[...]))
        out_ref[0, pl.ds(base, CONV_ROWS), 0:A_WIDTH] = ya.astype(BF16)
        return carry

    lax.fori_loop(0, ts // CONV_ROWS, norm_body, 0, unroll=True)

    rec0 = 2 * A_WIDTH + B_WIDTH
    b_ext[LRU_HALO:LRU_HALO + ts, :] = _dot(xb_ref[...], w_in_ref[:, rec0:]) + b_in_ref[:, rec0:]
    loff = LRU_HALO - (LRU_CONV_WIDTH - 1)
    xcs = []
    for blk in range(B_WIDTH // MXU_COLS):
        bs = slice(blk * MXU_COLS, (blk + 1) * MXU_COLS)
        xc = jnp.broadcast_to(lconv_b_ref[:, bs], (ts, MXU_COLS))
        for j in range(LRU_CONV_WIDTH):
            xc = xc + lconv_w_ref[j:j + 1, bs] * b_ext[loff + j:loff + j + ts, bs]
        xcs.append(xc)
        xcb = xc.astype(BF16)
        for gate in range(2):
            cs = slice(gate * B_WIDTH + blk * MXU_COLS, gate * B_WIDTH + (blk + 1) * MXU_COLS)
            gates_ref[:, cs] = _dot(xcb, w_gate_ref[bs, cs]) + b_gate_ref[:, cs]
    b_ext[0:LRU_HALO, :] = b_ext[ts:ts + LRU_HALO, :]
    for blk, xc in enumerate(xcs):
        b_ext[LRU_HALO:LRU_HALO + ts, blk * MXU_COLS:(blk + 1) * MXU_COLS] = xc
    bgate_ref[...] = _dot(xb_ref[...], w_in_ref[:, 2 * A_WIDTH:rec0]) + b_in_ref[:, 2 * A_WIDTH:rec0]

    def scan_steps(a_cum, u_cum, axis, steps):
        idx = lax.broadcasted_iota(jnp.int32, a_cum.shape, axis)
        for step in steps:
            keep = idx >= step
            a_prev = jnp.where(keep, pltpu.roll(a_cum, step, axis=axis), 1.0)
            u_prev = jnp.where(keep, pltpu.roll(u_cum, step, axis=axis), 0.0)
            u_cum = a_cum * u_prev + u_cum
            a_cum = a_cum * a_prev
        return a_cum, u_cum

    nblk = ts // SUBLANES
    for g in range(B_WIDTH // LANES):
        sl = slice(g * LANES, (g + 1) * LANES)
        xg = b_ext[LRU_HALO:LRU_HALO + ts, sl]
        gate_r = jax.nn.sigmoid(gates_ref[:, sl])
        gate_i = jax.nn.sigmoid(gates_ref[:, B_WIDTH + g * LANES:B_WIDTH + (g + 1) * LANES])
        lam = lam_ref[:, sl]
        neg = -lam
        softplus = jnp.maximum(neg, 0.0) + jnp.log1p(jnp.exp(-jnp.abs(neg)))
        log_a = (-LRU_C * gate_r) * softplus
        th = jnp.tanh(log_a)
        y = -2.0 * th
        scale = jnp.where(y > 0.0, y * lax.rsqrt(y), 0.0) * lax.rsqrt(1.0 - th)
        a_blk, u_blk = scan_steps(jnp.exp(log_a).reshape(nblk, SUBLANES, LANES),
                                  (scale * (gate_i * xg)).reshape(nblk, SUBLANES, LANES),
                                  1, (1, 2, 4))
        au_ref[0] = a_blk.reshape(ts, LANES)
        au_ref[1] = u_blk.reshape(ts, LANES)
        last = pl.ds(SUBLANES - 1, nblk, stride=SUBLANES)
        a_end, u_end = scan_steps(au_ref[0, last, :], au_ref[1, last, :], 0,
                                  [1 << k for k in range(nblk.bit_length() - 1)])
        h_prev = h_carry[0:1, sl]
        h_end = a_end * h_prev + u_end
        h_carry[:, sl] = jnp.broadcast_to(h_end[nblk - 1:nblk, :], (SUBLANES, LANES))
        row0 = lax.broadcasted_iota(jnp.int32, (nblk, LANES), 0) == 0
        h_in = jnp.where(row0, h_prev, pltpu.roll(h_end, 1, axis=0))
        hg = (a_blk * h_in[:, None, :] + u_blk).reshape(ts, LANES)
        yb = hg * jax.nn.gelu(bgate_ref[:, sl])
        out_ref[0, :, A_WIDTH + g * LANES:A_WIDTH + (g + 1) * LANES] = yb.astype(BF16)


def _mixer_call(x, w_in, b_in, conv_w, conv_b, cn_g, cn_b, lconv_w, lconv_b, w_gate, b_gate, lam,
                cast_weights, cast_layers):
    ts = MIX_TS
    seq_steps = SEQ // ts
    full = lambda a: pl.BlockSpec(a.shape, lambda b, s, nd=a.ndim: (0,) * nd,
                                  pipeline_mode=pl.Buffered(1))
    cast_in, cast_out, cast_shapes = _cast_specs(
        cast_weights, cast_layers, lambda b, s: b * seq_steps + s, BATCH * seq_steps)
    return pl.pallas_call(
        _mixer_kernel,
        grid=(BATCH, seq_steps),
        in_specs=[pl.BlockSpec((1, ts, D_MODEL), lambda b, s: (b, s, 0)),
                  full(w_in), full(b_in), full(conv_w), full(conv_b), full(cn_g), full(cn_b),
                  full(lconv_w), full(lconv_b), full(w_gate), full(b_gate), full(lam)] + cast_in,
        out_specs=[pl.BlockSpec((1, ts, A_WIDTH + B_WIDTH), lambda b, s: (b, s, 0))] + cast_out,
        out_shape=[jax.ShapeDtypeStruct((BATCH, SEQ, A_WIDTH + B_WIDTH), BF16)] + cast_shapes,
        scratch_shapes=[pltpu.VMEM((A_WIDTH // LANES, CONV_HALO + ts + CONV_BLK, LANES), F32),
                        pltpu.VMEM((SUBLANES - 1, A_WIDTH // LANES, ts + CONV_BLK, LANES), F32),
                        pltpu.VMEM((LRU_HALO + ts, B_WIDTH), F32),
                        pltpu.VMEM((SUBLANES, B_WIDTH), F32),
                        pltpu.VMEM((ts, B_WIDTH), F32),
                        pltpu.VMEM((ts, 2 * B_WIDTH), F32),
                        pltpu.VMEM((2, ts, LANES), F32),
                        pltpu.VMEM((ts, D_MODEL), BF16),
                        pltpu.VMEM((ts, A_WIDTH), F32),
                        pltpu.VMEM((D_MODEL, IN_WIDTH), BF16)],
        compiler_params=pltpu.CompilerParams(
            dimension_semantics=("arbitrary", "arbitrary"), vmem_limit_bytes=VMEM_LIMIT),
        name="mixer0",
    )(x, w_in, b_in, conv_w, conv_b, cn_g, cn_b, lconv_w, lconv_b, w_gate, b_gate, lam,
      *cast_weights)


def _post_kernel(m_ref, x_ref, w_out_ref, g1_ref, b1_ref, wg_ref, wu_ref, wd_ref, g2_ref, b2_ref,
                 out_ref, x1_ref, xb_ref, acc_ref):
    y = _dot(m_ref[...], w_out_ref[...])
    x1 = _ln(DN_ALPHA * x_ref[...] + y, g1_ref[...], b1_ref[...])
    x1_ref[...] = x1
    xb_ref[...] = x1.astype(BF16)
    for c in range(D_FF // FF_CHUNK):
        cs = slice(c * FF_CHUNK, (c + 1) * FF_CHUNK)
        gate = _dot(xb_ref[...], wg_ref[:, cs])
        up = _dot(xb_ref[...], wu_ref[:, cs])
        act = (jax.nn.silu(gate) * up).astype(BF16)
        contrib = _dot(act, wd_ref[cs, :])
        if c == 0:
            acc_ref[...] = contrib
        else:
            acc_ref[...] += contrib
    out_ref[...] = _ln(DN_ALPHA * x1_ref[...] + acc_ref[...], g2_ref[...], b2_ref[...])


def _post_call(m, x, w_out, g1, b1, wg, wu, wd, g2, b2):
    tm = POST_TM
    rows = m.shape[0]
    const = lambda i: (0, 0)
    full = lambda a: pl.BlockSpec(a.shape, const, pipeline_mode=pl.Buffered(1))
    return pl.pallas_call(
        _post_kernel,
        grid=(rows // tm,),
        in_specs=[pl.BlockSpec((tm, D_MODEL), lambda i: (i, 0)),
                  pl.BlockSpec((tm, D_MODEL), lambda i: (i, 0)),
                  full(w_out), full(g1), full(b1), full(wg), full(wu), full(wd),
                  full(g2), full(b2)],
        out_specs=pl.BlockSpec((tm, D_MODEL), lambda i: (i, 0)),
        out_shape=jax.ShapeDtypeStruct((rows, D_MODEL), F32),
        scratch_shapes=[pltpu.VMEM((tm, D_MODEL), F32),
                        pltpu.VMEM((tm, D_MODEL), BF16),
                        pltpu.VMEM((tm, D_MODEL), F32)],
        compiler_params=pltpu.CompilerParams(
            dimension_semantics=("arbitrary",), vmem_limit_bytes=VMEM_LIMIT),
        name="post",
    )(m, x, w_out, g1, b1, wg, wu, wd, g2, b2)


def _qkv_kernel(x_ref, w_f32_ref, cos_ref, sin_ref, *rest):
    cast_in, rest = rest[:N_CAST], rest[N_CAST:]
    q_ref, k_ref, v_ref = rest[:3]
    cast_out, (w_ref,) = rest[3:3 + N_CAST], rest[3 + N_CAST:]
    _cast_slabs(cast_in, cast_out)

    @pl.when(pl.program_id(0) == 0)
    def _():
        w_ref[...] = w_f32_ref[...].astype(BF16)

    xb = x_ref[...].astype(BF16)
    cos = cos_ref[...]
    sin = sin_ref[...]
    half_dim = DIFF_HEAD_DIM // 2
    lane = lax.broadcasted_iota(jnp.int32, (1, LANES), 1)
    first_half = (lane % DIFF_HEAD_DIM) < half_dim
    for g in range(2 * QK_WIDTH // MXU_COLS):
        t2 = _dot(xb, w_ref[:, g * MXU_COLS:(g + 1) * MXU_COLS])
        for half in range(MXU_COLS // LANES):
            t = t2[:, half * LANES:(half + 1) * LANES]
            rot = jnp.where(first_half, pltpu.roll(t, LANES - half_dim, axis=1),
                            pltpu.roll(t, half_dim, axis=1))
            r = t * cos + rot * sin
            col = g * MXU_COLS + half * LANES
            if col < QK_WIDTH:
                q_ref[:, col:col + LANES] = (r * Q_SCALE).astype(BF16)
            else:
                k_ref[:, col - QK_WIDTH:col - QK_WIDTH + LANES] = r.astype(BF16)
    v_ref[...] = _dot(xb, w_ref[:, 2 * QK_WIDTH:]).astype(BF16)


def _qkv_call(x, w, cos, sin, cast_weights, cast_layers):
    tm = QKV_TM
    rows = x.shape[0]
    pos_blocks = SEQ // tm
    row_spec = pl.BlockSpec((tm, D_MODEL), lambda i: (i, 0))
    tab_spec = pl.BlockSpec((tm, LANES), lambda i: (i % pos_blocks, 0))
    out = jax.ShapeDtypeStruct((rows, QK_WIDTH), BF16)
    cast_in, cast_out, cast_shapes = _cast_specs(cast_weights, cast_layers, lambda i: i,
                                                 rows // tm)
    return pl.pallas_call(
        _qkv_kernel,
        grid=(rows // tm,),
        in_specs=[row_spec,
                  pl.BlockSpec(w.shape, lambda i: (0, 0), pipeline_mode=pl.Buffered(1)),
                  tab_spec, tab_spec] + cast_in,
        out_specs=[row_spec, row_spec, row_spec] + cast_out,
        out_shape=[out, out, out] + cast_shapes,
        scratch_shapes=[pltpu.VMEM(w.shape, BF16)],
        compiler_params=pltpu.CompilerParams(
            dimension_semantics=("arbitrary",), vmem_limit_bytes=VMEM_LIMIT),
        name="qkv",
    )(x, w, cos, sin, *cast_weights)


def _attn_kernel(lq1_ref, lk1_ref, lq2_ref, lk2_ref, g_ref, q_ref, k_ref, v_ref, o_ref,
                 qs_ref, m_ref, acc_ref, *, lambda_init):
    th = ATT_TH
    tq = 2 * th
    lane = lax.broadcasted_iota(jnp.int32, (1, LANES), 1)
    is_map1 = lane < DIFF_HEAD_DIM
    lam = (jnp.exp(jnp.sum(lq1_ref[...] * lk1_ref[...], axis=-1, keepdims=True))
           - jnp.exp(jnp.sum(lq2_ref[...] * lk2_ref[...], axis=-1, keepdims=True)) + lambda_init)
    gain = g_ref[...] * (1.0 - lambda_init)
    zero = jnp.zeros((), BF16)

    def chunk_mask(n_rows):
        row = lax.broadcasted_iota(jnp.int32, (n_rows, th), 0) % th
        return row // CHUNK >= lax.broadcasted_iota(jnp.int32, (n_rows, th), 1) // CHUNK

    diag = chunk_mask(2 * th)

    def update(h, rows, key_start, n_keys, visible, first=False):
        ks = pl.ds(key_start, n_keys)
        hs = slice(h * LANES, (h + 1) * LANES)
        s = lax.dot_general(qs_ref[h, rows, :], k_ref[0, ks, hs], (((1,), (1,)), ((), ())),
                            preferred_element_type=F32)
        if visible is not None:
            n = visible.shape[0]
            masked = jnp.where(visible, s[:n], NEG_INF)
            s = masked if n == s.shape[0] else jnp.concatenate([masked, s[n:]], axis=0)
        m_new = jnp.broadcast_to(jnp.max(s, axis=-1, keepdims=True), (s.shape[0], LANES))
        if not first:
            m_old = m_ref[h, rows, :]
            m_new = jnp.maximum(m_old, m_new)
            alpha = jnp.exp2(m_old - m_new)
        e = jnp.exp2(s - jnp.concatenate([m_new] * (n_keys // LANES), axis=1)).astype(BF16)
        v_ext = jnp.concatenate([v_ref[0, ks, hs], jnp.ones((n_keys, LANES), BF16)], axis=1)
        pv = _dot(e, v_ext)
        if first:
            acc_ref[h, rows, :] = pv
        else:
            acc_ref[h, rows, :] = jnp.concatenate([alpha, alpha], axis=1) * acc_ref[h, rows, :] + pv
        m_ref[h, rows, :] = m_new

    all_rows = slice(0, 4 * th)
    second_half = slice(2 * th, 4 * th)

    def q_block(i, carry):
        q0 = pl.multiple_of(i * tq, tq)
        for h in range(ATT_HP):
            for half in range(2):
                q = q_ref[0, pl.ds(q0 + half * th, th), h * LANES:(h + 1) * LANES]
                qs_ref[h, (2 * half) * th:(2 * half + 1) * th, :] = jnp.where(is_map1, q, zero)
                qs_ref[h, (2 * half + 1) * th:(2 * half + 2) * th, :] = jnp.where(is_map1, zero, q)
        for h in range(ATT_HP):
            update(h, all_rows, q0, th, diag, first=True)
            update(h, second_half, q0 + th, th, diag)

        def off_diag(j, c):
            for h in range(ATT_HP):
                update(h, all_rows, pl.multiple_of(j * tq, tq), tq, None)
            return c

        lax.fori_loop(0, i, off_diag, 0)

        for h in range(ATT_HP):
            for half in range(2):
                a1 = acc_ref[h, (2 * half) * th:(2 * half + 1) * th, :]
                a2 = acc_ref[h, (2 * half + 1) * th:(2 * half + 2) * th, :]
                o = (a1[:, :LANES] * (1.0 / a1[:, LANES:])
                     - a2[:, :LANES] * (lam * (1.0 / a2[:, LANES:])))
                o = o * lax.rsqrt(jnp.mean(o * o, axis=-1, keepdims=True) + LN_EPS) * gain
                o_ref[0, pl.ds(q0 + half * th, th), h * LANES:(h + 1) * LANES] = o.astype(BF16)
        return carry

    lax.fori_loop(0, SEQ // tq, q_block, 0)


def _attn_call(lq1, lk1, lq2, lk2, g, q, k, v, lambda_init):
    const = lambda b, h: (0, 0)
    small = lambda a: pl.BlockSpec(a.shape, const)
    head_spec = pl.BlockSpec((1, SEQ, ATT_HP * LANES), lambda b, h: (b, 0, h))
    return pl.pallas_call(
        functools.partial(_attn_kernel, lambda_init=lambda_init),
        grid=(BATCH, DIFF_HEADS // ATT_HP),
        in_specs=[small(lq1), small(lk1), small(lq2), small(lk2), small(g),
                  head_spec, head_spec, head_spec],
        out_specs=head_spec,
        out_shape=jax.ShapeDtypeStruct((BATCH, SEQ, V_WIDTH), BF16),
        scratch_shapes=[pltpu.VMEM((ATT_HP, 4 * ATT_TH, LANES), BF16),
                        pltpu.VMEM((ATT_HP, 4 * ATT_TH, LANES), F32),
                        pltpu.VMEM((ATT_HP, 4 * ATT_TH, 2 * LANES), F32)],
        compiler_params=pltpu.CompilerParams(
            dimension_semantics=("arbitrary", "arbitrary"), vmem_limit_bytes=VMEM_LIMIT),
        name="diff_attn",
    )(lq1, lk1, lq2, lk2, g, q, k, v)


def _block_diag(w):
    same_block = np.eye(LRU_BLOCKS, dtype=np.float32)[:, None, :, None]
    return (w[:, :, None, :] * same_block).reshape(B_WIDTH, B_WIDTH)


def _rope_tables():
    half_dim = DIFF_HEAD_DIM // 2
    pos = np.arange(SEQ, dtype=np.float64)
    inv_freq = ROPE_THETA ** (-np.arange(0, DIFF_HEAD_DIM, 2, dtype=np.float64) / DIFF_HEAD_DIM)
    lane = np.arange(LANES)
    ang = pos[:, None] * inv_freq[lane % half_dim][None, :]
    sign = np.where(lane % DIFF_HEAD_DIM < half_dim, -1.0, 1.0)
    return (jnp.asarray(np.cos(ang), dtype=F32), jnp.asarray(np.sin(ang) * sign[None, :], dtype=F32))


def kernel(x, even_w_in, even_b_in, even_conv_w, even_conv_b, even_cnorm_g, even_cnorm_b,
           even_lru_conv_w, even_lru_conv_b, even_w_a, even_b_a, even_w_x, even_b_x,
           even_lru_lambda, even_w_out, odd_w_qkv, odd_lambda_q1, odd_lambda_k1,
           odd_lambda_q2, odd_lambda_k2, odd_subln_g, odd_w_out, mix_ln_g, mix_ln_b,
           ffn_w_gate, ffn_w_up, ffn_w_down, ffn_ln_g, ffn_ln_b):
    row = lambda a: a.reshape(1, -1)
    rows = BATCH * SEQ

    def post(m, xres, weights, layer):
        w_out, wg, wu, wd = weights
        return _post_call(m.reshape(rows, -1), xres.reshape(rows, D_MODEL), w_out,
                          row(mix_ln_g[layer]), row(mix_ln_b[layer]), wg, wu, wd,
                          row(ffn_ln_g[layer]), row(ffn_ln_b[layer]))

    w_gate = jnp.concatenate([_block_diag(even_w_a[0]), _block_diag(even_w_x[0])], axis=1)
    b_gate = jnp.concatenate([even_b_a[0], even_b_x[0]]).reshape(1, -1)
    conv_w = jnp.broadcast_to(even_conv_w[0][:, None, :], (CONV_WIDTH, SUBLANES, A_WIDTH))
    m0, *tail0 = _mixer_call(x, even_w_in[0], row(even_b_in[0]), conv_w,
                             row(even_conv_b[0]), row(even_cnorm_g[0]), row(even_cnorm_b[0]),
                             even_lru_conv_w[0], row(even_lru_conv_b[0]), w_gate.astype(BF16),
                             b_gate, row(even_lru_lambda[0]),
                             [even_w_out, ffn_w_gate, ffn_w_up, ffn_w_down], [0, 0, 0, 0])
    x1 = post(m0, x, tail0, 0)

    lambda_init = 0.8 - 0.6 * math.exp(-0.3 * 1)
    cos, sin = _rope_tables()
    q, k, v, *tail1 = _qkv_call(x1, odd_w_qkv[0], cos, sin,
                                [odd_w_out, ffn_w_gate, ffn_w_up, ffn_w_down], [0, 1, 1, 1])
    shape3 = (BATCH, SEQ, QK_WIDTH)
    o = _attn_call(row(odd_lambda_q1[0]), row(odd_lambda_k1[0]), row(odd_lambda_q2[0]),
                   row(odd_lambda_k2[0]), row(odd_subln_g[0]),
                   q.reshape(shape3), k.reshape(shape3), v.reshape(shape3), lambda_init)
    out = post(o, x1, tail1, 1)
    return out.reshape(BATCH, SEQ, D_MODEL)
```

```python
import functools
import math

import jax
import jax.numpy as jnp
import numpy as np
from jax import lax
from jax.experimental import pallas as pl
from jax.experimental.pallas import tpu as pltpu

F32 = jnp.float32
BF16 = jnp.bfloat16

D_MODEL = 1024
BATCH = 8
SEQ = 2048
DEPTH = 2
CHUNK = 64
A_WIDTH = 512
B_WIDTH = 512
CONV_WIDTH = 31
LRU_BLOCKS = 8
LRU_CONV_WIDTH = 4
LRU_C = 8.0
IN_WIDTH = 2 * A_WIDTH + 2 * B_WIDTH
DIFF_HEADS = 8
DIFF_HEAD_DIM = 64
QK_WIDTH = 1024
V_WIDTH = 1024
ROPE_THETA = 10000.0
D_FF = 2816
LN_EPS = 1e-5
DN_ALPHA = (2 * DEPTH) ** 0.25
NEG_INF = -1e30
Q_SCALE = DIFF_HEAD_DIM ** -0.5 * math.log2(math.e)

LANES = 128
SUBLANES = 8
BF16_SUBLANES = 16
MXU_COLS = 256
VMEM_LIMIT = 56 * 1024 * 1024

MIX_TS = 512
CONV_HALO = 32
CONV_ROWS = 32
CONV_BLK = 64
LRU_HALO = 8
POST_TM = 1024
FF_CHUNK = 256
QKV_TM = 1024
ATT_HP = 8
ATT_TH = 256


def _ln(x, g, b):
    mu = jnp.mean(x, axis=-1, keepdims=True)
    xc = x - mu
    var = jnp.mean(xc * xc, axis=-1, keepdims=True)
    return xc * lax.rsqrt(var + LN_EPS) * g + b


def _dot(a, b):
    return jnp.dot(a, b, preferred_element_type=F32)


N_CAST = 4


def _cast_slabs(cast_in, cast_out):
    for src, dst in zip(cast_in, cast_out):
        dst[...] = src[...].astype(BF16)


def _cast_specs(weights, layers, step_of, n_steps):
    in_specs, out_specs, out_shapes = [], [], []
    for w, layer in zip(weights, layers):
        _, rows, cols = w.shape
        slab = rows // n_steps
        per_slab = 1
        while (slab * per_slab) % BF16_SUBLANES:
            per_slab *= 2
        slab *= per_slab
        in_specs.append(pl.BlockSpec(
            (None, slab, cols), lambda *g, l=layer, p=per_slab: (l, step_of(*g) // p, 0)))
        out_specs.append(pl.BlockSpec(
            (slab, cols), lambda *g, p=per_slab: (step_of(*g) // p, 0)))
        out_shapes.append(jax.ShapeDtypeStruct((rows, cols), BF16))
    return in_specs, out_specs, out_shapes


def _mixer_kernel(x_ref, w_in_f32_ref, b_in_ref, conv_w_ref, conv_b_ref, cn_g_ref, cn_b_ref,
                  lconv_w_ref, lconv_b_ref, w_gate_ref, b_gate_ref, lam_ref, *rest):
    cast_in, rest = rest[:N_CAST], rest[N_CAST:]
    out_ref, rest = rest[0], rest[1:]
    cast_out, rest = rest[:N_CAST], rest[N_CAST:]
    (a_ext, a_sh, b_ext, h_carry, bgate_ref, gates_ref, au_ref, xb_ref, conv_ref,
     w_in_ref) = rest
    ts = MIX_TS
    s = pl.program_id(1)

    n_groups = A_WIDTH // LANES
    _cast_slabs(cast_in, cast_out)

    @pl.when((pl.program_id(0) == 0) & (s == 0))
    def _():
        w_in_ref[...] = w_in_f32_ref[...].astype(BF16)

    @pl.when(s == 0)
    def _():
        a_ext[:, 0:CONV_HALO, :] = jnp.zeros((n_groups, CONV_HALO, LANES), F32)
        a_ext[:, CONV_HALO + ts:, :] = jnp.zeros((n_groups, CONV_BLK, LANES), F32)
        b_ext[0:LRU_HALO, :] = jnp.zeros((LRU_HALO, B_WIDTH), F32)
        h_carry[...] = jnp.zeros_like(h_carry)

    xb_ref[...] = x_ref[0].astype(BF16)
    ha = _dot(xb_ref[...], w_in_ref[:, 0:2 * A_WIDTH]) + b_in_ref[:, 0:2 * A_WIDTH]
    glu = ha[:, 0:A_WIDTH] * jax.nn.sigmoid(ha[:, A_WIDTH:])
    for g in range(n_groups):
        a_ext[g, CONV_HALO:CONV_HALO + ts, :] = glu[:, g * LANES:(g + 1) * LANES]

    off = CONV_HALO - (CONV_WIDTH - 1)

    for g in range(n_groups):
        def shift_body(i, carry, g=g):
            base = pl.multiple_of(i * CONV_BLK, CONV_BLK)
            win = a_ext[g, pl.ds(base, CONV_BLK + SUBLANES), :]
            for r in range(1, SUBLANES):
                rolled = pltpu.roll(win, CONV_BLK + SUBLANES - r, axis=0)
                a_sh[r - 1, g, pl.ds(base, CONV_BLK), :] = rolled[0:CONV_BLK, :]
            return carry

        lax.fori_loop(0, (ts + CONV_BLK) // CONV_BLK, shift_body, 0, unroll=3)

    for g in range(n_groups):
        sl = slice(g * LANES, (g + 1) * LANES)
        taps = [conv_w_ref[j, :, sl] for j in range(CONV_WIDTH)]
        bias = conv_b_ref[:, sl]

        def conv_body(i, carry, g=g, sl=sl, taps=taps, bias=bias):
            base = pl.multiple_of(i * CONV_BLK, CONV_BLK)
            acc = jnp.broadcast_to(bias, (CONV_BLK, LANES))
            for j in range(CONV_WIDTH):
                q, r = divmod(off + j, SUBLANES)
                rows = pl.ds(base + q * SUBLANES, CONV_BLK)
                win = a_ext[g, rows, :] if r == 0 else a_sh[r - 1, g, rows, :]
                acc = acc + (win.reshape(-1, SUBLANES, LANES) * taps[j][None]).reshape(CONV_BLK, LANES)
            conv_ref[pl.ds(base, CONV_BLK), sl] = acc
            return carry

        lax.fori_loop(0, ts // CONV_BLK, conv_body, 0, unroll=4)
    a_ext[:, 0:CONV_HALO, :] = a_ext[:, ts:ts + CONV_HALO, :]

    def norm_body(i, carry):
        base = pl.multiple_of(i * CONV_ROWS, CONV_ROWS)
        ya = jax.nn.silu(_ln(conv_ref[pl.ds(base, CONV_ROWS), :], cn_g_ref[...], cn_b_ref[...]))
        out_ref[0, pl.ds(base, CONV_ROWS), 0:A_WIDTH] = ya.astype(BF16)
        return carry

    lax.fori_loop(0, ts // CONV_ROWS, norm_body, 0, unroll=True)

    rec0 = 2 * A_WIDTH + B_WIDTH
    b_ext[LRU_HALO:LRU_HALO + ts, :] = _dot(xb_ref[...], w_in_ref[:, rec0:]) + b_in_ref[:, rec0:]
    loff = LRU_HALO - (LRU_CONV_WIDTH - 1)
    xcs = []
    for blk in range(B_WIDTH // MXU_COLS):
        bs = slice(blk * MXU_COLS, (blk + 1) * MXU_COLS)
        xc = jnp.broadcast_to(lconv_b_ref[:, bs], (ts, MXU_COLS))
        for j in range(LRU_CONV_WIDTH):
            xc = xc + lconv_w_ref[j:j + 1, bs] * b_ext[loff + j:loff + j + ts, bs]
        xcs.append(xc)
        xcb = xc.astype(BF16)
        for gate in range(2):
            cs = slice(gate * B_WIDTH + blk * MXU_COLS, gate * B_WIDTH + (blk + 1) * MXU_COLS)
            gates_ref[:, cs] = _dot(xcb, w_gate_ref[bs, cs]) + b_gate_ref[:, cs]
    b_ext[0:LRU_HALO, :] = b_ext[ts:ts + LRU_HALO, :]
    for blk, xc in enumerate(xcs):
        b_ext[LRU_HALO:LRU_HALO + ts, blk * MXU_COLS:(blk + 1) * MXU_COLS] = xc
    bgate_ref[...] = _dot(xb_ref[...], w_in_ref[:, 2 * A_WIDTH:rec0]) + b_in_ref[:, 2 * A_WIDTH:rec0]

    def scan_steps(a_cum, u_cum, axis, steps):
        idx = lax.broadcasted_iota(jnp.int32, a_cum.shape, axis)
        for step in steps:
            keep = idx >= step
            a_prev = jnp.where(keep, pltpu.roll(a_cum, step, axis=axis), 1.0)
            u_prev = jnp.where(keep, pltpu.roll(u_cum, step, axis=axis), 0.0)
            u_cum = a_cum * u_prev + u_cum
            a_cum = a_cum * a_prev
        return a_cum, u_cum

    nblk = ts // SUBLANES
    for g in range(B_WIDTH // LANES):
        sl = slice(g * LANES, (g + 1) * LANES)
        xg = b_ext[LRU_HALO:LRU_HALO + ts, sl]
        gate_r = jax.nn.sigmoid(gates_ref[:, sl])
        gate_i = jax.nn.sigmoid(gates_ref[:, B_WIDTH + g * LANES:B_WIDTH + (g + 1) * LANES])
        lam = lam_ref[:, sl]
        neg = -lam
        softplus = jnp.maximum(neg, 0.0) + jnp.log1p(jnp.exp(-jnp.abs(neg)))
        log_a = (-LRU_C * gate_r) * softplus
        th = jnp.tanh(log_a)
        y = -2.0 * th
        scale = jnp.where(y > 0.0, y * lax.rsqrt(y), 0.0) * lax.rsqrt(1.0 - th)
        a_blk, u_blk = scan_steps(jnp.exp(log_a).reshape(nblk, SUBLANES, LANES),
                                  (scale * (gate_i * xg)).reshape(nblk, SUBLANES, LANES),
                                  1, (1, 2, 4))
        au_ref[0] = a_blk.reshape(ts, LANES)
        au_ref[1] = u_blk.reshape(ts, LANES)
        last = pl.ds(SUBLANES - 1, nblk, stride=SUBLANES)
        a_end, u_end = scan_steps(au_ref[0, last, :], au_ref[1, last, :], 0,
                                  [1 << k for k in range(nblk.bit_length() - 1)])
        h_prev = h_carry[0:1, sl]
        h_end = a_end * h_prev + u_end
        h_carry[:, sl] = jnp.broadcast_to(h_end[nblk - 1:nblk, :], (SUBLANES, LANES))
        row0 = lax.broadcasted_iota(jnp.int32, (nblk, LANES), 0) == 0
        h_in = jnp.where(row0, h_prev, pltpu.roll(h_end, 1, axis=0))
        hg = (a_blk * h_in[:, None, :] + u_blk).reshape(ts, LANES)
        yb = hg * jax.nn.gelu(bgate_ref[:, sl])
        out_ref[0, :, A_WIDTH + g * LANES:A_WIDTH + (g + 1) * LANES] = yb.astype(BF16)


def _mixer_call(x, w_in, b_in, conv_w, conv_b, cn_g, cn_b, lconv_w, lconv_b, w_gate, b_gate, lam,
                cast_weights, cast_layers):
    ts = MIX_TS
    seq_steps = SEQ // ts
    full = lambda a: pl.BlockSpec(a.shape, lambda b, s, nd=a.ndim: (0,) * nd,
                                  pipeline_mode=pl.Buffered(1))
    cast_in, cast_out, cast_shapes = _cast_specs(
        cast_weights, cast_layers, lambda b, s: b * seq_steps + s, BATCH * seq_steps)
    return pl.pallas_call(
        _mixer_kernel,
        grid=(BATCH, seq_steps),
        in_specs=[pl.BlockSpec((1, ts, D_MODEL), lambda b, s: (b, s, 0)),
                  full(w_in), full(b_in), full(conv_w), full(conv_b), full(cn_g), full(cn_b),
                  full(lconv_w), full(lconv_b), full(w_gate), full(b_gate), full(lam)] + cast_in,
        out_specs=[pl.BlockSpec((1, ts, A_WIDTH + B_WIDTH), lambda b, s: (b, s, 0))] + cast_out,
        out_shape=[jax.ShapeDtypeStruct((BATCH, SEQ, A_WIDTH + B_WIDTH), BF16)] + cast_shapes,
        scratch_shapes=[pltpu.VMEM((A_WIDTH // LANES, CONV_HALO + ts + CONV_BLK, LANES), F32),
                        pltpu.VMEM((SUBLANES - 1, A_WIDTH // LANES, ts + CONV_BLK, LANES), F32),
                        pltpu.VMEM((LRU_HALO + ts, B_WIDTH), F32),
                        pltpu.VMEM((SUBLANES, B_WIDTH), F32),
                        pltpu.VMEM((ts, B_WIDTH), F32),
                        pltpu.VMEM((ts, 2 * B_WIDTH), F32),
                        pltpu.VMEM((2, ts, LANES), F32),
                        pltpu.VMEM((ts, D_MODEL), BF16),
                        pltpu.VMEM((ts, A_WIDTH), F32),
                        pltpu.VMEM((D_MODEL, IN_WIDTH), BF16)],
        compiler_params=pltpu.CompilerParams(
            dimension_semantics=("arbitrary", "arbitrary"), vmem_limit_bytes=VMEM_LIMIT),
        name="mixer0",
    )(x, w_in, b_in, conv_w, conv_b, cn_g, cn_b, lconv_w, lconv_b, w_gate, b_gate, lam,
      *cast_weights)


def _post_kernel(m_ref, x_ref, w_out_ref, g1_ref, b1_ref, wg_ref, wu_ref, wd_ref, g2_ref, b2_ref,
                 out_ref, x1_ref, xb_ref, act_ref):
    y = _dot(m_ref[...], w_out_ref[...])
    x1 = _ln(DN_ALPHA * x_ref[...] + y, g1_ref[...], b1_ref[...])
    x1_ref[...] = x1
    xb_ref[...] = x1.astype(BF16)
    for c in range(D_FF // FF_CHUNK):
        cs = slice(c * FF_CHUNK, (c + 1) * FF_CHUNK)
        gate = _dot(xb_ref[...], wg_ref[:, cs])
        up = _dot(xb_ref[...], wu_ref[:, cs])
        act_ref[:, cs] = (jax.nn.silu(gate) * up).astype(BF16)
    ffn = _dot(act_ref[...], wd_ref[...])
    out_ref[...] = _ln(DN_ALPHA * x1_ref[...] + ffn, g2_ref[...], b2_ref[...])


def _post_call(m, x, w_out, g1, b1, wg, wu, wd, g2, b2):
    tm = POST_TM
    rows = m.shape[0]
    const = lambda i: (0, 0)
    full = lambda a: pl.BlockSpec(a.shape, const, pipeline_mode=pl.Buffered(1))
    return pl.pallas_call(
        _post_kernel,
        grid=(rows // tm,),
        in_specs=[pl.BlockSpec((tm, D_MODEL), lambda i: (i, 0)),
                  pl.BlockSpec((tm, D_MODEL), lambda i: (i, 0)),
                  full(w_out), full(g1), full(b1), full(wg), full(wu), full(wd),
                  full(g2), full(b2)],
        out_specs=pl.BlockSpec((tm, D_MODEL), lambda i: (i, 0)),
        out_shape=jax.ShapeDtypeStruct((rows, D_MODEL), F32),
        scratch_shapes=[pltpu.VMEM((tm, D_MODEL), F32),
                        pltpu.VMEM((tm, D_MODEL), BF16),
                        pltpu.VMEM((tm, D_FF), BF16)],
        compiler_params=pltpu.CompilerParams(
            dimension_semantics=("arbitrary",), vmem_limit_bytes=VMEM_LIMIT),
        name="post",
    )(m, x, w_out, g1, b1, wg, wu, wd, g2, b2)


def _qkv_kernel(x_ref, w_f32_ref, cos_ref, sin_ref, *rest):
    cast_in, rest = rest[:N_CAST], rest[N_CAST:]
    q_ref, k_ref, v_ref = rest[:3]
    cast_out, (w_ref,) = rest[3:3 + N_CAST], rest[3 + N_CAST:]
    _cast_slabs(cast_in, cast_out)

    @pl.when(pl.program_id(0) == 0)
    def _():
        w_ref[...] = w_f32_ref[...].astype(BF16)

    xb = x_ref[...].astype(BF16)
    cos = cos_ref[...]
    sin = sin_ref[...]
    half_dim = DIFF_HEAD_DIM // 2
    lane = lax.broadcasted_iota(jnp.int32, (1, LANES), 1)
    first_half = (lane % DIFF_HEAD_DIM) < half_dim
    for g in range(2 * QK_WIDTH // MXU_COLS):
        t2 = _dot(xb, w_ref[:, g * MXU_COLS:(g + 1) * MXU_COLS])
        for half in range(MXU_COLS // LANES):
            t = t2[:, half * LANES:(half + 1) * LANES]
            rot = jnp.where(first_half, pltpu.roll(t, LANES - half_dim, axis=1),
                            pltpu.roll(t, half_dim, axis=1))
            r = t * cos + rot * sin
            col = g * MXU_COLS + half * LANES
            if col < QK_WIDTH:
                q_ref[:, col:col + LANES] = (r * Q_SCALE).astype(BF16)
            else:
                k_ref[:, col - QK_WIDTH:col - QK_WIDTH + LANES] = r.astype(BF16)
    v_ref[...] = _dot(xb, w_ref[:, 2 * QK_WIDTH:]).astype(BF16)


def _qkv_call(x, w, cos, sin, cast_weights, cast_layers):
    tm = QKV_TM
    rows = x.shape[0]
    pos_blocks = SEQ // tm
    row_spec = pl.BlockSpec((tm, D_MODEL), lambda i: (i, 0))
    tab_spec = pl.BlockSpec((tm, LANES), lambda i: (i % pos_blocks, 0))
    out = jax.ShapeDtypeStruct((rows, QK_WIDTH), BF16)
    cast_in, cast_out, cast_shapes = _cast_specs(cast_weights, cast_layers, lambda i: i,
                                                 rows // tm)
    return pl.pallas_call(
        _qkv_kernel,
        grid=(rows // tm,),
        in_specs=[row_spec,
                  pl.BlockSpec(w.shape, lambda i: (0, 0), pipeline_mode=pl.Buffered(1)),
                  tab_spec, tab_spec] + cast_in,
        out_specs=[row_spec, row_spec, row_spec] + cast_out,
        out_shape=[out, out, out] + cast_shapes,
        scratch_shapes=[pltpu.VMEM(w.shape, BF16)],
        compiler_params=pltpu.CompilerParams(
            dimension_semantics=("arbitrary",), vmem_limit_bytes=VMEM_LIMIT),
        name="qkv",
    )(x, w, cos, sin, *cast_weights)


def _attn_kernel(lq1_ref, lk1_ref, lq2_ref, lk2_ref, g_ref, q_ref, k_ref, v_ref, o_ref,
                 qs_ref, m_ref, acc_ref, *, lambda_init):
    th = ATT_TH
    tq = 2 * th
    lane = lax.broadcasted_iota(jnp.int32, (1, LANES), 1)
    is_map1 = lane < DIFF_HEAD_DIM
    lam = (jnp.exp(jnp.sum(lq1_ref[...] * lk1_ref[...], axis=-1, keepdims=True))
           - jnp.exp(jnp.sum(lq2_ref[...] * lk2_ref[...], axis=-1, keepdims=True)) + lambda_init)
    gain = g_ref[...] * (1.0 - lambda_init)
    zero = jnp.zeros((), BF16)

    def chunk_mask(n_rows):
        row = lax.broadcasted_iota(jnp.int32, (n_rows, th), 0) % th
        return row // CHUNK >= lax.broadcasted_iota(jnp.int32, (n_rows, th), 1) // CHUNK

    diag = chunk_mask(2 * th)

    def update(h, rows, key_start, n_keys, visible, first=False):
        ks = pl.ds(key_start, n_keys)
        hs = slice(h * LANES, (h + 1) * LANES)
        s = lax.dot_general(qs_ref[h, rows, :], k_ref[0, ks, hs], (((1,), (1,)), ((), ())),
                            preferred_element_type=F32)
        if visible is not None:
            n = visible.shape[0]
            masked = jnp.where(visible, s[:n], NEG_INF)
            s = masked if n == s.shape[0] else jnp.concatenate([masked, s[n:]], axis=0)
        m_new = jnp.broadcast_to(jnp.max(s, axis=-1, keepdims=True), (s.shape[0], LANES))
        if not first:
            m_old = m_ref[h, rows, :]
            m_new = jnp.maximum(m_old, m_new)
            alpha = jnp.exp2(m_old - m_new)
        e = jnp.exp2(s - jnp.concatenate([m_new] * (n_keys // LANES), axis=1)).astype(BF16)
        v_ext = jnp.concatenate([v_ref[0, ks, hs], jnp.ones((n_keys, LANES), BF16)], axis=1)
        pv = _dot(e, v_ext)
        if first:
            acc_ref[h, rows, :] = pv
        else:
            acc_ref[h, rows, :] = jnp.concatenate([alpha, alpha], axis=1) * acc_ref[h, rows, :] + pv
        m_ref[h, rows, :] = m_new

    all_rows = slice(0, 4 * th)
    second_half = slice(2 * th, 4 * th)

    def q_block(i, carry):
        q0 = pl.multiple_of(i * tq, tq)
        for h in range(ATT_HP):
            for half in range(2):
                q = q_ref[0, pl.ds(q0 + half * th, th), h * LANES:(h + 1) * LANES]
                qs_ref[h, (2 * half) * th:(2 * half + 1) * th, :] = jnp.where(is_map1, q, zero)
                qs_ref[h, (2 * half + 1) * th:(2 * half + 2) * th, :] = jnp.where(is_map1, zero, q)
        for h in range(ATT_HP):
            update(h, all_rows, q0, th, diag, first=True)
            update(h, second_half, q0 + th, th, diag)

        def off_diag(j, c):
            for h in range(ATT_HP):
                update(h, all_rows, pl.multiple_of(j * tq, tq), tq, None)
            return c

        lax.fori_loop(0, i, off_diag, 0)

        for h in range(ATT_HP):
            for half in range(2):
                a1 = acc_ref[h, (2 * half) * th:(2 * half + 1) * th, :]
                a2 = acc_ref[h, (2 * half + 1) * th:(2 * half + 2) * th, :]
                o = (a1[:, :LANES] * (1.0 / a1[:, LANES:])
                     - a2[:, :LANES] * (lam * (1.0 / a2[:, LANES:])))
                o = o * lax.rsqrt(jnp.mean(o * o, axis=-1, keepdims=True) + LN_EPS) * gain
                o_ref[0, pl.ds(q0 + half * th, th), h * LANES:(h + 1) * LANES] = o.astype(BF16)
        return carry

    lax.fori_loop(0, SEQ // tq, q_block, 0)


def _attn_call(lq1, lk1, lq2, lk2, g, q, k, v, lambda_init):
    const = lambda b, h: (0, 0)
    small = lambda a: pl.BlockSpec(a.shape, const)
    head_spec = pl.BlockSpec((1, SEQ, ATT_HP * LANES), lambda b, h: (b, 0, h))
    return pl.pallas_call(
        functools.partial(_attn_kernel, lambda_init=lambda_init),
        grid=(BATCH, DIFF_HEADS // ATT_HP),
        in_specs=[small(lq1), small(lk1), small(lq2), small(lk2), small(g),
                  head_spec, head_spec, head_spec],
        out_specs=head_spec,
        out_shape=jax.ShapeDtypeStruct((BATCH, SEQ, V_WIDTH), BF16),
        scratch_shapes=[pltpu.VMEM((ATT_HP, 4 * ATT_TH, LANES), BF16),
                        pltpu.VMEM((ATT_HP, 4 * ATT_TH, LANES), F32),
                        pltpu.VMEM((ATT_HP, 4 * ATT_TH, 2 * LANES), F32)],
        compiler_params=pltpu.CompilerParams(
            dimension_semantics=("arbitrary", "arbitrary"), vmem_limit_bytes=VMEM_LIMIT),
        name="diff_attn",
    )(lq1, lk1, lq2, lk2, g, q, k, v)


def _block_diag(w):
    same_block = np.eye(LRU_BLOCKS, dtype=np.float32)[:, None, :, None]
    return (w[:, :, None, :] * same_block).reshape(B_WIDTH, B_WIDTH)


def _rope_tables():
    half_dim = DIFF_HEAD_DIM // 2
    pos = np.arange(SEQ, dtype=np.float64)
    inv_freq = ROPE_THETA ** (-np.arange(0, DIFF_HEAD_DIM, 2, dtype=np.float64) / DIFF_HEAD_DIM)
    lane = np.arange(LANES)
    ang = pos[:, None] * inv_freq[lane % half_dim][None, :]
    sign = np.where(lane % DIFF_HEAD_DIM < half_dim, -1.0, 1.0)
    return (jnp.asarray(np.cos(ang), dtype=F32), jnp.asarray(np.sin(ang) * sign[None, :], dtype=F32))


def kernel(x, even_w_in, even_b_in, even_conv_w, even_conv_b, even_cnorm_g, even_cnorm_b,
           even_lru_conv_w, even_lru_conv_b, even_w_a, even_b_a, even_w_x, even_b_x,
           even_lru_lambda, even_w_out, odd_w_qkv, odd_lambda_q1, odd_lambda_k1,
           odd_lambda_q2, odd_lambda_k2, odd_subln_g, odd_w_out, mix_ln_g, mix_ln_b,
           ffn_w_gate, ffn_w_up, ffn_w_down, ffn_ln_g, ffn_ln_b):
    row = lambda a: a.reshape(1, -1)
    rows = BATCH * SEQ

    def post(m, xres, weights, layer):
        w_out, wg, wu, wd = weights
        return _post_call(m.reshape(rows, -1), xres.reshape(rows, D_MODEL), w_out,
                          row(mix_ln_g[layer]), row(mix_ln_b[layer]), wg, wu, wd,
                          row(ffn_ln_g[layer]), row(ffn_ln_b[layer]))

    w_gate = jnp.concatenate([_block_diag(even_w_a[0]), _block_diag(even_w_x[0])], axis=1)
    b_gate = jnp.concatenate([even_b_a[0], even_b_x[0]]).reshape(1, -1)
    conv_w = jnp.broadcast_to(even_conv_w[0][:, None, :], (CONV_WIDTH, SUBLANES, A_WIDTH))
    m0, *tail0 = _mixer_call(x, even_w_in[0], row(even_b_in[0]), conv_w,
                             row(even_conv_b[0]), row(even_cnorm_g[0]), row(even_cnorm_b[0]),
                             even_lru_conv_w[0], row(even_lru_conv_b[0]), w_gate.astype(BF16),
                             b_gate, row(even_lru_lambda[0]),
                             [even_w_out, ffn_w_gate, ffn_w_up, ffn_w_down], [0, 0, 0, 0])
    x1 = post(m0, x, tail0, 0)

    lambda_init = 0.8 - 0.6 * math.exp(-0.3 * 1)
    cos, sin = _rope_tables()
    q, k, v, *tail1 = _qkv_call(x1, odd_w_qkv[0], cos, sin,
                                [odd_w_out, ffn_w_gate, ffn_w_up, ffn_w_down], [0, 1, 1, 1])
    shape3 = (BATCH, SEQ, QK_WIDTH)
    o = _attn_call(row(odd_lambda_q1[0]), row(odd_lambda_k1[0]), row(odd_lambda_q2[0]),
                   row(odd_lambda_k2[0]), row(odd_subln_g[0]),
                   q.reshape(shape3), k.reshape(shape3), v.reshape(shape3), lambda_init)
    out = post(o, x1, tail1, 1)
    return out.reshape(BATCH, SEQ, D_MODEL)
```

```python
import functools
import math

import jax
import jax.numpy as jnp
import numpy as np
from jax import lax
from jax.experimental import pallas as pl
from jax.experimental.pallas import tpu as pltpu

F32 = jnp.float32
BF16 = jnp.bfloat16

D_MODEL = 1024
BATCH = 8
SEQ = 2048
DEPTH = 2
CHUNK = 64
A_WIDTH = 512
B_WIDTH = 512
CONV_WIDTH = 31
LRU_BLOCKS = 8
LRU_CONV_WIDTH = 4
LRU_C = 8.0
IN_WIDTH = 2 * A_WIDTH + 2 * B_WIDTH
DIFF_HEADS = 8
DIFF_HEAD_DIM = 64
QK_WIDTH = 1024
V_WIDTH = 1024
ROPE_THETA = 10000.0
D_FF = 2816
LN_EPS = 1e-5
DN_ALPHA = (2 * DEPTH) ** 0.25
NEG_INF = -1e30
Q_SCALE = DIFF_HEAD_DIM ** -0.5 * math.log2(math.e)

LANES = 128
SUBLANES = 8
BF16_SUBLANES = 16
MXU_COLS = 256
VMEM_LIMIT = 56 * 1024 * 1024

MIX_TS = 512
CONV_HALO = 32
CONV_ROWS = 32
CONV_BLK = 64
LRU_HALO = 8
POST_TM = 1024
FF_CHUNK = 256
POST_ROW_BLOCKS = 4
QKV_TM = 1024
ATT_HP = 8
ATT_TH = 256


def _ln(x, g, b):
    mu = jnp.mean(x, axis=-1, keepdims=True)
    xc = x - mu
    var = jnp.mean(xc * xc, axis=-1, keepdims=True)
    return xc * lax.rsqrt(var + LN_EPS) * g + b


def _dot(a, b):
    return jnp.dot(a, b, preferred_element_type=F32)


N_CAST = 4


def _cast_slabs(cast_in, cast_out):
    for src, dst in zip(cast_in, cast_out):
        dst[...] = src[...].astype(BF16)


def _cast_specs(weights, layers, step_of, n_steps):
    in_specs, out_specs, out_shapes = [], [], []
    for w, layer in zip(weights, layers):
        _, rows, cols = w.shape
        slab = rows // n_steps
        per_slab = 1
        while (slab * per_slab) % BF16_SUBLANES:
            per_slab *= 2
        slab *= per_slab
        in_specs.append(pl.BlockSpec(
            (None, slab, cols), lambda *g, l=layer, p=per_slab: (l, step_of(*g) // p, 0)))
        out_specs.append(pl.BlockSpec(
            (slab, cols), lambda *g, p=per_slab: (step_of(*g) // p, 0)))
        out_shapes.append(jax.ShapeDtypeStruct((rows, cols), BF16))
    return in_specs, out_specs, out_shapes


def _mixer_kernel(x_ref, w_in_f32_ref, b_in_ref, conv_w_ref, conv_b_ref, cn_g_ref, cn_b_ref,
                  lconv_w_ref, lconv_b_ref, w_gate_ref, b_gate_ref, lam_ref, *rest):
    cast_in, rest = rest[:N_CAST], rest[N_CAST:]
    out_ref, rest = rest[0], rest[1:]
    cast_out, rest = rest[:N_CAST], rest[N_CAST:]
    (a_ext, a_sh, b_ext, h_carry, bgate_ref, gates_ref, au_ref, xb_ref, conv_ref,
     w_in_ref) = rest
    ts = MIX_TS
    s = pl.program_id(1)

    n_groups = A_WIDTH // LANES
    _cast_slabs(cast_in, cast_out)

    @pl.when((pl.program_id(0) == 0) & (s == 0))
    def _():
        w_in_ref[...] = w_in_f32_ref[...].astype(BF16)

    @pl.when(s == 0)
    def _():
        a_ext[:, 0:CONV_HALO, :] = jnp.zeros((n_groups, CONV_HALO, LANES), F32)
        a_ext[:, CONV_HALO + ts:, :] = jnp.zeros((n_groups, CONV_BLK, LANES), F32)
        b_ext[0:LRU_HALO, :] = jnp.zeros((LRU_HALO, B_WIDTH), F32)
        h_carry[...] = jnp.zeros_like(h_carry)

    xb_ref[...] = x_ref[0].astype(BF16)
    ha = _dot(xb_ref[...], w_in_ref[:, 0:2 * A_WIDTH]) + b_in_ref[:, 0:2 * A_WIDTH]
    glu = ha[:, 0:A_WIDTH] * jax.nn.sigmoid(ha[:, A_WIDTH:])
    for g in range(n_groups):
        a_ext[g, CONV_HALO:CONV_HALO + ts, :] = glu[:, g * LANES:(g + 1) * LANES]

    off = CONV_HALO - (CONV_WIDTH - 1)

    for g in range(n_groups):
        def shift_body(i, carry, g=g):
            base = pl.multiple_of(i * CONV_BLK, CONV_BLK)
            win = a_ext[g, pl.ds(base, CONV_BLK + SUBLANES), :]
            for r in range(1, SUBLANES):
                rolled = pltpu.roll(win, CONV_BLK + SUBLANES - r, axis=0)
                a_sh[r - 1, g, pl.ds(base, CONV_BLK), :] = rolled[0:CONV_BLK, :]
            return carry

        lax.fori_loop(0, (ts + CONV_BLK) // CONV_BLK, shift_body, 0, unroll=3)

    for g in range(n_groups):
        sl = slice(g * LANES, (g + 1) * LANES)
        taps = [conv_w_ref[j, :, sl] for j in range(CONV_WIDTH)]
        bias = conv_b_ref[:, sl]

        def conv_body(i, carry, g=g, sl=sl, taps=taps, bias=bias):
            base = pl.multiple_of(i * CONV_BLK, CONV_BLK)
            acc = jnp.broadcast_to(bias, (CONV_BLK, LANES))
            for j in range(CONV_WIDTH):
                q, r = divmod(off + j, SUBLANES)
                rows = pl.ds(base + q * SUBLANES, CONV_BLK)
                win = a_ext[g, rows, :] if r == 0 else a_sh[r - 1, g, rows, :]
                acc = acc + (win.reshape(-1, SUBLANES, LANES) * taps[j][None]).reshape(CONV_BLK, LANES)
            conv_ref[pl.ds(base, CONV_BLK), sl] = acc
            return carry

        lax.fori_loop(0, ts // CONV_BLK, conv_body, 0, unroll=4)
    a_ext[:, 0:CONV_HALO, :] = a_ext[:, ts:ts + CONV_HALO, :]

    def norm_body(i, carry):
        base = pl.multiple_of(i * CONV_ROWS, CONV_ROWS)
        ya = jax.nn.silu(_ln(conv_ref[pl.ds(base, CONV_ROWS), :], cn_g_ref[...], cn_b_ref[...]))
        out_ref[0, pl.ds(base, CONV_ROWS), 0:A_WIDTH] = ya.astype(BF16)
        return carry

    lax.fori_loop(0, ts // CONV_ROWS, norm_body, 0, unroll=True)

    rec0 = 2 * A_WIDTH + B_WIDTH
    b_ext[LRU_HALO:LRU_HALO + ts, :] = _dot(xb_ref[...], w_in_ref[:, rec0:]) + b_in_ref[:, rec0:]
    loff = LRU_HALO - (LRU_CONV_WIDTH - 1)
    xcs = []
    for blk in range(B_WIDTH // MXU_COLS):
        bs = slice(blk * MXU_COLS, (blk + 1) * MXU_COLS)
        xc = jnp.broadcast_to(lconv_b_ref[:, bs], (ts, MXU_COLS))
        for j in range(LRU_CONV_WIDTH):
            xc = xc + lconv_w_ref[j:j + 1, bs] * b_ext[loff + j:loff + j + ts, bs]
        xcs.append(xc)
        xcb = xc.astype(BF16)
        for gate in range(2):
            cs = slice(gate * B_WIDTH + blk * MXU_COLS, gate * B_WIDTH + (blk + 1) * MXU_COLS)
            gates_ref[:, cs] = _dot(xcb, w_gate_ref[bs, cs]) + b_gate_ref[:, cs]
    b_ext[0:LRU_HALO, :] = b_ext[ts:ts + LRU_HALO, :]
    for blk, xc in enumerate(xcs):
        b_ext[LRU_HALO:LRU_HALO + ts, blk * MXU_COLS:(blk + 1) * MXU_COLS] = xc
    bgate_ref[...] = _dot(xb_ref[...], w_in_ref[:, 2 * A_WIDTH:rec0]) + b_in_ref[:, 2 * A_WIDTH:rec0]

    def scan_steps(a_cum, u_cum, axis, steps):
        idx = lax.broadcasted_iota(jnp.int32, a_cum.shape, axis)
        for step in steps:
            keep = idx >= step
            a_prev = jnp.where(keep, pltpu.roll(a_cum, step, axis=axis), 1.0)
            u_prev = jnp.where(keep, pltpu.roll(u_cum, step, axis=axis), 0.0)
            u_cum = a_cum * u_prev + u_cum
            a_cum = a_cum * a_prev
        return a_cum, u_cum

    nblk = ts // SUBLANES
    for g in range(B_WIDTH // LANES):
        sl = slice(g * LANES, (g + 1) * LANES)
        xg = b_ext[LRU_HALO:LRU_HALO + ts, sl]
        gate_r = jax.nn.sigmoid(gates_ref[:, sl])
        gate_i = jax.nn.sigmoid(gates_ref[:, B_WIDTH + g * LANES:B_WIDTH + (g + 1) * LANES])
        lam = lam_ref[:, sl]
        neg = -lam
        softplus = jnp.maximum(neg, 0.0) + jnp.log1p(jnp.exp(-jnp.abs(neg)))
        log_a = (-LRU_C * gate_r) * softplus
        th = jnp.tanh(log_a)
        y = -2.0 * th
        scale = jnp.where(y > 0.0, y * lax.rsqrt(y), 0.0) * lax.rsqrt(1.0 - th)
        a_blk, u_blk = scan_steps(jnp.exp(log_a).reshape(nblk, SUBLANES, LANES),
                                  (scale * (gate_i * xg)).reshape(nblk, SUBLANES, LANES),
                                  1, (1, 2, 4))
        au_ref[0] = a_blk.reshape(ts, LANES)
        au_ref[1] = u_blk.reshape(ts, LANES)
        last = pl.ds(SUBLANES - 1, nblk, stride=SUBLANES)
        a_end, u_end = scan_steps(au_ref[0, last, :], au_ref[1, last, :], 0,
                                  [1 << k for k in range(nblk.bit_length() - 1)])
        h_prev = h_carry[0:1, sl]
        h_end = a_end * h_prev + u_end
        h_carry[:, sl] = jnp.broadcast_to(h_end[nblk - 1:nblk, :], (SUBLANES, LANES))
        row0 = lax.broadcasted_iota(jnp.int32, (nblk, LANES), 0) == 0
        h_in = jnp.where(row0, h_prev, pltpu.roll(h_end, 1, axis=0))
        hg = (a_blk * h_in[:, None, :] + u_blk).reshape(ts, LANES)
        yb = hg * jax.nn.gelu(bgate_ref[:, sl])
        out_ref[0, :, A_WIDTH + g * LANES:A_WIDTH + (g + 1) * LANES] = yb.astype(BF16)


def _mixer_call(x, w_in, b_in, conv_w, conv_b, cn_g, cn_b, lconv_w, lconv_b, w_gate, b_gate, lam,
                cast_weights, cast_layers):
    ts = MIX_TS
    seq_steps = SEQ // ts
    full = lambda a: pl.BlockSpec(a.shape, lambda b, s, nd=a.ndim: (0,) * nd,
                                  pipeline_mode=pl.Buffered(1))
    cast_in, cast_out, cast_shapes = _cast_specs(
        cast_weights, cast_layers, lambda b, s: b * seq_steps + s, BATCH * seq_steps)
    return pl.pallas_call(
        _mixer_kernel,
        grid=(BATCH, seq_steps),
        in_specs=[pl.BlockSpec((1, ts, D_MODEL), lambda b, s: (b, s, 0)),
                  full(w_in), full(b_in), full(conv_w), full(conv_b), full(cn_g), full(cn_b),
                  full(lconv_w), full(lconv_b), full(w_gate), full(b_gate), full(lam)] + cast_in,
        out_specs=[pl.BlockSpec((1, ts, A_WIDTH + B_WIDTH), lambda b, s: (b, s, 0))] + cast_out,
        out_shape=[jax.ShapeDtypeStruct((BATCH, SEQ, A_WIDTH + B_WIDTH), BF16)] + cast_shapes,
        scratch_shapes=[pltpu.VMEM((A_WIDTH // LANES, CONV_HALO + ts + CONV_BLK, LANES), F32),
                        pltpu.VMEM((SUBLANES - 1, A_WIDTH // LANES, ts + CONV_BLK, LANES), F32),
                        pltpu.VMEM((LRU_HALO + ts, B_WIDTH), F32),
                        pltpu.VMEM((SUBLANES, B_WIDTH), F32),
                        pltpu.VMEM((ts, B_WIDTH), F32),
                        pltpu.VMEM((ts, 2 * B_WIDTH), F32),
                        pltpu.VMEM((2, ts, LANES), F32),
                        pltpu.VMEM((ts, D_MODEL), BF16),
                        pltpu.VMEM((ts, A_WIDTH), F32),
                        pltpu.VMEM((D_MODEL, IN_WIDTH), BF16)],
        compiler_params=pltpu.CompilerParams(
            dimension_semantics=("arbitrary", "arbitrary"), vmem_limit_bytes=VMEM_LIMIT),
        name="mixer0",
    )(x, w_in, b_in, conv_w, conv_b, cn_g, cn_b, lconv_w, lconv_b, w_gate, b_gate, lam,
      *cast_weights)


def _post_kernel(m_ref, x_ref, w_out_ref, g1_ref, b1_ref, wg_ref, wu_ref, wd_ref, g2_ref, b2_ref,
                 out_ref, x1_ref, xb_ref, act_ref):
    blk = POST_TM // POST_ROW_BLOCKS
    for rb in range(POST_ROW_BLOCKS):
        rs = slice(rb * blk, (rb + 1) * blk)
        y = _dot(m_ref[rs, :], w_out_ref[...])
        x1 = _ln(DN_ALPHA * x_ref[rs, :] + y, g1_ref[...], b1_ref[...])
        x1_ref[rs, :] = x1
        xb_ref[rs, :] = x1.astype(BF16)
    for c in range(D_FF // FF_CHUNK):
        cs = slice(c * FF_CHUNK, (c + 1) * FF_CHUNK)
        gate = _dot(xb_ref[...], wg_ref[:, cs])
        up = _dot(xb_ref[...], wu_ref[:, cs])
        act_ref[:, cs] = (jax.nn.silu(gate) * up).astype(BF16)
    for rb in range(POST_ROW_BLOCKS):
        rs = slice(rb * blk, (rb + 1) * blk)
        ffn = _dot(act_ref[rs, :], wd_ref[...])
        out_ref[rs, :] = _ln(DN_ALPHA * x1_ref[rs, :] + ffn, g2_ref[...], b2_ref[...])


def _post_call(m, x, w_out, g1, b1, wg, wu, wd, g2, b2):
    tm = POST_TM
    rows = m.shape[0]
    const = lambda i: (0, 0)
    full = lambda a: pl.BlockSpec(a.shape, const, pipeline_mode=pl.Buffered(1))
    return pl.pallas_call(
        _post_kernel,
        grid=(rows // tm,),
        in_specs=[pl.BlockSpec((tm, D_MODEL), lambda i: (i, 0)),
                  pl.BlockSpec((tm, D_MODEL), lambda i: (i, 0)),
                  full(w_out), full(g1), full(b1), full(wg), full(wu), full(wd),
                  full(g2), full(b2)],
        out_specs=pl.BlockSpec((tm, D_MODEL), lambda i: (i, 0)),
        out_shape=jax.ShapeDtypeStruct((rows, D_MODEL), F32),
        scratch_shapes=[pltpu.VMEM((tm, D_MODEL), F32),
                        pltpu.VMEM((tm, D_MODEL), BF16),
                        pltpu.VMEM((tm, D_FF), BF16)],
        compiler_params=pltpu.CompilerParams(
            dimension_semantics=("arbitrary",), vmem_limit_bytes=VMEM_LIMIT),
        name="post",
    )(m, x, w_out, g1, b1, wg, wu, wd, g2, b2)


def _qkv_kernel(x_ref, w_f32_ref, cos_ref, sin_ref, *rest):
    cast_in, rest = rest[:N_CAST], rest[N_CAST:]
    q_ref, k_ref, v_ref = rest[:3]
    cast_out, (w_ref,) = rest[3:3 + N_CAST], rest[3 + N_CAST:]
    _cast_slabs(cast_in, cast_out)

    @pl.when(pl.program_id(0) == 0)
    def _():
        w_ref[...] = w_f32_ref[...].astype(BF16)

    xb = x_ref[...].astype(BF16)
    cos = cos_ref[...]
    sin = sin_ref[...]
    half_dim = DIFF_HEAD_DIM // 2
    lane = lax.broadcasted_iota(jnp.int32, (1, LANES), 1)
    first_half = (lane % DIFF_HEAD_DIM) < half_dim
    for g in range(2 * QK_WIDTH // MXU_COLS):
        t2 = _dot(xb, w_ref[:, g * MXU_COLS:(g + 1) * MXU_COLS])
        for half in range(MXU_COLS // LANES):
            t = t2[:, half * LANES:(half + 1) * LANES]
            rot = jnp.where(first_half, pltpu.roll(t, LANES - half_dim, axis=1),
                            pltpu.roll(t, half_dim, axis=1))
            r = t * cos + rot * sin
            col = g * MXU_COLS + half * LANES
            if col < QK_WIDTH:
                q_ref[:, col:col + LANES] = (r * Q_SCALE).astype(BF16)
            else:
                k_ref[:, col - QK_WIDTH:col - QK_WIDTH + LANES] = r.astype(BF16)
    v_ref[...] = _dot(xb, w_ref[:, 2 * QK_WIDTH:]).astype(BF16)


def _qkv_call(x, w, cos, sin, cast_weights, cast_layers):
    tm = QKV_TM
    rows = x.shape[0]
    pos_blocks = SEQ // tm
    row_spec = pl.BlockSpec((tm, D_MODEL), lambda i: (i, 0))
    tab_spec = pl.BlockSpec((tm, LANES), lambda i: (i % pos_blocks, 0))
    out = jax.ShapeDtypeStruct((rows, QK_WIDTH), BF16)
    cast_in, cast_out, cast_shapes = _cast_specs(cast_weights, cast_layers, lambda i: i,
                                                 rows // tm)
    return pl.pallas_call(
        _qkv_kernel,
        grid=(rows // tm,),
        in_specs=[row_spec,
                  pl.BlockSpec(w.shape, lambda i: (0, 0), pipeline_mode=pl.Buffered(1)),
                  tab_spec, tab_spec] + cast_in,
        out_specs=[row_spec, row_spec, row_spec] + cast_out,
        out_shape=[out, out, out] + cast_shapes,
        scratch_shapes=[pltpu.VMEM(w.shape, BF16)],
        compiler_params=pltpu.CompilerParams(
            dimension_semantics=("arbitrary",), vmem_limit_bytes=VMEM_LIMIT),
        name="qkv",
    )(x, w, cos, sin, *cast_weights)


def _attn_kernel(lq1_ref, lk1_ref, lq2_ref, lk2_ref, g_ref, q_ref, k_ref, v_ref, o_ref,
                 qs_ref, m_ref, acc_ref, *, lambda_init):
    th = ATT_TH
    tq = 2 * th
    lane = lax.broadcasted_iota(jnp.int32, (1, LANES), 1)
    is_map1 = lane < DIFF_HEAD_DIM
    lam = (jnp.exp(jnp.sum(lq1_ref[...] * lk1_ref[...], axis=-1, keepdims=True))
           - jnp.exp(jnp.sum(lq2_ref[...] * lk2_ref[...], axis=-1, keepdims=True)) + lambda_init)
    gain = g_ref[...] * (1.0 - lambda_init)
    zero = jnp.zeros((), BF16)

    def chunk_mask(n_rows):
        row = lax.broadcasted_iota(jnp.int32, (n_rows, th), 0) % th
        return row // CHUNK >= lax.broadcasted_iota(jnp.int32, (n_rows, th), 1) // CHUNK

    diag = chunk_mask(2 * th)

    def update(h, rows, key_start, n_keys, visible, first=False):
        ks = pl.ds(key_start, n_keys)
        hs = slice(h * LANES, (h + 1) * LANES)
        s = lax.dot_general(qs_ref[h, rows, :], k_ref[0, ks, hs], (((1,), (1,)), ((), ())),
                            preferred_element_type=F32)
        if visible is not None:
            n = visible.shape[0]
            masked = jnp.where(visible, s[:n], NEG_INF)
            s = masked if n == s.shape[0] else jnp.concatenate([masked, s[n:]], axis=0)
        m_new = jnp.broadcast_to(jnp.max(s, axis=-1, keepdims=True), (s.shape[0], LANES))
        if not first:
            m_old = m_ref[h, rows, :]
            m_new = jnp.maximum(m_old, m_new)
            alpha = jnp.exp2(m_old - m_new)
        e = jnp.exp2(s - jnp.concatenate([m_new] * (n_keys // LANES), axis=1)).astype(BF16)
        v_ext = jnp.concatenate([v_ref[0, ks, hs], jnp.ones((n_keys, LANES), BF16)], axis=1)
        pv = _dot(e, v_ext)
        if first:
            acc_ref[h, rows, :] = pv
        else:
            acc_ref[h, rows, :] = jnp.concatenate([alpha, alpha], axis=1) * acc_ref[h, rows, :] + pv
        m_ref[h, rows, :] = m_new

    all_rows = slice(0, 4 * th)
    second_half = slice(2 * th, 4 * th)

    def q_block(i, carry):
        q0 = pl.multiple_of(i * tq, tq)
        for h in range(ATT_HP):
            for half in range(2):
                q = q_ref[0, pl.ds(q0 + half * th, th), h * LANES:(h + 1) * LANES]
                qs_ref[h, (2 * half) * th:(2 * half + 1) * th, :] = jnp.where(is_map1, q, zero)
                qs_ref[h, (2 * half + 1) * th:(2 * half + 2) * th, :] = jnp.where(is_map1, zero, q)
        for h in range(ATT_HP):
            update(h, all_rows, q0, th, diag, first=True)
            update(h, second_half, q0 + th, th, diag)

        def off_diag(j, c):
            for h in range(ATT_HP):
                update(h, all_rows, pl.multiple_of(j * tq, tq), tq, None)
            return c

        lax.fori_loop(0, i, off_diag, 0)

        for h in range(ATT_HP):
            for half in range(2):
                a1 = acc_ref[h, (2 * half) * th:(2 * half + 1) * th, :]
                a2 = acc_ref[h, (2 * half + 1) * th:(2 * half + 2) * th, :]
                o = (a1[:, :LANES] * (1.0 / a1[:, LANES:])
                     - a2[:, :LANES] * (lam * (1.0 / a2[:, LANES:])))
                o = o * lax.rsqrt(jnp.mean(o * o, axis=-1, keepdims=True) + LN_EPS) * gain
                o_ref[0, pl.ds(q0 + half * th, th), h * LANES:(h + 1) * LANES] = o.astype(BF16)
        return carry

    lax.fori_loop(0, SEQ // tq, q_block, 0)


def _attn_call(lq1, lk1, lq2, lk2, g, q, k, v, lambda_init):
    const = lambda b, h: (0, 0)
    small = lambda a: pl.BlockSpec(a.shape, const)
    head_spec = pl.BlockSpec((1, SEQ, ATT_HP * LANES), lambda b, h: (b, 0, h))
    return pl.pallas_call(
        functools.partial(_attn_kernel, lambda_init=lambda_init),
        grid=(BATCH, DIFF_HEADS // ATT_HP),
        in_specs=[small(lq1), small(lk1), small(lq2), small(lk2), small(g),
                  head_spec, head_spec, head_spec],
        out_specs=head_spec,
        out_shape=jax.ShapeDtypeStruct((BATCH, SEQ, V_WIDTH), BF16),
        scratch_shapes=[pltpu.VMEM((ATT_HP, 4 * ATT_TH, LANES), BF16),
                        pltpu.VMEM((ATT_HP, 4 * ATT_TH, LANES), F32),
                        pltpu.VMEM((ATT_HP, 4 * ATT_TH, 2 * LANES), F32)],
        compiler_params=pltpu.CompilerParams(
            dimension_semantics=("arbitrary", "arbitrary"), vmem_limit_bytes=VMEM_LIMIT),
        name="diff_attn",
    )(lq1, lk1, lq2, lk2, g, q, k, v)


def _block_diag(w):
    same_block = np.eye(LRU_BLOCKS, dtype=np.float32)[:, None, :, None]
    return (w[:, :, None, :] * same_block).reshape(B_WIDTH, B_WIDTH)


def _rope_tables():
    half_dim = DIFF_HEAD_DIM // 2
    pos = np.arange(SEQ, dtype=np.float64)
    inv_freq = ROPE_THETA ** (-np.arange(0, DIFF_HEAD_DIM, 2, dtype=np.float64) / DIFF_HEAD_DIM)
    lane = np.arange(LANES)
    ang = pos[:, None] * inv_freq[lane % half_dim][None, :]
    sign = np.where(lane % DIFF_HEAD_DIM < half_dim, -1.0, 1.0)
    return (jnp.asarray(np.cos(ang), dtype=F32), jnp.asarray(np.sin(ang) * sign[None, :], dtype=F32))


def kernel(x, even_w_in, even_b_in, even_conv_w, even_conv_b, even_cnorm_g, even_cnorm_b,
           even_lru_conv_w, even_lru_conv_b, even_w_a, even_b_a, even_w_x, even_b_x,
           even_lru_lambda, even_w_out, odd_w_qkv, odd_lambda_q1, odd_lambda_k1,
           odd_lambda_q2, odd_lambda_k2, odd_subln_g, odd_w_out, mix_ln_g, mix_ln_b,
           ffn_w_gate, ffn_w_up, ffn_w_down, ffn_ln_g, ffn_ln_b):
    row = lambda a: a.reshape(1, -1)
    rows = BATCH * SEQ

    def post(m, xres, weights, layer):
        w_out, wg, wu, wd = weights
        return _post_call(m.reshape(rows, -1), xres.reshape(rows, D_MODEL), w_out,
                          row(mix_ln_g[layer]), row(mix_ln_b[layer]), wg, wu, wd,
                          row(ffn_ln_g[layer]), row(ffn_ln_b[layer]))

    w_gate = jnp.concatenate([_block_diag(even_w_a[0]), _block_diag(even_w_x[0])], axis=1)
    b_gate = jnp.concatenate([even_b_a[0], even_b_x[0]]).reshape(1, -1)
    conv_w = jnp.broadcast_to(even_conv_w[0][:, None, :], (CONV_WIDTH, SUBLANES, A_WIDTH))
    m0, *tail0 = _mixer_call(x, even_w_in[0], row(even_b_in[0]), conv_w,
                             row(even_conv_b[0]), row(even_cnorm_g[0]), row(even_cnorm_b[0]),
                             even_lru_conv_w[0], row(even_lru_conv_b[0]), w_gate.astype(BF16),
                             b_gate, row(even_lru_lambda[0]),
                             [even_w_out, ffn_w_gate, ffn_w_up, ffn_w_down], [0, 0, 0, 0])
    x1 = post(m0, x, tail0, 0)

    lambda_init = 0.8 - 0.6 * math.exp(-0.3 * 1)
    cos, sin = _rope_tables()
    q, k, v, *tail1 = _qkv_call(x1, odd_w_qkv[0], cos, sin,
                                [odd_w_out, ffn_w_gate, ffn_w_up, ffn_w_down], [0, 1, 1, 1])
    shape3 = (BATCH, SEQ, QK_WIDTH)
    o = _attn_call(row(odd_lambda_q1[0]), row(odd_lambda_k1[0]), row(odd_lambda_q2[0]),
                   row(odd_lambda_k2[0]), row(odd_subln_g[0]),
                   q.reshape(shape3), k.reshape(shape3), v.reshape(shape3), lambda_init)
    out = post(o, x1, tail1, 1)
    return out.reshape(BATCH, SEQ, D_MODEL)
```

```python
import functools
import math

import jax
import jax.numpy as jnp
import numpy as np
from jax import lax
from jax.experimental import pallas as pl
from jax.experimental.pallas import tpu as pltpu

F32 = jnp.float32
BF16 = jnp.bfloat16

D_MODEL = 1024
BATCH = 8
SEQ = 2048
DEPTH = 2
CHUNK = 64
A_WIDTH = 512
B_WIDTH = 512
CONV_WIDTH = 31
LRU_BLOCKS = 8
LRU_CONV_WIDTH = 4
LRU_C = 8.0
IN_WIDTH = 2 * A_WIDTH + 2 * B_WIDTH
DIFF_HEADS = 8
DIFF_HEAD_DIM = 64
QK_WIDTH = 1024
V_WIDTH = 1024
ROPE_THETA = 10000.0
D_FF = 2816
LN_EPS = 1e-5
DN_ALPHA = (2 * DEPTH) ** 0.25
NEG_INF = -1e30
Q_SCALE = DIFF_HEAD_DIM ** -0.5 * math.log2(math.e)

LANES = 128
SUBLANES = 8
BF16_SUBLANES = 16
MXU_COLS = 256
VMEM_LIMIT = 56 * 1024 * 1024

MIX_TS = 512
CONV_HALO = 32
CONV_ROWS = 32
CONV_BLK = 64
LRU_HALO = 8
LRU_ROW_BLOCKS = 2
POST_TM = 1024
FF_CHUNK = 256
POST_ROW_BLOCKS = 4
QKV_TM = 1024
ATT_HP = 8
ATT_TH = 256


def _ln(x, g, b):
    mu = jnp.mean(x, axis=-1, keepdims=True)
    xc = x - mu
    var = jnp.mean(xc * xc, axis=-1, keepdims=True)
    return xc * lax.rsqrt(var + LN_EPS) * g + b


def _dot(a, b):
    return jnp.dot(a, b, preferred_element_type=F32)


N_CAST = 4


def _cast_slabs(cast_in, cast_out):
    for src, dst in zip(cast_in, cast_out):
        dst[...] = src[...].astype(BF16)


def _cast_specs(weights, layers, step_of, n_steps):
    in_specs, out_specs, out_shapes = [], [], []
    for w, layer in zip(weights, layers):
        _, rows, cols = w.shape
        slab = rows // n_steps
        per_slab = 1
        while (slab * per_slab) % BF16_SUBLANES:
            per_slab *= 2
        slab *= per_slab
        in_specs.append(pl.BlockSpec(
            (None, slab, cols), lambda *g, l=layer, p=per_slab: (l, step_of(*g) // p, 0)))
        out_specs.append(pl.BlockSpec(
            (slab, cols), lambda *g, p=per_slab: (step_of(*g) // p, 0)))
        out_shapes.append(jax.ShapeDtypeStruct((rows, cols), BF16))
    return in_specs, out_specs, out_shapes


def _mixer_kernel(x_ref, w_in_f32_ref, b_in_ref, conv_w_ref, conv_b_ref, cn_g_ref, cn_b_ref,
                  lconv_w_ref, lconv_b_ref, w_gate_ref, b_gate_ref, lam_ref, *rest):
    cast_in, rest = rest[:N_CAST], rest[N_CAST:]
    out_ref, rest = rest[0], rest[1:]
    cast_out, rest = rest[:N_CAST], rest[N_CAST:]
    (a_ext, a_sh, b_ext, h_carry, bgate_ref, gates_ref, au_ref, xb_ref, conv_ref,
     w_in_ref) = rest
    ts = MIX_TS
    s = pl.program_id(1)

    n_groups = A_WIDTH // LANES
    _cast_slabs(cast_in, cast_out)

    @pl.when((pl.program_id(0) == 0) & (s == 0))
    def _():
        w_in_ref[...] = w_in_f32_ref[...].astype(BF16)

    @pl.when(s == 0)
    def _():
        a_ext[:, 0:CONV_HALO, :] = jnp.zeros((n_groups, CONV_HALO, LANES), F32)
        a_ext[:, CONV_HALO + ts:, :] = jnp.zeros((n_groups, CONV_BLK, LANES), F32)
        b_ext[0:LRU_HALO, :] = jnp.zeros((LRU_HALO, B_WIDTH), F32)
        h_carry[...] = jnp.zeros_like(h_carry)

    xb_ref[...] = x_ref[0].astype(BF16)
    ha = _dot(xb_ref[...], w_in_ref[:, 0:2 * A_WIDTH]) + b_in_ref[:, 0:2 * A_WIDTH]
    glu = ha[:, 0:A_WIDTH] * jax.nn.sigmoid(ha[:, A_WIDTH:])
    for g in range(n_groups):
        a_ext[g, CONV_HALO:CONV_HALO + ts, :] = glu[:, g * LANES:(g + 1) * LANES]

    off = CONV_HALO - (CONV_WIDTH - 1)

    for g in range(n_groups):
        def shift_body(i, carry, g=g):
            base = pl.multiple_of(i * CONV_BLK, CONV_BLK)
            win = a_ext[g, pl.ds(base, CONV_BLK + SUBLANES), :]
            for r in range(1, SUBLANES):
                rolled = pltpu.roll(win, CONV_BLK + SUBLANES - r, axis=0)
                a_sh[r - 1, g, pl.ds(base, CONV_BLK), :] = rolled[0:CONV_BLK, :]
            return carry

        lax.fori_loop(0, (ts + CONV_BLK) // CONV_BLK, shift_body, 0, unroll=3)

    for g in range(n_groups):
        sl = slice(g * LANES, (g + 1) * LANES)
        taps = [conv_w_ref[j, :, sl] for j in range(CONV_WIDTH)]
        bias = conv_b_ref[:, sl]

        def conv_body(i, carry, g=g, sl=sl, taps=taps, bias=bias):
            base = pl.multiple_of(i * CONV_BLK, CONV_BLK)
            acc = jnp.broadcast_to(bias, (CONV_BLK, LANES))
            for j in range(CONV_WIDTH):
                q, r = divmod(off + j, SUBLANES)
                rows = pl.ds(base + q * SUBLANES, CONV_BLK)
                win = a_ext[g, rows, :] if r == 0 else a_sh[r - 1, g, rows, :]
                acc = acc + (win.reshape(-1, SUBLANES, LANES) * taps[j][None]).reshape(CONV_BLK, LANES)
            conv_ref[pl.ds(base, CONV_BLK), sl] = acc
            return carry

        lax.fori_loop(0, ts // CONV_BLK, conv_body, 0, unroll=4)
    a_ext[:, 0:CONV_HALO, :] = a_ext[:, ts:ts + CONV_HALO, :]

    def norm_body(i, carry):
        base = pl.multiple_of(i * CONV_ROWS, CONV_ROWS)
        ya = jax.nn.silu(_ln(conv_ref[pl.ds(base, CONV_ROWS), :], cn_g_ref[...], cn_b_ref[...]))
        out_ref[0, pl.ds(base, CONV_ROWS), 0:A_WIDTH] = ya.astype(BF16)
        return carry

    lax.fori_loop(0, ts // CONV_ROWS, norm_body, 0, unroll=True)

    rec0 = 2 * A_WIDTH + B_WIDTH
    loff = LRU_HALO - (LRU_CONV_WIDTH - 1)
    rblk = ts // LRU_ROW_BLOCKS
    for rb in range(LRU_ROW_BLOCKS):
        r0 = rb * rblk
        b_ext[LRU_HALO + r0:LRU_HALO + r0 + rblk, :] = (
            _dot(xb_ref[r0:r0 + rblk, :], w_in_ref[:, rec0:]) + b_in_ref[:, rec0:])
    xcs = []
    for rb in range(LRU_ROW_BLOCKS):
        r0 = rb * rblk
        xc = jnp.broadcast_to(lconv_b_ref[...], (rblk, B_WIDTH))
        for j in range(LRU_CONV_WIDTH):
            xc = xc + lconv_w_ref[j:j + 1, :] * b_ext[loff + j + r0:loff + j + r0 + rblk, :]
        xcs.append(xc)
        xcb = xc.astype(BF16)
        for blk in range(B_WIDTH // MXU_COLS):
            bs = slice(blk * MXU_COLS, (blk + 1) * MXU_COLS)
            for gate in range(2):
                cs = slice(gate * B_WIDTH + blk * MXU_COLS, gate * B_WIDTH + (blk + 1) * MXU_COLS)
                gates_ref[r0:r0 + rblk, cs] = _dot(xcb[:, bs], w_gate_ref[bs, cs]) + b_gate_ref[:, cs]
    b_ext[0:LRU_HALO, :] = b_ext[ts:ts + LRU_HALO, :]
    for rb, xc in enumerate(xcs):
        b_ext[LRU_HALO + rb * rblk:LRU_HALO + (rb + 1) * rblk, :] = xc
    bgate_ref[...] = _dot(xb_ref[...], w_in_ref[:, 2 * A_WIDTH:rec0]) + b_in_ref[:, 2 * A_WIDTH:rec0]

    def scan_steps(a_cum, u_cum, axis, steps):
        idx = lax.broadcasted_iota(jnp.int32, a_cum.shape, axis)
        for step in steps:
            keep = idx >= step
            a_prev = jnp.where(keep, pltpu.roll(a_cum, step, axis=axis), 1.0)
            u_prev = jnp.where(keep, pltpu.roll(u_cum, step, axis=axis), 0.0)
            u_cum = a_cum * u_prev + u_cum
            a_cum = a_cum * a_prev
        return a_cum, u_cum

    nblk = ts // SUBLANES
    for g in range(B_WIDTH // LANES):
        sl = slice(g * LANES, (g + 1) * LANES)
        xg = b_ext[LRU_HALO:LRU_HALO + ts, sl]
        gate_r = jax.nn.sigmoid(gates_ref[:, sl])
        gate_i = jax.nn.sigmoid(gates_ref[:, B_WIDTH + g * LANES:B_WIDTH + (g + 1) * LANES])
        lam = lam_ref[:, sl]
        neg = -lam
        softplus = jnp.maximum(neg, 0.0) + jnp.log1p(jnp.exp(-jnp.abs(neg)))
        log_a = (-LRU_C * gate_r) * softplus
        th = jnp.tanh(log_a)
        y = -2.0 * th
        scale = jnp.where(y > 0.0, y * lax.rsqrt(y), 0.0) * lax.rsqrt(1.0 - th)
        a_blk, u_blk = scan_steps(jnp.exp(log_a).reshape(nblk, SUBLANES, LANES),
                                  (scale * (gate_i * xg)).reshape(nblk, SUBLANES, LANES),
                                  1, (1, 2, 4))
        au_ref[0] = a_blk.reshape(ts, LANES)
        au_ref[1] = u_blk.reshape(ts, LANES)
        last = pl.ds(SUBLANES - 1, nblk, stride=SUBLANES)
        a_end, u_end = scan_steps(au_ref[0, last, :], au_ref[1, last, :], 0,
                                  [1 << k for k in range(nblk.bit_length() - 1)])
        h_prev = h_carry[0:1, sl]
        h_end = a_end * h_prev + u_end
        h_carry[:, sl] = jnp.broadcast_to(h_end[nblk - 1:nblk, :], (SUBLANES, LANES))
        row0 = lax.broadcasted_iota(jnp.int32, (nblk, LANES), 0) == 0
        h_in = jnp.where(row0, h_prev, pltpu.roll(h_end, 1, axis=0))
        hg = (a_blk * h_in[:, None, :] + u_blk).reshape(ts, LANES)
        yb = hg * jax.nn.gelu(bgate_ref[:, sl])
        out_ref[0, :, A_WIDTH + g * LANES:A_WIDTH + (g + 1) * LANES] = yb.astype(BF16)


def _mixer_call(x, w_in, b_in, conv_w, conv_b, cn_g, cn_b, lconv_w, lconv_b, w_gate, b_gate, lam,
                cast_weights, cast_layers):
    ts = MIX_TS
    seq_steps = SEQ // ts
    full = lambda a: pl.BlockSpec(a.shape, lambda b, s, nd=a.ndim: (0,) * nd,
                                  pipeline_mode=pl.Buffered(1))
    cast_in, cast_out, cast_shapes = _cast_specs(
        cast_weights, cast_layers, lambda b, s: b * seq_steps + s, BATCH * seq_steps)
    return pl.pallas_call(
        _mixer_kernel,
        grid=(BATCH, seq_steps),
        in_specs=[pl.BlockSpec((1, ts, D_MODEL), lambda b, s: (b, s, 0)),
                  full(w_in), full(b_in), full(conv_w), full(conv_b), full(cn_g), full(cn_b),
                  full(lconv_w), full(lconv_b), full(w_gate), full(b_gate), full(lam)] + cast_in,
        out_specs=[pl.BlockSpec((1, ts, A_WIDTH + B_WIDTH), lambda b, s: (b, s, 0))] + cast_out,
        out_shape=[jax.ShapeDtypeStruct((BATCH, SEQ, A_WIDTH + B_WIDTH), BF16)] + cast_shapes,
        scratch_shapes=[pltpu.VMEM((A_WIDTH // LANES, CONV_HALO + ts + CONV_BLK, LANES), F32),
                        pltpu.VMEM((SUBLANES - 1, A_WIDTH // LANES, ts + CONV_BLK, LANES), F32),
                        pltpu.VMEM((LRU_HALO + ts, B_WIDTH), F32),
                        pltpu.VMEM((SUBLANES, B_WIDTH), F32),
                        pltpu.VMEM((ts, B_WIDTH), F32),
                        pltpu.VMEM((ts, 2 * B_WIDTH), F32),
                        pltpu.VMEM((2, ts, LANES), F32),
                        pltpu.VMEM((ts, D_MODEL), BF16),
                        pltpu.VMEM((ts, A_WIDTH), F32),
                        pltpu.VMEM((D_MODEL, IN_WIDTH), BF16)],
        compiler_params=pltpu.CompilerParams(
            dimension_semantics=("arbitrary", "arbitrary"), vmem_limit_bytes=VMEM_LIMIT),
        name="mixer0",
    )(x, w_in, b_in, conv_w, conv_b, cn_g, cn_b, lconv_w, lconv_b, w_gate, b_gate, lam,
      *cast_weights)


def _post_kernel(m_ref, x_ref, w_out_ref, g1_ref, b1_ref, wg_ref, wu_ref, wd_ref, g2_ref, b2_ref,
                 out_ref, x1_ref, xb_ref, act_ref):
    blk = POST_TM // POST_ROW_BLOCKS
    for rb in range(POST_ROW_BLOCKS):
        rs = slice(rb * blk, (rb + 1) * blk)
        y = _dot(m_ref[rs, :], w_out_ref[...])
        x1 = _ln(DN_ALPHA * x_ref[rs, :] + y, g1_ref[...], b1_ref[...])
        x1_ref[rs, :] = x1
        xb_ref[rs, :] = x1.astype(BF16)
    for c in range(D_FF // FF_CHUNK):
        cs = slice(c * FF_CHUNK, (c + 1) * FF_CHUNK)
        gate = _dot(xb_ref[...], wg_ref[:, cs])
        up = _dot(xb_ref[...], wu_ref[:, cs])
        act_ref[:, cs] = (jax.nn.silu(gate) * up).astype(BF16)
    for rb in range(POST_ROW_BLOCKS):
        rs = slice(rb * blk, (rb + 1) * blk)
        ffn = _dot(act_ref[rs, :], wd_ref[...])
        out_ref[rs, :] = _ln(DN_ALPHA * x1_ref[rs, :] + ffn, g2_ref[...], b2_ref[...])


def _post_call(m, x, w_out, g1, b1, wg, wu, wd, g2, b2):
    tm = POST_TM
    rows = m.shape[0]
    const = lambda i: (0, 0)
    full = lambda a: pl.BlockSpec(a.shape, const, pipeline_mode=pl.Buffered(1))
    return pl.pallas_call(
        _post_kernel,
        grid=(rows // tm,),
        in_specs=[pl.BlockSpec((tm, D_MODEL), lambda i: (i, 0)),
                  pl.BlockSpec((tm, D_MODEL), lambda i: (i, 0)),
                  full(w_out), full(g1), full(b1), full(wg), full(wu), full(wd),
                  full(g2), full(b2)],
        out_specs=pl.BlockSpec((tm, D_MODEL), lambda i: (i, 0)),
        out_shape=jax.ShapeDtypeStruct((rows, D_MODEL), F32),
        scratch_shapes=[pltpu.VMEM((tm, D_MODEL), F32),
                        pltpu.VMEM((tm, D_MODEL), BF16),
                        pltpu.VMEM((tm, D_FF), BF16)],
        compiler_params=pltpu.CompilerParams(
            dimension_semantics=("arbitrary",), vmem_limit_bytes=VMEM_LIMIT),
        name="post",
    )(m, x, w_out, g1, b1, wg, wu, wd, g2, b2)


def _qkv_kernel(x_ref, w_f32_ref, cos_ref, sin_ref, *rest):
    cast_in, rest = rest[:N_CAST], rest[N_CAST:]
    q_ref, k_ref, v_ref = rest[:3]
    cast_out, (w_ref,) = rest[3:3 + N_CAST], rest[3 + N_CAST:]
    _cast_slabs(cast_in, cast_out)

    @pl.when(pl.program_id(0) == 0)
    def _():
        w_ref[...] = w_f32_ref[...].astype(BF16)

    xb = x_ref[...].astype(BF16)
    cos = cos_ref[...]
    sin = sin_ref[...]
    half_dim = DIFF_HEAD_DIM // 2
    lane = lax.broadcasted_iota(jnp.int32, (1, LANES), 1)
    first_half = (lane % DIFF_HEAD_DIM) < half_dim
    for g in range(2 * QK_WIDTH // MXU_COLS):
        t2 = _dot(xb, w_ref[:, g * MXU_COLS:(g + 1) * MXU_COLS])
        for half in range(MXU_COLS // LANES):
            t = t2[:, half * LANES:(half + 1) * LANES]
            rot = jnp.where(first_half, pltpu.roll(t, LANES - half_dim, axis=1),
                            pltpu.roll(t, half_dim, axis=1))
            r = t * cos + rot * sin
            col = g * MXU_COLS + half * LANES
            if col < QK_WIDTH:
                q_ref[:, col:col + LANES] = (r * Q_SCALE).astype(BF16)
            else:
                k_ref[:, col - QK_WIDTH:col - QK_WIDTH + LANES] = r.astype(BF16)
    v_ref[...] = _dot(xb, w_ref[:, 2 * QK_WIDTH:]).astype(BF16)


def _qkv_call(x, w, cos, sin, cast_weights, cast_layers):
    tm = QKV_TM
    rows = x.shape[0]
    pos_blocks = SEQ // tm
    row_spec = pl.BlockSpec((tm, D_MODEL), lambda i: (i, 0))
    tab_spec = pl.BlockSpec((tm, LANES), lambda i: (i % pos_blocks, 0))
    out = jax.ShapeDtypeStruct((rows, QK_WIDTH), BF16)
    cast_in, cast_out, cast_shapes = _cast_specs(cast_weights, cast_layers, lambda i: i,
                                                 rows // tm)
    return pl.pallas_call(
        _qkv_kernel,
        grid=(rows // tm,),
        in_specs=[row_spec,
                  pl.BlockSpec(w.shape, lambda i: (0, 0), pipeline_mode=pl.Buffered(1)),
                  tab_spec, tab_spec] + cast_in,
        out_specs=[row_spec, row_spec, row_spec] + cast_out,
        out_shape=[out, out, out] + cast_shapes,
        scratch_shapes=[pltpu.VMEM(w.shape, BF16)],
        compiler_params=pltpu.CompilerParams(
            dimension_semantics=("arbitrary",), vmem_limit_bytes=VMEM_LIMIT),
        name="qkv",
    )(x, w, cos, sin, *cast_weights)


def _attn_kernel(lq1_ref, lk1_ref, lq2_ref, lk2_ref, g_ref, q_ref, k_ref, v_ref, o_ref,
                 qs_ref, m_ref, acc_ref, *, lambda_init):
    th = ATT_TH
    tq = 2 * th
    lane = lax.broadcasted_iota(jnp.int32, (1, LANES), 1)
    is_map1 = lane < DIFF_HEAD_DIM
    lam = (jnp.exp(jnp.sum(lq1_ref[...] * lk1_ref[...], axis=-1, keepdims=True))
           - jnp.exp(jnp.sum(lq2_ref[...] * lk2_ref[...], axis=-1, keepdims=True)) + lambda_init)
    gain = g_ref[...] * (1.0 - lambda_init)
    zero = jnp.zeros((), BF16)

    def chunk_mask(n_rows):
        row = lax.broadcasted_iota(jnp.int32, (n_rows, th), 0) % th
        return row // CHUNK >= lax.broadcasted_iota(jnp.int32, (n_rows, th), 1) // CHUNK

    diag = chunk_mask(2 * th)

    def update(h, rows, key_start, n_keys, visible, first=False):
        ks = pl.ds(key_start, n_keys)
        hs = slice(h * LANES, (h + 1) * LANES)
        s = lax.dot_general(qs_ref[h, rows, :], k_ref[0, ks, hs], (((1,), (1,)), ((), ())),
                            preferred_element_type=F32)
        if visible is not None:
            n = visible.shape[0]
            masked = jnp.where(visible, s[:n], NEG_INF)
            s = masked if n == s.shape[0] else jnp.concatenate([masked, s[n:]], axis=0)
        m_new = jnp.broadcast_to(jnp.max(s, axis=-1, keepdims=True), (s.shape[0], LANES))
        if not first:
            m_old = m_ref[h, rows, :]
            m_new = jnp.maximum(m_old, m_new)
            alpha = jnp.exp2(m_old - m_new)
        e = jnp.exp2(s - jnp.concatenate([m_new] * (n_keys // LANES), axis=1)).astype(BF16)
        v_ext = jnp.concatenate([v_ref[0, ks, hs], jnp.ones((n_keys, LANES), BF16)], axis=1)
        pv = _dot(e, v_ext)
        if first:
            acc_ref[h, rows, :] = pv
        else:
            acc_ref[h, rows, :] = jnp.concatenate([alpha, alpha], axis=1) * acc_ref[h, rows, :] + pv
        m_ref[h, rows, :] = m_new

    all_rows = slice(0, 4 * th)
    second_half = slice(2 * th, 4 * th)

    def q_block(i, carry):
        q0 = pl.multiple_of(i * tq, tq)
        for h in range(ATT_HP):
            for half in range(2):
                q = q_ref[0, pl.ds(q0 + half * th, th), h * LANES:(h + 1) * LANES]
                qs_ref[h, (2 * half) * th:(2 * half + 1) * th, :] = jnp.where(is_map1, q, zero)
                qs_ref[h, (2 * half + 1) * th:(2 * half + 2) * th, :] = jnp.where(is_map1, zero, q)
        for h in range(ATT_HP):
            update(h, all_rows, q0, th, diag, first=True)
            update(h, second_half, q0 + th, th, diag)

        def off_diag(j, c):
            for h in range(ATT_HP):
                update(h, all_rows, pl.multiple_of(j * tq, tq), tq, None)
            return c

        lax.fori_loop(0, i, off_diag, 0)

        for h in range(ATT_HP):
            for half in range(2):
                a1 = acc_ref[h, (2 * half) * th:(2 * half + 1) * th, :]
                a2 = acc_ref[h, (2 * half + 1) * th:(2 * half + 2) * th, :]
                o = (a1[:, :LANES] * (1.0 / a1[:, LANES:])
                     - a2[:, :LANES] * (lam * (1.0 / a2[:, LANES:])))
                o = o * lax.rsqrt(jnp.mean(o * o, axis=-1, keepdims=True) + LN_EPS) * gain
                o_ref[0, pl.ds(q0 + half * th, th), h * LANES:(h + 1) * LANES] = o.astype(BF16)
        return carry

    lax.fori_loop(0, SEQ // tq, q_block, 0)


def _attn_call(lq1, lk1, lq2, lk2, g, q, k, v, lambda_init):
    const = lambda b, h: (0, 0)
    small = lambda a: pl.BlockSpec(a.shape, const)
    head_spec = pl.BlockSpec((1, SEQ, ATT_HP * LANES), lambda b, h: (b, 0, h))
    return pl.pallas_call(
        functools.partial(_attn_kernel, lambda_init=lambda_init),
        grid=(BATCH, DIFF_HEADS // ATT_HP),
        in_specs=[small(lq1), small(lk1), small(lq2), small(lk2), small(g),
                  head_spec, head_spec, head_spec],
        out_specs=head_spec,
        out_shape=jax.ShapeDtypeStruct((BATCH, SEQ, V_WIDTH), BF16),
        scratch_shapes=[pltpu.VMEM((ATT_HP, 4 * ATT_TH, LANES), BF16),
                        pltpu.VMEM((ATT_HP, 4 * ATT_TH, LANES), F32),
                        pltpu.VMEM((ATT_HP, 4 * ATT_TH, 2 * LANES), F32)],
        compiler_params=pltpu.CompilerParams(
            dimension_semantics=("arbitrary", "arbitrary"), vmem_limit_bytes=VMEM_LIMIT),
        name="diff_attn",
    )(lq1, lk1, lq2, lk2, g, q, k, v)


def _block_diag(w):
    same_block = np.eye(LRU_BLOCKS, dtype=np.float32)[:, None, :, None]
    return (w[:, :, None, :] * same_block).reshape(B_WIDTH, B_WIDTH)


def _rope_tables():
    half_dim = DIFF_HEAD_DIM // 2
    pos = np.arange(SEQ, dtype=np.float64)
    inv_freq = ROPE_THETA ** (-np.arange(0, DIFF_HEAD_DIM, 2, dtype=np.float64) / DIFF_HEAD_DIM)
    lane = np.arange(LANES)
    ang = pos[:, None] * inv_freq[lane % half_dim][None, :]
    sign = np.where(lane % DIFF_HEAD_DIM < half_dim, -1.0, 1.0)
    return (jnp.asarray(np.cos(ang), dtype=F32), jnp.asarray(np.sin(ang) * sign[None, :], dtype=F32))


def kernel(x, even_w_in, even_b_in, even_conv_w, even_conv_b, even_cnorm_g, even_cnorm_b,
           even_lru_conv_w, even_lru_conv_b, even_w_a, even_b_a, even_w_x, even_b_x,
           even_lru_lambda, even_w_out, odd_w_qkv, odd_lambda_q1, odd_lambda_k1,
           odd_lambda_q2, odd_lambda_k2, odd_subln_g, odd_w_out, mix_ln_g, mix_ln_b,
           ffn_w_gate, ffn_w_up, ffn_w_down, ffn_ln_g, ffn_ln_b):
    row = lambda a: a.reshape(1, -1)
    rows = BATCH * SEQ

    def post(m, xres, weights, layer):
        w_out, wg, wu, wd = weights
        return _post_call(m.reshape(rows, -1), xres.reshape(rows, D_MODEL), w_out,
                          row(mix_ln_g[layer]), row(mix_ln_b[layer]), wg, wu, wd,
                          row(ffn_ln_g[layer]), row(ffn_ln_b[layer]))

    w_gate = jnp.concatenate([_block_diag(even_w_a[0]), _block_diag(even_w_x[0])], axis=1)
    b_gate = jnp.concatenate([even_b_a[0], even_b_x[0]]).reshape(1, -1)
    conv_w = jnp.broadcast_to(even_conv_w[0][:, None, :], (CONV_WIDTH, SUBLANES, A_WIDTH))
    m0, *tail0 = _mixer_call(x, even_w_in[0], row(even_b_in[0]), conv_w,
                             row(even_conv_b[0]), row(even_cnorm_g[0]), row(even_cnorm_b[0]),
                             even_lru_conv_w[0], row(even_lru_conv_b[0]), w_gate.astype(BF16),
                             b_gate, row(even_lru_lambda[0]),
                             [even_w_out, ffn_w_gate, ffn_w_up, ffn_w_down], [0, 0, 0, 0])
    x1 = post(m0, x, tail0, 0)

    lambda_init = 0.8 - 0.6 * math.exp(-0.3 * 1)
    cos, sin = _rope_tables()
    q, k, v, *tail1 = _qkv_call(x1, odd_w_qkv[0], cos, sin,
                                [odd_w_out, ffn_w_gate, ffn_w_up, ffn_w_down], [0, 1, 1, 1])
    shape3 = (BATCH, SEQ, QK_WIDTH)
    o = _attn_call(row(odd_lambda_q1[0]), row(odd_lambda_k1[0]), row(odd_lambda_q2[0]),
                   row(odd_lambda_k2[0]), row(odd_subln_g[0]),
                   q.reshape(shape3), k.reshape(shape3), v.reshape(shape3), lambda_init)
    out = post(o, x1, tail1, 1)
    return out.reshape(BATCH, SEQ, D_MODEL)
```
